```python
import math
import jax
import jax.numpy as jnp
from jax import lax
import numpy as np

D_MODEL = 2048
BATCH = 4
SEQ = 2048
DEPTH = 2

GRID_W = 64
CTX_LEN = 256
EPS = 1e-6
F32 = jnp.float32

MIX_WIDTH = D_MODEL
GROUP_W = MIX_WIDTH // 4
SC_CONV = 3
CF_CONV = 31
SG_HEADS = 4
SG_CHUNK = 128
SG_HEAD_DIM = GROUP_W // SG_HEADS
SSD_HEAD_DIM = 64
SSD_HEADS = GROUP_W // SSD_HEAD_DIM
SSD_GROUPS = 2
SSD_STATE = 128
SSD_CHUNK = 128
SSD_CONV = 3
SSD_XBC = GROUP_W + 2 * SSD_GROUPS * SSD_STATE
N_EXPERTS = 32
TOP_K = 4
D_FF = D_MODEL
SWIGLU_LIMIT = 7.0
SWIGLU_ALPHA = 1.702
MOE_BLOCK = 128

OFF_CF = 3 * GROUP_W
OFF_SG = OFF_CF + 2 * GROUP_W
OFF_SSD = OFF_SG + 2 * GROUP_W
IN_COLS = OFF_SSD + GROUP_W + SSD_XBC + 2 * SSD_HEADS

kernel_name = "hybrid_parallel_mixer_moe_dit"


def rmsnorm(x, g):
    xf = x.astype(F32)
    y = xf * lax.rsqrt(jnp.mean(xf * xf, axis=-1, keepdims=True) + EPS)
    return (y * g.astype(F32)).astype(x.dtype)


def layernorm(x, g, b):
    xf = x.astype(F32)
    xc = xf - jnp.mean(xf, axis=-1, keepdims=True)
    y = xc * lax.rsqrt(jnp.mean(xc * xc, axis=-1, keepdims=True) + EPS)
    return (y * g.astype(F32) + b.astype(F32)).astype(x.dtype)


def dwconv(u, w, b, grid):
    bsz, length, ch = u.shape
    if grid:
        rows = length // GRID_W
        u = u.reshape(bsz * rows, GRID_W, ch)
    width = w.shape[0]
    y = lax.conv_general_dilated(u, w[:, None, :].astype(u.dtype), (1,), [(width // 2, width // 2)],
                                 dimension_numbers=('NWC', 'WIO', 'NWC'), feature_group_count=ch)
    if b is not None:
        y = y + b
    return y.reshape(bsz, length, ch)


def short_conv_mixer(p, conv_w, grid):
    gate_b, gate_c, v = jnp.split(p, 3, axis=-1)
    return gate_b * dwconv(gate_c * v, conv_w, None, grid)


def conformer_conv_mixer(p, conv_w, conv_b, ln_g, ln_b, grid):
    a, g = jnp.split(p, 2, axis=-1)
    u = dwconv(a * jax.nn.sigmoid(g), conv_w, conv_b, grid)
    return jax.nn.silu(layernorm(u, ln_g, ln_b))


def spatial_gating_mixer(p, ln_g, ln_b, w_s, b_s):
    bsz, length, _ = p.shape
    u, v = jnp.split(jax.nn.gelu(p, approximate=False), 2, axis=-1)
    v = layernorm(v, ln_g, ln_b).reshape(bsz, length // SG_CHUNK, SG_CHUNK, SG_HEADS, SG_HEAD_DIM)
    s = jnp.einsum('hts,bcshd->bcthd', w_s, v) + jnp.swapaxes(b_s, 0, 1)[:, :, None]
    return u * s.reshape(bsz, length, GROUP_W)


def local_mixers(p, grid, sc_conv_w, cf_conv_w, cf_conv_b, cf_ln_g, cf_ln_b, sg_ln_g, sg_ln_b, sg_w, sg_b):
    return [short_conv_mixer(p[..., :OFF_CF], sc_conv_w, grid),
            conformer_conv_mixer(p[..., OFF_CF:OFF_SG], cf_conv_w, cf_conv_b, cf_ln_g, cf_ln_b, grid),
            spatial_gating_mixer(p[..., OFF_SG:OFF_SSD], sg_ln_g, sg_ln_b, sg_w, sg_b)]


def ssd_prepare(p, conv_w, conv_b, grid):
    bsz, length, _ = p.shape
    z, xbc, dt = jnp.split(p, [GROUP_W, GROUP_W + SSD_XBC], axis=-1)
    xbc = jax.nn.silu(dwconv(xbc, conv_w, conv_b, grid)).astype(F32)
    xs, bm, cm = jnp.split(xbc, [GROUP_W, GROUP_W + SSD_GROUPS * SSD_STATE], axis=-1)
    xs = xs.reshape(bsz, length, SSD_HEADS, SSD_HEAD_DIM)
    rep = lambda t: jnp.repeat(t.reshape(bsz, length, SSD_GROUPS, SSD_STATE), SSD_HEADS // SSD_GROUPS, axis=2)
    return z, xs, rep(bm), rep(cm), dt.astype(F32).reshape(bsz, length, 2, SSD_HEADS)


def ssd_chunk_states(x, dt, a, bm, init):
    bsz, length, h, p = x.shape
    nc = length // SSD_CHUNK
    xd = (x * dt[..., None]).reshape(bsz, nc, SSD_CHUNK, h, p)
    a_cum = jnp.cumsum((dt * a).reshape(bsz, nc, SSD_CHUNK, h), axis=2)
    bm = bm.reshape(bsz, nc, SSD_CHUNK, h, -1)
    decay_to_end = jnp.exp(a_cum[:, :, -1:] - a_cum)
    chunk_states = jnp.einsum('bclhn,bclhp->bchpn', bm * decay_to_end[..., None], xd)
    chunk_decay = jnp.exp(a_cum[:, :, -1])

    def step(s, inp):
        st, dec = inp
        return s * dec[:, :, None, None] + st, s

    final, prev = lax.scan(step, init, (jnp.moveaxis(chunk_states, 1, 0), jnp.moveaxis(chunk_decay, 1, 0)))
    return jnp.moveaxis(prev, 0, 1), final, xd, a_cum


def ssd_chunk_outputs(cm, bm, xd, a_cum, prev):
    bsz, nc, l, h, p = xd.shape
    cm = cm.reshape(bsz, nc, l, h, -1)
    bm = bm.reshape(bsz, nc, l, h, -1)
    seg = a_cum[:, :, :, None, :] - a_cum[:, :, None, :, :]
    lower = jnp.tril(jnp.ones((l, l), bool))[None, None, :, :, None]
    decay = jnp.exp(jnp.where(lower, seg, -jnp.inf))
    scores = jnp.einsum('bclhn,bcshn->bclsh', cm, bm) * decay
    y = jnp.einsum('bclsh,bcshp->bclhp', scores, xd)
    y = y + jnp.einsum('bclhn,bchpn->bclhp', cm, prev) * jnp.exp(a_cum)[..., None]
    return y.reshape(bsz, nc * l, h, p)


def gated_group_rmsnorm(y, z, g):
    bsz, length = y.shape[:2]
    v = (y.reshape(bsz, length, GROUP_W) * jax.nn.silu(z.astype(F32))).reshape(bsz, length, SSD_GROUPS, -1)
    v = v * lax.rsqrt(jnp.mean(v * v, axis=-1, keepdims=True) + EPS)
    return (v.reshape(bsz, length, GROUP_W) * g.astype(F32)).astype(z.dtype)


def ssd_mixer(p_ctx, p_lat, conv_w, conv_b, dt_bias, a_log, d_skip, norm_g, ctx_out):
    zc, xc, bc, cc, dtc = ssd_prepare(p_ctx, conv_w, conv_b, False)
    zl, xl, bl, cl, dtl = ssd_prepare(p_lat, conv_w, conv_b, True)
    y_lat, y_ctx = None, None
    for d in range(2):
        f = (lambda t: jnp.flip(t, axis=1)) if d == 1 else (lambda t: t)
        a = -jnp.exp(a_log[d].astype(F32))
        bias = dt_bias[d].astype(F32)
        dt_c = f(jax.nn.softplus(dtc[:, :, d] + bias))
        dt_l = f(jax.nn.softplus(dtl[:, :, d] + bias))
        init = jnp.zeros((xc.shape[0], SSD_HEADS, SSD_HEAD_DIM, SSD_STATE), F32)
        bc_d, bl_d = f(bc), f(bl)
        prev_c, state_c, xd_c, acum_c = ssd_chunk_states(f(xc), dt_c, a, bc_d, init)
        prev_l, _, xd_l, acum_l = ssd_chunk_states(f(xl), dt_l, a, bl_d, state_c)
        skip = d_skip[d].astype(F32)[:, None]
        yl = f(ssd_chunk_outputs(f(cl), bl_d, xd_l, acum_l, prev_l)) + skip * xl
        y_lat = yl if y_lat is None else y_lat + yl
        if ctx_out:
            yc = f(ssd_chunk_outputs(f(cc), bc_d, xd_c, acum_c, prev_c)) + skip * xc
            y_ctx = yc if y_ctx is None else y_ctx + yc
    out_ctx = gated_group_rmsnorm(y_ctx, zc, norm_g) if ctx_out else None
    return gated_group_rmsnorm(y_lat, zl, norm_g), out_ctx


def moe_ffn(h, w_router, b_router, w_gate_up, b_gate_up, w_down, b_down):
    n, d = h.shape
    logits = h.astype(F32) @ w_router.astype(F32) + b_router.astype(F32)
    top_logit, top_e = lax.top_k(logits, TOP_K)
    gates = jax.nn.softmax(top_logit, axis=-1).astype(h.dtype)
    slots = n * TOP_K
    flat_e = top_e.reshape(-1)
    order = jnp.argsort(flat_e)
    sorted_e = flat_e[order]
    counts = jnp.bincount(flat_e, length=N_EXPERTS)
    padded = (counts + MOE_BLOCK - 1) // MOE_BLOCK * MOE_BLOCK
    pad_end = jnp.cumsum(padded)
    pad_start = pad_end - padded
    start = jnp.cumsum(counts) - counts
    dest = pad_start[sorted_e] + jnp.arange(slots) - start[sorted_e]
    n_blocks = -(-(slots + N_EXPERTS * (MOE_BLOCK - 1)) // MOE_BLOCK)
    buf_tok = jnp.full((n_blocks * MOE_BLOCK,), n, jnp.int32).at[dest].set((order // TOP_K).astype(jnp.int32))
    block_e = jnp.minimum(jnp.searchsorted(pad_end, jnp.arange(n_blocks) * MOE_BLOCK, side='right'), N_EXPERTS - 1)
    h_pad = jnp.concatenate([h, jnp.zeros((1, d), h.dtype)], axis=0)
    xb = h_pad[buf_tok].reshape(n_blocks, MOE_BLOCK, d)

    def expert_block(args):
        xe, e = args
        gu = xe @ w_gate_up[e] + b_gate_up[e]
        g, u = jnp.split(gu, 2, axis=-1)
        g = jnp.minimum(g, SWIGLU_LIMIT)
        u = jnp.clip(u, -SWIGLU_LIMIT, SWIGLU_LIMIT)
        return ((u + 1) * (g * jax.nn.sigmoid(SWIGLU_ALPHA * g))) @ w_down[e] + b_down[e]

    yb = lax.map(expert_block, (xb, block_e)).reshape(-1, d)
    y_slot = jnp.zeros((slots, d), h.dtype).at[order].set(yb[dest])
    return jnp.einsum('nk,nkd->nd', gates, y_slot.reshape(n, TOP_K, d))


def hybrid_layer(x_lat, x_ctx, c, c_ctx, w_mod, b_mod, norm1_g, norm2_g, w_in, b_in,
                 sc_conv_w, cf_conv_w, cf_conv_b, cf_ln_g, cf_ln_b, sg_ln_g, sg_ln_b, sg_w, sg_b,
                 ssd_conv_w, ssd_conv_b, ssd_dt_bias, ssd_a_log, ssd_d, ssd_norm_g,
                 w_out, b_out, w_router, b_router, w_gate_up, b_gate_up, w_down, b_down, last):
    sh1, sc1, gt1, sh2, sc2, gt2 = jnp.split((jax.nn.silu(c) @ w_mod + b_mod)[:, None, :], 6, axis=-1)
    csh1, csc1, cgt1, csh2, csc2, cgt2 = jnp.split(jax.nn.silu(c_ctx) @ w_mod + b_mod, 6, axis=-1)
    local = lambda p, grid: local_mixers(p, grid, sc_conv_w, cf_conv_w, cf_conv_b, cf_ln_g, cf_ln_b,
                                         sg_ln_g, sg_ln_b, sg_w, sg_b)
    h_lat = rmsnorm(x_lat, norm1_g) * (1 + sc1) + sh1
    h_ctx = rmsnorm(x_ctx, norm1_g) * (1 + csc1) + csh1
    p_lat = h_lat @ w_in + b_in
    if last:
        p_ctx_ssd = h_ctx @ w_in[:, OFF_SSD:] + b_in[OFF_SSD:]
    else:
        p_ctx = h_ctx @ w_in + b_in
        p_ctx_ssd = p_ctx[..., OFF_SSD:]
    ssd_lat, ssd_ctx = ssd_mixer(p_ctx_ssd, p_lat[..., OFF_SSD:], ssd_conv_w, ssd_conv_b, ssd_dt_bias,
                                 ssd_a_log, ssd_d, ssd_norm_g, not last)
    m_lat = jnp.concatenate(local(p_lat, True) + [ssd_lat], axis=-1)
    x_lat = x_lat + gt1 * (m_lat @ w_out + b_out)
    f_lat = rmsnorm(x_lat, norm2_g) * (1 + sc2) + sh2
    if last:
        y = moe_ffn(f_lat.reshape(-1, D_MODEL), w_router, b_router, w_gate_up, b_gate_up, w_down, b_down)
        return x_lat + gt2 * y.reshape(x_lat.shape), None
    m_ctx = jnp.concatenate(local(p_ctx, False) + [ssd_ctx], axis=-1)
    x_ctx = x_ctx + cgt1 * (m_ctx @ w_out + b_out)
    f_ctx = rmsnorm(x_ctx, norm2_g) * (1 + csc2) + csh2
    n_ctx = f_ctx.shape[0] * f_ctx.shape[1]
    y = moe_ffn(jnp.concatenate([f_ctx.reshape(-1, D_MODEL), f_lat.reshape(-1, D_MODEL)], axis=0),
                w_router, b_router, w_gate_up, b_gate_up, w_down, b_down)
    x_ctx = x_ctx + cgt2 * y[:n_ctx].reshape(x_ctx.shape)
    x_lat = x_lat + gt2 * y[n_ctx:].reshape(x_lat.shape)
    return x_lat, x_ctx


def setup_inputs(seed: int = 0) -> dict:
    key = jax.random.key(seed)
    ks = iter(jax.random.split(key, 48))
    L, D, GW = DEPTH, D_MODEL, GROUP_W

    def nrm(shape, scale):
        return jax.random.normal(next(ks), shape, jnp.float32) * scale

    def gain(shape):
        return 1.0 + nrm(shape, 0.05)

    dt0 = jnp.exp(jax.random.uniform(next(ks), (L, 2, SSD_HEADS), jnp.float32,
                                     minval=math.log(1e-3), maxval=math.log(1e-1)))
    dt_bias = dt0 + jnp.log(-jnp.expm1(-dt0))
    a_log = jnp.log(jax.random.uniform(next(ks), (L, 2, SSD_HEADS), jnp.float32, minval=1.0, maxval=16.0))
    return {
        "x": nrm((BATCH, SEQ, D), 1.0),
        "c": nrm((BATCH, D), 1.0),
        "ctx": nrm((BATCH, CTX_LEN, D), 1.0),
        "c_ctx": nrm((D,), 1.0),
        "w_mod": nrm((L, D, 6 * D), 0.3 * D ** -0.5),
        "b_mod": nrm((L, 6 * D), 0.02),
        "norm1_g": gain((L, D)),
        "norm2_g": gain((L, D)),
        "w_in": nrm((L, D, IN_COLS), D ** -0.5),
        "b_in": nrm((L, IN_COLS), 0.02),
        "sc_conv_w": nrm((L, SC_CONV, GW), SC_CONV ** -0.5),
        "cf_conv_w": nrm((L, CF_CONV, GW), CF_CONV ** -0.5),
        "cf_conv_b": nrm((L, GW), 0.02),
        "cf_ln_g": gain((L, GW)),
        "cf_ln_b": nrm((L, GW), 0.02),
        "sg_ln_g": gain((L, GW)),
        "sg_ln_b": nrm((L, GW), 0.02),
        "sg_w": nrm((L, SG_HEADS, SG_CHUNK, SG_CHUNK), SG_CHUNK ** -0.5),
        "sg_b": 1.0 + nrm((L, SG_HEADS, SG_CHUNK), 0.1),
        "ssd_conv_w": nrm((L, SSD_CONV, SSD_XBC), SSD_CONV ** -0.5),
        "ssd_conv_b": nrm((L, SSD_XBC), 0.02),
        "ssd_dt_bias": dt_bias,
        "ssd_a_log": a_log,
        "ssd_d": gain((L, 2, SSD_HEADS)),
        "ssd_norm_g": gain((L, GW)),
        "w_out": nrm((L, MIX_WIDTH, D), MIX_WIDTH ** -0.5),
        "b_out": nrm((L, D), 0.02),
        "w_router": nrm((L, D, N_EXPERTS), D ** -0.5),
        "b_router": nrm((L, N_EXPERTS), 0.01),
        "w_gate_up": nrm((L, N_EXPERTS, D, 2 * D_FF), D ** -0.5),
        "b_gate_up": nrm((L, N_EXPERTS, 2 * D_FF), 0.02),
        "w_down": nrm((L, N_EXPERTS, D_FF, D), D_FF ** -0.5),
        "b_down": nrm((L, N_EXPERTS, D), 0.02),
        "final_norm_g": gain((D,)),
    }


def reference(x, c, ctx, c_ctx, w_mod, b_mod, norm1_g, norm2_g, w_in, b_in,
              sc_conv_w, cf_conv_w, cf_conv_b, cf_ln_g, cf_ln_b, sg_ln_g, sg_ln_b, sg_w, sg_b,
              ssd_conv_w, ssd_conv_b, ssd_dt_bias, ssd_a_log, ssd_d, ssd_norm_g,
              w_out, b_out, w_router, b_router, w_gate_up, b_gate_up, w_down, b_down, final_norm_g):
    x_lat, x_ctx = x, ctx
    for i in range(DEPTH):
        x_lat, x_ctx = hybrid_layer(
            x_lat, x_ctx, c, c_ctx, w_mod[i], b_mod[i], norm1_g[i], norm2_g[i], w_in[i], b_in[i],
            sc_conv_w[i], cf_conv_w[i], cf_conv_b[i], cf_ln_g[i], cf_ln_b[i], sg_ln_g[i], sg_ln_b[i], sg_w[i], sg_b[i],
            ssd_conv_w[i], ssd_conv_b[i], ssd_dt_bias[i], ssd_a_log[i], ssd_d[i], ssd_norm_g[i],
            w_out[i], b_out[i], w_router[i], b_router[i], w_gate_up[i], b_gate_up[i], w_down[i], b_down[i],
            i == DEPTH - 1)
    return rmsnorm(x_lat, final_norm_g)
```

```python
import functools

import jax
import jax.numpy as jnp
from jax import lax
from jax.experimental import pallas as pl
from jax.experimental.pallas import tpu as pltpu

F32 = jnp.float32
BF16 = jnp.bfloat16

D_MODEL = 2048
BATCH = 4
SEQ = 2048
DEPTH = 2
GRID_W = 64
CTX_LEN = 256
EPS = 1e-6
GROUP_W = 512
SG_HEADS = 4
SG_CHUNK = 128
SG_HEAD_DIM = 128
SSD_HEAD_DIM = 64
SSD_HEADS = 8
SSD_GROUPS = 2
SSD_STATE = 128
SSD_CHUNK = 128
SSD_XBC = 1024
N_EXPERTS = 32
TOP_K = 4
D_FF = 2048
SWIGLU_LIMIT = 7.0
SWIGLU_ALPHA = 1.702
OFF_CF = 1536
OFF_SG = 2560
OFF_SSD = 3584
MAIN_COLS = 5120
DT_COLS = 2 * SSD_HEADS
LANE = 128

N_CTX = BATCH * CTX_LEN
N_LAT = BATCH * SEQ

VMEM_LIMIT = 56 * 1024 * 1024

MOD_ROWS = 8
MOD_TN = 1024


def _mod_kernel(c_ref, w_ref, b_ref, o_ref):
    c = c_ref[...]
    s = c * jax.nn.sigmoid(c)
    o_ref[...] = jnp.dot(s.astype(BF16), w_ref[...].astype(BF16), preferred_element_type=F32) + b_ref[...]


def _modulation(cc, w_mod, b_mod):
    n_out = 6 * D_MODEL
    return pl.pallas_call(
        _mod_kernel,
        grid=(DEPTH, n_out // MOD_TN),
        in_specs=[
            pl.BlockSpec((MOD_ROWS, D_MODEL), lambda l, n: (0, 0)),
            pl.BlockSpec((None, D_MODEL, MOD_TN), lambda l, n: (l, 0, n)),
            pl.BlockSpec((None, 1, MOD_TN), lambda l, n: (l, 0, n)),
        ],
        out_specs=pl.BlockSpec((None, MOD_ROWS, MOD_TN), lambda l, n: (l, 0, n)),
        out_shape=jax.ShapeDtypeStruct((DEPTH, MOD_ROWS, n_out), F32),
        compiler_params=pltpu.CompilerParams(
            dimension_semantics=("arbitrary", "arbitrary"), vmem_limit_bytes=VMEM_LIMIT),
        name="adaln_mod",
    )(cc, w_mod, b_mod.reshape(DEPTH, 1, n_out))


def _mod_row(tile, tile_rows, ctx_tiles):
    tiles_per_batch = SEQ // tile_rows
    return jnp.where(tile < ctx_tiles, 0, 1 + (tile - ctx_tiles) // tiles_per_batch)


IN_TM = 1024
IN_TN = 512
IN_PRO_ROWS = 256


def _inproj_kernel(x_ref, g_ref, mod_ref, w_ref, b_ref, wdt_ref, bdt_ref, o_ref, odt_ref, h_ref):
    @pl.when(pl.program_id(1) == 0)
    def _():
        g = g_ref[...]
        scale = 1.0 + mod_ref[1:2, :]
        shift = mod_ref[0:1, :]
        for r in range(IN_TM // IN_PRO_ROWS):
            rows = slice(r * IN_PRO_ROWS, (r + 1) * IN_PRO_ROWS)
            x = x_ref[rows, :]
            y = x * lax.rsqrt(jnp.mean(x * x, axis=-1, keepdims=True) + EPS)
            h_ref[rows, :] = ((y * g) * scale + shift).astype(BF16)
        odt_ref[...] = jnp.dot(h_ref[...], wdt_ref[...], preferred_element_type=F32) + bdt_ref[...]

    o_ref[...] = jnp.dot(h_ref[...], w_ref[...], preferred_element_type=F32) + b_ref[...]


def _in_projection(x_all, norm_g, mod, w_bf, b, wdt_bf, bdt, *, row_tile0, n_row_tiles, col_tile0, n_col_tiles,
                   ctx_tiles):
    rows = n_row_tiles * IN_TM
    return pl.pallas_call(
        _inproj_kernel,
        grid=(n_row_tiles, n_col_tiles),
        in_specs=[
            pl.BlockSpec((IN_TM, D_MODEL), lambda m, n: (m + row_tile0, 0)),
            pl.BlockSpec((1, D_MODEL), lambda m, n: (0, 0)),
            pl.BlockSpec((None, 6, D_MODEL), lambda m, n: (_mod_row(m + row_tile0, IN_TM, ctx_tiles), 0, 0)),
            pl.BlockSpec((D_MODEL, IN_TN), lambda m, n: (0, n + col_tile0)),
            pl.BlockSpec((1, IN_TN), lambda m, n: (0, n + col_tile0)),
            pl.BlockSpec((D_MODEL, LANE), lambda m, n: (0, 0)),
            pl.BlockSpec((1, LANE), lambda m, n: (0, 0)),
        ],
        out_specs=[
            pl.BlockSpec((IN_TM, IN_TN), lambda m, n: (m, n)),
            pl.BlockSpec((IN_TM, LANE), lambda m, n: (m, 0)),
        ],
        out_shape=[
            jax.ShapeDtypeStruct((rows, n_col_tiles * IN_TN), F32),
            jax.ShapeDtypeStruct((rows, LANE), F32),
        ],
        scratch_shapes=[pltpu.VMEM((IN_TM, D_MODEL), BF16)],
        compiler_params=pltpu.CompilerParams(
            dimension_semantics=("arbitrary", "arbitrary"), vmem_limit_bytes=VMEM_LIMIT),
        name="in_proj",
    )(x_all, norm_g.reshape(1, D_MODEL), mod, w_bf, b, wdt_bf, bdt)


OUT_TM = 256


def _outproj_kernel(m_ref, x_ref, mod_ref, g_ref, w_ref, b_ref, wr_ref, br_ref, xo_ref, f_ref, lg_ref):
    y = jnp.dot(m_ref[...].astype(BF16), w_ref[...], preferred_element_type=F32) + b_ref[...]
    xn = x_ref[...] + mod_ref[2:3, :] * y
    xo_ref[...] = xn
    r = lax.rsqrt(jnp.mean(xn * xn, axis=-1, keepdims=True) + EPS)
    f = ((xn * r) * g_ref[...]) * (1.0 + mod_ref[4:5, :]) + mod_ref[3:4, :]
    fb = f.astype(BF16)
    f_ref[...] = fb
    lg_ref[...] = jnp.dot(fb, wr_ref[...], preferred_element_type=F32) + br_ref[...]


def _out_projection(m, x_all, mod, norm_g, w_bf, b, wr_bf, br, *, row_tile0, ctx_tiles):
    rows = m.shape[0]
    n_tiles = rows // OUT_TM
    return pl.pallas_call(
        _outproj_kernel,
        grid=(n_tiles,),
        in_specs=[
            pl.BlockSpec((OUT_TM, D_MODEL), lambda t: (t, 0)),
            pl.BlockSpec((OUT_TM, D_MODEL), lambda t: (t + row_tile0, 0)),
            pl.BlockSpec((None, 6, D_MODEL), lambda t: (_mod_row(t + row_tile0, OUT_TM, ctx_tiles), 0, 0)),
            pl.BlockSpec((1, D_MODEL), lambda t: (0, 0)),
            pl.BlockSpec((D_MODEL, D_MODEL), lambda t: (0, 0)),
            pl.BlockSpec((1, D_MODEL), lambda t: (0, 0)),
            pl.BlockSpec((D_MODEL, LANE), lambda t: (0, 0)),
            pl.BlockSpec((1, LANE), lambda t: (0, 0)),
        ],
        out_specs=[
            pl.BlockSpec((OUT_TM, D_MODEL), lambda t: (t, 0)),
            pl.BlockSpec((OUT_TM, D_MODEL), lambda t: (t, 0)),
            pl.BlockSpec((OUT_TM, LANE), lambda t: (t, 0)),
        ],
        out_shape=[
            jax.ShapeDtypeStruct((rows, D_MODEL), F32),
            jax.ShapeDtypeStruct((rows, D_MODEL), BF16),
            jax.ShapeDtypeStruct((rows, LANE), F32),
        ],
        compiler_params=pltpu.CompilerParams(
            dimension_semantics=("arbitrary",), vmem_limit_bytes=VMEM_LIMIT),
        name="out_proj",
    )(m, x_all, mod, norm_g.reshape(1, D_MODEL), w_bf, b, wr_bf, br)


MOE_SUB = 256
MOE_CHUNK = 2048
MOE_NSUB = MOE_CHUNK // MOE_SUB
MOE_TF = 256
MOE_F_STEPS = D_FF // MOE_TF
MOE_N_STEPS = D_MODEL // MOE_TF
MOE_STEPS = MOE_F_STEPS + MOE_N_STEPS


def _moe_rows(n_tokens):
    slots = n_tokens * TOP_K
    padded = slots + N_EXPERTS * (MOE_SUB - 1)
    padded = -(-padded // MOE_SUB) * MOE_SUB
    return padded, padded


def _moe_chunks(n_tokens):
    padded, _ = _moe_rows(n_tokens)
    return padded // MOE_CHUNK + N_EXPERTS


def _moe_kernel(ce_ref, cs_ref, cn_ref, xb_hbm, wg_ref, wu_ref, bg_ref, bu_ref, wd_ref, bd_ref, yb_hbm,
                xs, hs, os_, sem_in, sem_out):
    c = pl.program_id(0)
    j = pl.program_id(1)
    nsub = cn_ref[c]
    start = pl.multiple_of(cs_ref[c], MOE_SUB)

    def copy_in(i):
        return pltpu.make_async_copy(
            xb_hbm.at[pl.ds(start + i * MOE_SUB, MOE_SUB), :],
            xs.at[pl.ds(i * MOE_SUB, MOE_SUB), :],
            sem_in.at[0])

    def copy_out(slot, i, col):
        return pltpu.make_async_copy(
            os_.at[slot, pl.ds(i * MOE_SUB, MOE_SUB), :],
            yb_hbm.at[pl.ds(start + i * MOE_SUB, MOE_SUB), pl.ds(col, MOE_TF)],
            sem_out.at[slot])

    def for_valid_subs(fn):
        for i in range(MOE_NSUB):
            @pl.when(i < nsub)
            def _(i=i):
                fn(i)

    @pl.when(nsub > 0)
    def _():
        @pl.when(j == 0)
        def _():
            for_valid_subs(lambda i: copy_in(i).start())
            for_valid_subs(lambda i: copy_in(i).wait())

        @pl.when(j < MOE_F_STEPS)
        def _():
            wg = wg_ref[...].astype(BF16)
            wu = wu_ref[...].astype(BF16)
            bg = bg_ref[...]
            bu = bu_ref[...]

            def gate_up(i):
                x = xs[i * MOE_SUB:(i + 1) * MOE_SUB, :]
                g = jnp.dot(x, wg, preferred_element_type=F32) + bg
                u = jnp.dot(x, wu, preferred_element_type=F32) + bu
                g = jnp.minimum(g, SWIGLU_LIMIT)
                u = jnp.clip(u, -SWIGLU_LIMIT, SWIGLU_LIMIT)
                h = (u + 1.0) * (g * jax.nn.sigmoid(SWIGLU_ALPHA * g))
                hs[j, i * MOE_SUB:(i + 1) * MOE_SUB, :] = h.astype(BF16)

            for_valid_subs(gate_up)

        @pl.when(j >= MOE_F_STEPS)
        def _():
            jn = j - MOE_F_STEPS
            slot = jn % 2
            col = pl.multiple_of(jn * MOE_TF, MOE_TF)
            wd = wd_ref[...].astype(BF16)
            bd = bd_ref[...]

            @pl.when(jn >= 2)
            def _():
                for_valid_subs(lambda i: copy_out(slot, i, col).wait())

            def down(i):
                h = jnp.concatenate(
                    [hs[f, i * MOE_SUB:(i + 1) * MOE_SUB, :] for f in range(MOE_F_STEPS)], axis=1)
                os_[slot, i * MOE_SUB:(i + 1) * MOE_SUB, :] = jnp.dot(h, wd, preferred_element_type=F32) + bd
                copy_out(slot, i, col).start()

            for_valid_subs(down)

            @pl.when(jn == MOE_N_STEPS - 1)
            def _():
                for_valid_subs(lambda i: copy_out(1 - slot, i, col).wait())
                for_valid_subs(lambda i: copy_out(slot, i, col).wait())


def _moe_experts(layer, chunk_e, chunk_start, chunk_nsub, xb, w_gate_up, b_gate_up, w_down, b_down):
    rows = xb.shape[0]
    n_chunks = chunk_e.shape[0]

    def gu_idx(half):
        def idx(c, j, ce, cs, cn):
            jj = jnp.where(cn[c] > 0, jnp.minimum(j, MOE_F_STEPS - 1), MOE_F_STEPS - 1)
            return (layer, ce[c], 0, half * MOE_F_STEPS + jj)
        return idx

    def d_idx(c, j, ce, cs, cn):
        jj = jnp.where(cn[c] > 0, jnp.maximum(j - MOE_F_STEPS, 0), MOE_N_STEPS - 1)
        return (layer, ce[c], 0, jj)

    grid_spec = pltpu.PrefetchScalarGridSpec(
        num_scalar_prefetch=3,
        grid=(n_chunks, MOE_STEPS),
        in_specs=[
            pl.BlockSpec(memory_space=pl.ANY),
            pl.BlockSpec((None, None, D_MODEL, MOE_TF), gu_idx(0)),
            pl.BlockSpec((None, None, D_MODEL, MOE_TF), gu_idx(1)),
            pl.BlockSpec((None, None, 1, MOE_TF), gu_idx(0)),
            pl.BlockSpec((None, None, 1, MOE_TF), gu_idx(1)),
            pl.BlockSpec((None, None, D_FF, MOE_TF), d_idx),
            pl.BlockSpec((None, None, 1, MOE_TF), d_idx),
        ],
        out_specs=pl.BlockSpec(memory_space=pl.ANY),
        scratch_shapes=[
            pltpu.VMEM((MOE_CHUNK, D_MODEL), BF16),
            pltpu.VMEM((MOE_F_STEPS, MOE_CHUNK, MOE_TF), BF16),
            pltpu.VMEM((2, MOE_CHUNK, MOE_TF), F32),
            pltpu.SemaphoreType.DMA((1,)),
            pltpu.SemaphoreType.DMA((2,)),
        ],
    )
    bgu = b_gate_up.reshape(DEPTH, N_EXPERTS, 1, 2 * D_FF)
    bd = b_down.reshape(DEPTH, N_EXPERTS, 1, D_MODEL)
    return pl.pallas_call(
        _moe_kernel,
        grid_spec=grid_spec,
        out_shape=jax.ShapeDtypeStruct((rows, D_MODEL), F32),
        compiler_params=pltpu.CompilerParams(
            dimension_semantics=("arbitrary", "arbitrary"), vmem_limit_bytes=VMEM_LIMIT),
        name="moe_experts",
    )(chunk_e, chunk_start, chunk_nsub, xb, w_gate_up, w_gate_up, bgu, bgu, w_down, bd)


def _moe_ffn(layer, f_bf, logits, w_gate_up, b_gate_up, w_down, b_down):
    n = f_bf.shape[0]
    slots = n * TOP_K
    padded_rows, buf_rows = _moe_rows(n)
    n_chunks = _moe_chunks(n)

    top_logit, top_e = lax.top_k(logits, TOP_K)
    gates = jax.nn.softmax(top_logit, axis=-1)
    flat_e = top_e.reshape(-1).astype(jnp.int32)
    order = jnp.argsort(flat_e)
    sorted_e = flat_e[order]
    counts = jnp.bincount(flat_e, length=N_EXPERTS).astype(jnp.int32)
    padded = (counts + MOE_SUB - 1) // MOE_SUB * MOE_SUB
    pad_end = jnp.cumsum(padded)
    pad_start = pad_end - padded
    start = jnp.cumsum(counts) - counts
    dest_sorted = pad_start[sorted_e] + jnp.arange(slots, dtype=jnp.int32) - start[sorted_e]
    buf_tok = jnp.zeros((buf_rows,), jnp.int32).at[dest_sorted].set((order // TOP_K).astype(jnp.int32))
    dest_of_slot = jnp.zeros((slots,), jnp.int32).at[order].set(dest_sorted)

    e_chunks = (padded + MOE_CHUNK - 1) // MOE_CHUNK
    chunk_end = jnp.cumsum(e_chunks)
    total_chunks = chunk_end[-1]
    cidx = jnp.arange(n_chunks, dtype=jnp.int32)
    ce = jnp.minimum(jnp.searchsorted(chunk_end, cidx, side='right'), N_EXPERTS - 1).astype(jnp.int32)
    local = cidx - (chunk_end[ce] - e_chunks[ce])
    valid = cidx < total_chunks
    c_start = jnp.where(valid, pad_start[ce] + local * MOE_CHUNK, 0).astype(jnp.int32)
    c_nsub = jnp.where(valid, jnp.minimum(MOE_CHUNK, padded[ce] - local * MOE_CHUNK) // MOE_SUB, 0).astype(jnp.int32)
    last_e = ce[jnp.maximum(total_chunks - 1, 0)]
    ce = jnp.where(valid, ce, last_e).astype(jnp.int32)

    xb = jnp.take(f_bf, buf_tok, axis=0)
    yb = _moe_experts(layer, ce, c_start, c_nsub, xb, w_gate_up, b_gate_up, w_down, b_down)
    y_slot = jnp.take(yb, dest_of_slot, axis=0).reshape(n, TOP_K, D_MODEL)
    return jnp.einsum('nk,nkd->nd', gates, y_slot)


def _layernorm(x, g, b):
    xc = x - jnp.mean(x, axis=-1, keepdims=True)
    y = xc * lax.rsqrt(jnp.mean(xc * xc, axis=-1, keepdims=True) + EPS)
    return y * g + b


def _dwconv(u, w, b, grid):
    bsz, length, ch = u.shape
    if grid:
        u = u.reshape(bsz * (length // GRID_W), GRID_W, ch)
    width = w.shape[0]
    y = lax.conv_general_dilated(u, w[:, None, :], (1,), [(width // 2, width // 2)],
                                 dimension_numbers=('NWC', 'WIO', 'NWC'), feature_group_count=ch)
    if b is not None:
        y = y + b
    return y.reshape(bsz, length, ch)


def _local_mixers(p, grid, lw):
    gate_b, gate_c, v = jnp.split(p[..., :OFF_CF], 3, axis=-1)
    m_sc = gate_b * _dwconv(gate_c * v, lw['sc_conv_w'], None, grid)
    a, g = jnp.split(p[..., OFF_CF:OFF_SG], 2, axis=-1)
    u = _dwconv(a * jax.nn.sigmoid(g), lw['cf_conv_w'], lw['cf_conv_b'], grid)
    m_cf = jax.nn.silu(_layernorm(u, lw['cf_ln_g'], lw['cf_ln_b']))
    bsz, length, _ = p.shape
    u2, v2 = jnp.split(jax.nn.gelu(p[..., OFF_SG:OFF_SSD], approximate=False), 2, axis=-1)
    v2 = _layernorm(v2, lw['sg_ln_g'], lw['sg_ln_b']).reshape(bsz, length // SG_CHUNK, SG_CHUNK, SG_HEADS, SG_HEAD_DIM)
    s = jnp.einsum('hts,bcshd->bcthd', lw['sg_w'], v2) + jnp.swapaxes(lw['sg_b'], 0, 1)[:, :, None]
    m_sg = u2 * s.reshape(bsz, length, GROUP_W)
    return [m_sc, m_cf, m_sg]


def _ssd_prepare(p, dt, conv_w, conv_b, grid):
    bsz, length, _ = p.shape
    z, xbc = p[..., :GROUP_W], p[..., GROUP_W:]
    xbc = jax.nn.silu(_dwconv(xbc, conv_w, conv_b, grid))
    xs, bm, cm = jnp.split(xbc, [GROUP_W, GROUP_W + SSD_GROUPS * SSD_STATE], axis=-1)
    xs = xs.reshape(bsz, length, SSD_HEADS, SSD_HEAD_DIM)
    rep = lambda t: jnp.repeat(t.reshape(bsz, length, SSD_GROUPS, SSD_STATE), SSD_HEADS // SSD_GROUPS, axis=2)
    return z, xs, rep(bm), rep(cm), dt.reshape(bsz, length, 2, SSD_HEADS)


def _ssd_chunk_states(x, dt, a, bm, init):
    bsz, length, h, p = x.shape
    nc = length // SSD_CHUNK
    xd = (x * dt[..., None]).reshape(bsz, nc, SSD_CHUNK, h, p)
    a_cum = jnp.cumsum((dt * a).reshape(bsz, nc, SSD_CHUNK, h), axis=2)
    bm = bm.reshape(bsz, nc, SSD_CHUNK, h, -1)
    decay_to_end = jnp.exp(a_cum[:, :, -1:] - a_cum)
    chunk_states = jnp.einsum('bclhn,bclhp->bchpn', bm * decay_to_end[..., None], xd)
    chunk_decay = jnp.exp(a_cum[:, :, -1])

    def step(s, inp):
        st, dec = inp
        return s * dec[:, :, None, None] + st, s

    final, prev = lax.scan(step, init, (jnp.moveaxis(chunk_states, 1, 0), jnp.moveaxis(chunk_decay, 1, 0)))
    return jnp.moveaxis(prev, 0, 1), final, xd, a_cum


def _ssd_chunk_outputs(cm, bm, xd, a_cum, prev):
    bsz, nc, l, h, p = xd.shape
    cm = cm.reshape(bsz, nc, l, h, -1)
    bm = bm.reshape(bsz, nc, l, h, -1)
    seg = a_cum[:, :, :, None, :] - a_cum[:, :, None, :, :]
    lower = jnp.tril(jnp.ones((l, l), bool))[None, None, :, :, None]
    decay = jnp.exp(jnp.where(lower, seg, -jnp.inf))
    scores = jnp.einsum('bclhn,bcshn->bclsh', cm, bm) * decay
    y = jnp.einsum('bclsh,bcshp->bclhp', scores, xd)
    y = y + jnp.einsum('bclhn,bchpn->bclhp', cm, prev) * jnp.exp(a_cum)[..., None]
    return y.reshape(bsz, nc * l, h, p)


def _gated_group_rmsnorm(y, z, g):
    bsz, length = y.shape[:2]
    v = (y.reshape(bsz, length, GROUP_W) * jax.nn.silu(z)).reshape(bsz, length, SSD_GROUPS, -1)
    v = v * lax.rsqrt(jnp.mean(v * v, axis=-1, keepdims=True) + EPS)
    return v.reshape(bsz, length, GROUP_W) * g


def _ssd_mixer(p_ctx, dt_ctx, p_lat, dt_lat, lw, ctx_out):
    conv_w, conv_b = lw['ssd_conv_w'], lw['ssd_conv_b']
    zc, xc, bc, cc, dtc = _ssd_prepare(p_ctx, dt_ctx, conv_w, conv_b, False)
    zl, xl, bl, cl, dtl = _ssd_prepare(p_lat, dt_lat, conv_w, conv_b, True)
    y_lat, y_ctx = None, None
    for d in range(2):
        f = (lambda t: jnp.flip(t, axis=1)) if d == 1 else (lambda t: t)
        a = -jnp.exp(lw['ssd_a_log'][d])
        bias = lw['ssd_dt_bias'][d]
        dt_c = f(jax.nn.softplus(dtc[:, :, d] + bias))
        dt_l = f(jax.nn.softplus(dtl[:, :, d] + bias))
        init = jnp.zeros((xc.shape[0], SSD_HEADS, SSD_HEAD_DIM, SSD_STATE), F32)
        bc_d, bl_d = f(bc), f(bl)
        prev_c, state_c, xd_c, acum_c = _ssd_chunk_states(f(xc), dt_c, a, bc_d, init)
        prev_l, _, xd_l, acum_l = _ssd_chunk_states(f(xl), dt_l, a, bl_d, state_c)
        skip = lw['ssd_d'][d][:, None]
        yl = f(_ssd_chunk_outputs(f(cl), bl_d, xd_l, acum_l, prev_l)) + skip * xl
        y_lat = yl if y_lat is None else y_lat + yl
        if ctx_out:
            yc = f(_ssd_chunk_outputs(f(cc), bc_d, xd_c, acum_c, prev_c)) + skip * xc
            y_ctx = yc if y_ctx is None else y_ctx + yc
    out_ctx = _gated_group_rmsnorm(y_ctx, zc, lw['ssd_norm_g']) if ctx_out else None
    return _gated_group_rmsnorm(y_lat, zl, lw['ssd_norm_g']), out_ctx


def kernel(x, c, ctx, c_ctx, w_mod, b_mod, norm1_g, norm2_g, w_in, b_in, sc_conv_w, cf_conv_w, cf_conv_b,
           cf_ln_g, cf_ln_b, sg_ln_g, sg_ln_b, sg_w, sg_b, ssd_conv_w, ssd_conv_b, ssd_dt_bias, ssd_a_log,
           ssd_d, ssd_norm_g, w_out, b_out, w_router, b_router, w_gate_up, b_gate_up, w_down, b_down,
           final_norm_g):
    cc = jnp.concatenate([c_ctx[None, :], c, jnp.zeros((MOD_ROWS - 1 - BATCH, D_MODEL), F32)], axis=0)
    mod_all = _modulation(cc, w_mod, b_mod).reshape(DEPTH, MOD_ROWS, 6, D_MODEL)

    x_all = jnp.concatenate([ctx.reshape(N_CTX, D_MODEL), x.reshape(N_LAT, D_MODEL)], axis=0)

    for i in range(DEPTH):
        last = i == DEPTH - 1
        lw = dict(sc_conv_w=sc_conv_w[i], cf_conv_w=cf_conv_w[i], cf_conv_b=cf_conv_b[i], cf_ln_g=cf_ln_g[i],
                  cf_ln_b=cf_ln_b[i], sg_ln_g=sg_ln_g[i], sg_ln_b=sg_ln_b[i], sg_w=sg_w[i], sg_b=sg_b[i],
                  ssd_conv_w=ssd_conv_w[i], ssd_conv_b=ssd_conv_b[i], ssd_dt_bias=ssd_dt_bias[i],
                  ssd_a_log=ssd_a_log[i], ssd_d=ssd_d[i], ssd_norm_g=ssd_norm_g[i])
        mod = mod_all[i]
        w_in_bf = w_in[i, :, :MAIN_COLS].astype(BF16)
        b_in_main = b_in[i, :MAIN_COLS].reshape(1, MAIN_COLS)
        wdt_bf = jnp.pad(w_in[i, :, MAIN_COLS:], ((0, 0), (0, LANE - DT_COLS))).astype(BF16)
        bdt = jnp.pad(b_in[i, MAIN_COLS:], (0, LANE - DT_COLS)).reshape(1, LANE)
        w_out_bf = w_out[i].astype(BF16)
        wr_bf = jnp.pad(w_router[i], ((0, 0), (0, LANE - N_EXPERTS))).astype(BF16)
        br = jnp.pad(b_router[i], (0, LANE - N_EXPERTS)).reshape(1, LANE)
        in_ctx_tiles = N_CTX // IN_TM
        proj = functools.partial(_in_projection, x_all, norm1_g[i], mod, w_in_bf, b_in_main, wdt_bf, bdt,
                                 ctx_tiles=in_ctx_tiles)

        if not last:
            p_all, dt_all = proj(row_tile0=0, n_row_tiles=(N_CTX + N_LAT) // IN_TM,
                                 col_tile0=0, n_col_tiles=MAIN_COLS // IN_TN)
            p_ctx = p_all[:N_CTX].reshape(BATCH, CTX_LEN, MAIN_COLS)
            p_lat = p_all[N_CTX:].reshape(BATCH, SEQ, MAIN_COLS)
            dt_ctx = dt_all[:N_CTX, :DT_COLS].reshape(BATCH, CTX_LEN, DT_COLS)
            dt_lat = dt_all[N_CTX:, :DT_COLS].reshape(BATCH, SEQ, DT_COLS)
            p_ctx_ssd = p_ctx[..., OFF_SSD:]
        else:
            p_lat, dt_lat = proj(row_tile0=in_ctx_tiles, n_row_tiles=N_LAT // IN_TM,
                                 col_tile0=0, n_col_tiles=MAIN_COLS // IN_TN)
            p_ctx_ssd, dt_ctx = proj(row_tile0=0, n_row_tiles=in_ctx_tiles,
                                     col_tile0=OFF_SSD // IN_TN, n_col_tiles=(MAIN_COLS - OFF_SSD) // IN_TN)
            p_lat = p_lat.reshape(BATCH, SEQ, MAIN_COLS)
            dt_lat = dt_lat[:, :DT_COLS].reshape(BATCH, SEQ, DT_COLS)
            p_ctx_ssd = p_ctx_ssd.reshape(BATCH, CTX_LEN, MAIN_COLS - OFF_SSD)
            dt_ctx = dt_ctx[:, :DT_COLS].reshape(BATCH, CTX_LEN, DT_COLS)

        ssd_lat, ssd_ctx = _ssd_mixer(p_ctx_ssd, dt_ctx, p_lat[..., OFF_SSD:], dt_lat, lw, not last)
        m_lat = jnp.concatenate(_local_mixers(p_lat, True, lw) + [ssd_lat], axis=-1).reshape(N_LAT, D_MODEL)
        out_ctx_tiles = N_CTX // OUT_TM
        if not last:
            m_ctx = jnp.concatenate(_local_mixers(p_ctx, False, lw) + [ssd_ctx], axis=-1).reshape(N_CTX, D_MODEL)
            m_all = jnp.concatenate([m_ctx, m_lat], axis=0)
            row_tile0 = 0
        else:
            m_all = m_lat
            row_tile0 = out_ctx_tiles
        x_mid, f_bf, logits = _out_projection(m_all, x_all, mod, norm2_g[i], w_out_bf, b_out[i].reshape(1, D_MODEL),
                                              wr_bf, br, row_tile0=row_tile0, ctx_tiles=out_ctx_tiles)
        y = _moe_ffn(i, f_bf, logits[:, :N_EXPERTS], w_gate_up, b_gate_up, w_down, b_down)
        if not last:
            gate2 = jnp.concatenate([jnp.broadcast_to(mod[0:1, 5, :], (N_CTX, D_MODEL)),
                                     jnp.repeat(mod[1:1 + BATCH, 5, :], SEQ, axis=0)], axis=0)
            x_all = x_mid + gate2 * y
        else:
            gate2 = jnp.repeat(mod[1:1 + BATCH, 5, :], SEQ, axis=0)
            x_lat = x_mid + gate2 * y

    r = lax.rsqrt(jnp.mean(x_lat * x_lat, axis=-1, keepdims=True) + EPS)
    return (x_lat * r * final_norm_g).reshape(BATCH, SEQ, D_MODEL)
```

```python
import functools

import jax
import jax.numpy as jnp
from jax import lax
from jax.experimental import pallas as pl
from jax.experimental.pallas import tpu as pltpu

F32 = jnp.float32
BF16 = jnp.bfloat16

D_MODEL = 2048
BATCH = 4
SEQ = 2048
DEPTH = 2
GRID_W = 64
CTX_LEN = 256
EPS = 1e-6
GROUP_W = 512
SG_HEADS = 4
SG_CHUNK = 128
SG_HEAD_DIM = 128
SSD_HEAD_DIM = 64
SSD_HEADS = 8
SSD_GROUPS = 2
SSD_STATE = 128
SSD_CHUNK = 128
SSD_XBC = 1024
N_EXPERTS = 32
TOP_K = 4
D_FF = 2048
SWIGLU_LIMIT = 7.0
SWIGLU_ALPHA = 1.702
OFF_CF = 1536
OFF_SG = 2560
OFF_SSD = 3584
MAIN_COLS = 5120
DT_COLS = 2 * SSD_HEADS
LANE = 128
HALF_D = D_MODEL // 2

N_CTX = BATCH * CTX_LEN
N_LAT = BATCH * SEQ

VMEM_LIMIT = 56 * 1024 * 1024

MOD_ROWS = 8
MOD_TN = 1024


def _mod_kernel(c_ref, w_ref, b_ref, o_ref):
    c = c_ref[...]
    s = c * jax.nn.sigmoid(c)
    o_ref[...] = jnp.dot(s.astype(BF16), w_ref[...].astype(BF16), preferred_element_type=F32) + b_ref[...]


def _modulation(cc, w_mod, b_mod):
    n_out = 6 * D_MODEL
    return pl.pallas_call(
        _mod_kernel,
        grid=(DEPTH, n_out // MOD_TN),
        in_specs=[
            pl.BlockSpec((MOD_ROWS, D_MODEL), lambda l, n: (0, 0)),
            pl.BlockSpec((None, D_MODEL, MOD_TN), lambda l, n: (l, 0, n)),
            pl.BlockSpec((None, 1, MOD_TN), lambda l, n: (l, 0, n)),
        ],
        out_specs=pl.BlockSpec((None, MOD_ROWS, MOD_TN), lambda l, n: (l, 0, n)),
        out_shape=jax.ShapeDtypeStruct((DEPTH, MOD_ROWS, n_out), F32),
        compiler_params=pltpu.CompilerParams(
            dimension_semantics=("arbitrary", "arbitrary"), vmem_limit_bytes=VMEM_LIMIT),
        name="adaln_mod",
    )(cc, w_mod, b_mod.reshape(DEPTH, 1, n_out))


def _mod_row(tile, tile_rows, ctx_tiles):
    tiles_per_batch = SEQ // tile_rows
    return jnp.where(tile < ctx_tiles, 0, 1 + (tile - ctx_tiles) // tiles_per_batch)


IN_TM = 1024
IN_TN = 512
IN_PRO_ROWS = 256


def _inproj_kernel(x_ref, g_ref, mod_ref, w_ref, b_ref, wdt_ref, bdt_ref, o_ref, odt_ref, h_ref):
    @pl.when(pl.program_id(1) == 0)
    def _():
        g = g_ref[...]
        scale = 1.0 + mod_ref[1:2, :]
        shift = mod_ref[0:1, :]
        for r in range(IN_TM // IN_PRO_ROWS):
            rows = slice(r * IN_PRO_ROWS, (r + 1) * IN_PRO_ROWS)
            x = x_ref[rows, :]
            y = x * lax.rsqrt(jnp.mean(x * x, axis=-1, keepdims=True) + EPS)
            h_ref[rows, :] = ((y * g) * scale + shift).astype(BF16)
        odt_ref[...] = jnp.dot(h_ref[...], wdt_ref[...], preferred_element_type=F32) + bdt_ref[...]

    o_ref[...] = jnp.dot(h_ref[...], w_ref[...], preferred_element_type=F32) + b_ref[...]


def _in_projection(x_all, norm_g, mod, w_bf, b, wdt_bf, bdt, *, row_tile0, n_row_tiles, col_tile0, n_col_tiles,
                   ctx_tiles):
    rows = n_row_tiles * IN_TM
    return pl.pallas_call(
        _inproj_kernel,
        grid=(n_row_tiles, n_col_tiles),
        in_specs=[
            pl.BlockSpec((IN_TM, D_MODEL), lambda m, n: (m + row_tile0, 0)),
            pl.BlockSpec((1, D_MODEL), lambda m, n: (0, 0)),
            pl.BlockSpec((None, 6, D_MODEL), lambda m, n: (_mod_row(m + row_tile0, IN_TM, ctx_tiles), 0, 0)),
            pl.BlockSpec((D_MODEL, IN_TN), lambda m, n: (0, n + col_tile0)),
            pl.BlockSpec((1, IN_TN), lambda m, n: (0, n + col_tile0)),
            pl.BlockSpec((D_MODEL, LANE), lambda m, n: (0, 0)),
            pl.BlockSpec((1, LANE), lambda m, n: (0, 0)),
        ],
        out_specs=[
            pl.BlockSpec((IN_TM, IN_TN), lambda m, n: (m, n)),
            pl.BlockSpec((IN_TM, LANE), lambda m, n: (m, 0)),
        ],
        out_shape=[
            jax.ShapeDtypeStruct((rows, n_col_tiles * IN_TN), F32),
            jax.ShapeDtypeStruct((rows, LANE), F32),
        ],
        scratch_shapes=[pltpu.VMEM((IN_TM, D_MODEL), BF16)],
        compiler_params=pltpu.CompilerParams(
            dimension_semantics=("arbitrary", "arbitrary"), vmem_limit_bytes=VMEM_LIMIT),
        name="in_proj",
    )(x_all, norm_g.reshape(1, D_MODEL), mod, w_bf, b, wdt_bf, bdt)


OUT_TM = 256


def _outproj_kernel(m_ref, x_ref, mod_ref, g_ref, w_ref, b_ref, wr_ref, br_ref, xo_ref, f_ref, lg_ref):
    y = jnp.dot(m_ref[...].astype(BF16), w_ref[...], preferred_element_type=F32) + b_ref[...]
    xn = x_ref[...] + mod_ref[2:3, :] * y
    xo_ref[...] = xn
    r = lax.rsqrt(jnp.mean(xn * xn, axis=-1, keepdims=True) + EPS)
    f = ((xn * r) * g_ref[...]) * (1.0 + mod_ref[4:5, :]) + mod_ref[3:4, :]
    fb = f.astype(BF16)
    bits = lax.bitcast_convert_type(fb.astype(F32), jnp.uint32)
    f_ref[...] = (bits[:, HALF_D:] & jnp.uint32(0xFFFF0000)) | (bits[:, :HALF_D] >> 16)
    lg_ref[...] = jnp.dot(fb, wr_ref[...], preferred_element_type=F32) + br_ref[...]


def _out_projection(m, x_all, mod, norm_g, w_bf, b, wr_bf, br, *, row_tile0, ctx_tiles):
    rows = m.shape[0]
    n_tiles = rows // OUT_TM
    return pl.pallas_call(
        _outproj_kernel,
        grid=(n_tiles,),
        in_specs=[
            pl.BlockSpec((OUT_TM, D_MODEL), lambda t: (t, 0)),
            pl.BlockSpec((OUT_TM, D_MODEL), lambda t: (t + row_tile0, 0)),
            pl.BlockSpec((None, 6, D_MODEL), lambda t: (_mod_row(t + row_tile0, OUT_TM, ctx_tiles), 0, 0)),
            pl.BlockSpec((1, D_MODEL), lambda t: (0, 0)),
            pl.BlockSpec((D_MODEL, D_MODEL), lambda t: (0, 0)),
            pl.BlockSpec((1, D_MODEL), lambda t: (0, 0)),
            pl.BlockSpec((D_MODEL, LANE), lambda t: (0, 0)),
            pl.BlockSpec((1, LANE), lambda t: (0, 0)),
        ],
        out_specs=[
            pl.BlockSpec((OUT_TM, D_MODEL), lambda t: (t, 0)),
            pl.BlockSpec((OUT_TM, HALF_D), lambda t: (t, 0)),
            pl.BlockSpec((OUT_TM, LANE), lambda t: (t, 0)),
        ],
        out_shape=[
            jax.ShapeDtypeStruct((rows, D_MODEL), F32),
            jax.ShapeDtypeStruct((rows, HALF_D), jnp.uint32),
            jax.ShapeDtypeStruct((rows, LANE), F32),
        ],
        compiler_params=pltpu.CompilerParams(
            dimension_semantics=("arbitrary",), vmem_limit_bytes=VMEM_LIMIT),
        name="out_proj",
    )(m, x_all, mod, norm_g.reshape(1, D_MODEL), w_bf, b, wr_bf, br)


MOE_SUB = 256
MOE_CHUNK = 2048
MOE_NSUB = MOE_CHUNK // MOE_SUB
MOE_TF = 256
MOE_F_STEPS = D_FF // MOE_TF
MOE_N_STEPS = D_MODEL // MOE_TF
MOE_STEPS = MOE_F_STEPS + MOE_N_STEPS


def _moe_rows(n_tokens):
    slots = n_tokens * TOP_K
    padded = slots + N_EXPERTS * (MOE_SUB - 1)
    padded = -(-padded // MOE_SUB) * MOE_SUB
    return padded, padded


def _moe_chunks(n_tokens):
    padded, _ = _moe_rows(n_tokens)
    return padded // MOE_CHUNK + N_EXPERTS


def _moe_kernel(ce_ref, cs_ref, cn_ref, xb_hbm, wg_ref, wu_ref, bg_ref, bu_ref, wd_ref, bd_ref, yb_hbm,
                xw, xs, hs, os_, sem_in, sem_out):
    c = pl.program_id(0)
    j = pl.program_id(1)
    nsub = cn_ref[c]
    start = pl.multiple_of(cs_ref[c], MOE_SUB)

    def copy_in(i):
        return pltpu.make_async_copy(
            xb_hbm.at[pl.ds(start + i * MOE_SUB, MOE_SUB), :],
            xw.at[pl.ds(i * MOE_SUB, MOE_SUB), :],
            sem_in.at[i])

    def unpack(i):
        rows = slice(i * MOE_SUB, (i + 1) * MOE_SUB)
        w = xw[rows, :]
        xs[rows, :HALF_D] = lax.bitcast_convert_type(w << 16, F32).astype(BF16)
        xs[rows, HALF_D:] = lax.bitcast_convert_type(w & jnp.uint32(0xFFFF0000), F32).astype(BF16)

    def copy_out(slot, i, col):
        return pltpu.make_async_copy(
            os_.at[slot, pl.ds(i * MOE_SUB, MOE_SUB), :],
            yb_hbm.at[pl.ds(start + i * MOE_SUB, MOE_SUB), pl.ds(col, MOE_TF)],
            sem_out.at[slot])

    def for_valid_subs(fn):
        for i in range(MOE_NSUB):
            @pl.when(i < nsub)
            def _(i=i):
                fn(i)

    @pl.when(nsub > 0)
    def _():
        @pl.when(j == 0)
        def _():
            for_valid_subs(lambda i: copy_in(i).start())

            def land(i):
                copy_in(i).wait()
                unpack(i)

            for_valid_subs(land)

        @pl.when(j < MOE_F_STEPS)
        def _():
            wg = wg_ref[...].astype(BF16)
            wu = wu_ref[...].astype(BF16)
            bg = bg_ref[...]
            bu = bu_ref[...]

            def gate_up(i):
                x = xs[i * MOE_SUB:(i + 1) * MOE_SUB, :]
                g = jnp.dot(x, wg, preferred_element_type=F32) + bg
                u = jnp.dot(x, wu, preferred_element_type=F32) + bu
                g = jnp.minimum(g, SWIGLU_LIMIT)
                u = jnp.clip(u, -SWIGLU_LIMIT, SWIGLU_LIMIT)
                h = (u + 1.0) * (g * jax.nn.sigmoid(SWIGLU_ALPHA * g))
                hs[j, i * MOE_SUB:(i + 1) * MOE_SUB, :] = h.astype(BF16)

            for_valid_subs(gate_up)

        @pl.when(j >= MOE_F_STEPS)
        def _():
            jn = j - MOE_F_STEPS
            slot = jn % 2
            col = pl.multiple_of(jn * MOE_TF, MOE_TF)
            wd = wd_ref[...].astype(BF16)
            bd = bd_ref[...]

            @pl.when(jn >= 2)
            def _():
                for_valid_subs(lambda i: copy_out(slot, i, col).wait())

            def down(i):
                h = jnp.concatenate(
                    [hs[f, i * MOE_SUB:(i + 1) * MOE_SUB, :] for f in range(MOE_F_STEPS)], axis=1)
                os_[slot, i * MOE_SUB:(i + 1) * MOE_SUB, :] = jnp.dot(h, wd, preferred_element_type=F32) + bd
                copy_out(slot, i, col).start()

            for_valid_subs(down)

            @pl.when(jn == MOE_N_STEPS - 1)
            def _():
                for_valid_subs(lambda i: copy_out(1 - slot, i, col).wait())
                for_valid_subs(lambda i: copy_out(slot, i, col).wait())


def _moe_experts(layer, chunk_e, chunk_start, chunk_nsub, xb, w_gate_up, b_gate_up, w_down, b_down):
    rows = xb.shape[0]
    n_chunks = chunk_e.shape[0]

    def gu_idx(half):
        def idx(c, j, ce, cs, cn):
            jj = jnp.where(cn[c] > 0, jnp.minimum(j, MOE_F_STEPS - 1), MOE_F_STEPS - 1)
            return (layer, ce[c], 0, half * MOE_F_STEPS + jj)
        return idx

    def d_idx(c, j, ce, cs, cn):
        jj = jnp.where(cn[c] > 0, jnp.maximum(j - MOE_F_STEPS, 0), MOE_N_STEPS - 1)
        return (layer, ce[c], 0, jj)

    grid_spec = pltpu.PrefetchScalarGridSpec(
        num_scalar_prefetch=3,
        grid=(n_chunks, MOE_STEPS),
        in_specs=[
            pl.BlockSpec(memory_space=pl.ANY),
            pl.BlockSpec((None, None, D_MODEL, MOE_TF), gu_idx(0)),
            pl.BlockSpec((None, None, D_MODEL, MOE_TF), gu_idx(1)),
            pl.BlockSpec((None, None, 1, MOE_TF), gu_idx(0)),
            pl.BlockSpec((None, None, 1, MOE_TF), gu_idx(1)),
            pl.BlockSpec((None, None, D_FF, MOE_TF), d_idx),
            pl.BlockSpec((None, None, 1, MOE_TF), d_idx),
        ],
        out_specs=pl.BlockSpec(memory_space=pl.ANY),
        scratch_shapes=[
            pltpu.VMEM((MOE_CHUNK, HALF_D), jnp.uint32),
            pltpu.VMEM((MOE_CHUNK, D_MODEL), BF16),
            pltpu.VMEM((MOE_F_STEPS, MOE_CHUNK, MOE_TF), BF16),
            pltpu.VMEM((2, MOE_CHUNK, MOE_TF), F32),
            pltpu.SemaphoreType.DMA((MOE_NSUB,)),
            pltpu.SemaphoreType.DMA((2,)),
        ],
    )
    bgu = b_gate_up.reshape(DEPTH, N_EXPERTS, 1, 2 * D_FF)
    bd = b_down.reshape(DEPTH, N_EXPERTS, 1, D_MODEL)
    return pl.pallas_call(
        _moe_kernel,
        grid_spec=grid_spec,
        out_shape=jax.ShapeDtypeStruct((rows, D_MODEL), F32),
        compiler_params=pltpu.CompilerParams(
            dimension_semantics=("arbitrary", "arbitrary"), vmem_limit_bytes=VMEM_LIMIT),
        name="moe_experts",
    )(chunk_e, chunk_start, chunk_nsub, xb, w_gate_up, w_gate_up, bgu, bgu, w_down, bd)


GATHER_ROWS = 256


def _row_copy_wait(src_hbm, dst, sem, rows):
    pltpu.make_async_copy(src_hbm.at[pl.ds(0, rows)], dst, sem).wait()


def _gather_rows_kernel(nblk_ref, idx_ref, src_hbm, dst_hbm, sem):
    b = pl.program_id(0)
    nblk = nblk_ref[0]

    def block_dst(blk):
        return dst_hbm.at[pl.ds(pl.multiple_of(blk * GATHER_ROWS, GATHER_ROWS), GATHER_ROWS)]

    @pl.when(b < nblk)
    def _():
        base = b * GATHER_ROWS

        def issue(r, carry):
            pltpu.make_async_copy(src_hbm.at[pl.ds(idx_ref[0, r], 1)], dst_hbm.at[pl.ds(base + r, 1)],
                                  sem.at[b % 2]).start()
            return carry

        lax.fori_loop(0, GATHER_ROWS, issue, 0, unroll=8)

        @pl.when(b >= 1)
        def _():
            _row_copy_wait(src_hbm, block_dst(b - 1), sem.at[(b - 1) % 2], GATHER_ROWS)

        @pl.when(b == nblk - 1)
        def _():
            _row_copy_wait(src_hbm, block_dst(b), sem.at[b % 2], GATHER_ROWS)


def _gather_rows(src, idx, n_valid_blocks):
    rows = idx.shape[0]
    n_blocks = rows // GATHER_ROWS
    width = src.shape[1]
    grid_spec = pltpu.PrefetchScalarGridSpec(
        num_scalar_prefetch=1,
        grid=(n_blocks,),
        in_specs=[
            pl.BlockSpec((None, 1, GATHER_ROWS), lambda b, n: (b, 0, 0), memory_space=pltpu.SMEM),
            pl.BlockSpec(memory_space=pl.ANY),
        ],
        out_specs=pl.BlockSpec(memory_space=pl.ANY),
        scratch_shapes=[pltpu.SemaphoreType.DMA((2,))],
    )
    return pl.pallas_call(
        _gather_rows_kernel,
        grid_spec=grid_spec,
        out_shape=jax.ShapeDtypeStruct((rows, width), src.dtype),
        compiler_params=pltpu.CompilerParams(dimension_semantics=("arbitrary",)),
        name="gather_rows",
    )(n_valid_blocks.reshape(1), idx.reshape(n_blocks, 1, GATHER_ROWS), src)


CMB_TM = 256


def _combine_kernel(idx_ref, yb_hbm, gates_ref, x_ref, mod_ref, *rest, final):
    if final:
        g_ref, o_ref, buf, sem = rest
    else:
        o_ref, buf, sem = rest

    for k in range(TOP_K):
        def issue(r, carry, k=k):
            pltpu.make_async_copy(yb_hbm.at[pl.ds(idx_ref[0, k * CMB_TM + r], 1)], buf.at[k, pl.ds(r, 1)],
                                  sem.at[k]).start()
            return carry

        lax.fori_loop(0, CMB_TM, issue, 0, unroll=8)

    gates = gates_ref[...]
    y = None
    for k in range(TOP_K):
        _row_copy_wait(yb_hbm, buf.at[k], sem.at[k], CMB_TM)
        t = gates[:, k:k + 1] * buf[k]
        y = t if y is None else y + t
    xn = x_ref[...] + mod_ref[5:6, :] * y
    if final:
        xn = xn * lax.rsqrt(jnp.mean(xn * xn, axis=-1, keepdims=True) + EPS) * g_ref[...]
    o_ref[...] = xn


def _combine(yb, dest, gates, x_mid, mod, final_g, *, row_tile0, ctx_tiles):
    n = x_mid.shape[0]
    n_tiles = n // CMB_TM
    final = final_g is not None
    idx = dest.reshape(n_tiles, CMB_TM, TOP_K).transpose(0, 2, 1).reshape(n_tiles, 1, TOP_K * CMB_TM)
    in_specs = [
        pl.BlockSpec((None, 1, TOP_K * CMB_TM), lambda t: (t, 0, 0), memory_space=pltpu.SMEM),
        pl.BlockSpec(memory_space=pl.ANY),
        pl.BlockSpec((CMB_TM, TOP_K), lambda t: (t, 0)),
        pl.BlockSpec((CMB_TM, D_MODEL), lambda t: (t, 0)),
        pl.BlockSpec((None, 6, D_MODEL), lambda t: (_mod_row(t + row_tile0, CMB_TM, ctx_tiles), 0, 0)),
    ]
    args = [idx, yb, gates, x_mid, mod]
    if final:
        in_specs.append(pl.BlockSpec((1, D_MODEL), lambda t: (0, 0)))
        args.append(final_g.reshape(1, D_MODEL))
    return pl.pallas_call(
        functools.partial(_combine_kernel, final=final),
        grid=(n_tiles,),
        in_specs=in_specs,
        out_specs=pl.BlockSpec((CMB_TM, D_MODEL), lambda t: (t, 0)),
        out_shape=jax.ShapeDtypeStruct((n, D_MODEL), F32),
        scratch_shapes=[pltpu.VMEM((TOP_K, CMB_TM, D_MODEL), F32), pltpu.SemaphoreType.DMA((TOP_K,))],
        compiler_params=pltpu.CompilerParams(dimension_semantics=("arbitrary",), vmem_limit_bytes=VMEM_LIMIT),
        name="moe_combine",
    )(*args)


def _moe_ffn(layer, f_packed, logits, w_gate_up, b_gate_up, w_down, b_down):
    n = f_packed.shape[0]
    slots = n * TOP_K
    padded_rows, buf_rows = _moe_rows(n)
    n_chunks = _moe_chunks(n)

    top_logit, top_e = lax.top_k(logits, TOP_K)
    gates = jax.nn.softmax(top_logit, axis=-1)
    flat_e = top_e.reshape(-1).astype(jnp.int32)
    order = jnp.argsort(flat_e)
    sorted_e = flat_e[order]
    counts = jnp.bincount(flat_e, length=N_EXPERTS).astype(jnp.int32)
    padded = (counts + MOE_SUB - 1) // MOE_SUB * MOE_SUB
    pad_end = jnp.cumsum(padded)
    pad_start = pad_end - padded
    start = jnp.cumsum(counts) - counts
    dest_sorted = pad_start[sorted_e] + jnp.arange(slots, dtype=jnp.int32) - start[sorted_e]
    buf_tok = jnp.zeros((buf_rows,), jnp.int32).at[dest_sorted].set((order // TOP_K).astype(jnp.int32))
    dest_of_slot = jnp.zeros((slots,), jnp.int32).at[order].set(dest_sorted)

    e_chunks = (padded + MOE_CHUNK - 1) // MOE_CHUNK
    chunk_end = jnp.cumsum(e_chunks)
    total_chunks = chunk_end[-1]
    cidx = jnp.arange(n_chunks, dtype=jnp.int32)
    ce = jnp.minimum(jnp.searchsorted(chunk_end, cidx, side='right'), N_EXPERTS - 1).astype(jnp.int32)
    local = cidx - (chunk_end[ce] - e_chunks[ce])
    valid = cidx < total_chunks
    c_start = jnp.where(valid, pad_start[ce] + local * MOE_CHUNK, 0).astype(jnp.int32)
    c_nsub = jnp.where(valid, jnp.minimum(MOE_CHUNK, padded[ce] - local * MOE_CHUNK) // MOE_SUB, 0).astype(jnp.int32)
    last_e = ce[jnp.maximum(total_chunks - 1, 0)]
    ce = jnp.where(valid, ce, last_e).astype(jnp.int32)

    xb = _gather_rows(f_packed, buf_tok, (pad_end[-1] // GATHER_ROWS).astype(jnp.int32))
    yb = _moe_experts(layer, ce, c_start, c_nsub, xb, w_gate_up, b_gate_up, w_down, b_down)
    return yb, dest_of_slot.reshape(n, TOP_K), gates


def _layernorm(x, g, b):
    xc = x - jnp.mean(x, axis=-1, keepdims=True)
    y = xc * lax.rsqrt(jnp.mean(xc * xc, axis=-1, keepdims=True) + EPS)
    return y * g + b


def _dwconv(u, w, b, grid):
    bsz, length, ch = u.shape
    if grid:
        u = u.reshape(bsz * (length // GRID_W), GRID_W, ch)
    width = w.shape[0]
    y = lax.conv_general_dilated(u, w[:, None, :], (1,), [(width // 2, width // 2)],
                                 dimension_numbers=('NWC', 'WIO', 'NWC'), feature_group_count=ch)
    if b is not None:
        y = y + b
    return y.reshape(bsz, length, ch)


def _local_mixers(p, grid, lw):
    gate_b, gate_c, v = jnp.split(p[..., :OFF_CF], 3, axis=-1)
    m_sc = gate_b * _dwconv(gate_c * v, lw['sc_conv_w'], None, grid)
    a, g = jnp.split(p[..., OFF_CF:OFF_SG], 2, axis=-1)
    u = _dwconv(a * jax.nn.sigmoid(g), lw['cf_conv_w'], lw['cf_conv_b'], grid)
    m_cf = jax.nn.silu(_layernorm(u, lw['cf_ln_g'], lw['cf_ln_b']))
    bsz, length, _ = p.shape
    u2, v2 = jnp.split(jax.nn.gelu(p[..., OFF_SG:OFF_SSD], approximate=False), 2, axis=-1)
    v2 = _layernorm(v2, lw['sg_ln_g'], lw['sg_ln_b']).reshape(bsz, length // SG_CHUNK, SG_CHUNK, SG_HEADS, SG_HEAD_DIM)
    s = jnp.einsum('hts,bcshd->bcthd', lw['sg_w'], v2) + jnp.swapaxes(lw['sg_b'], 0, 1)[:, :, None]
    m_sg = u2 * s.reshape(bsz, length, GROUP_W)
    return [m_sc, m_cf, m_sg]


def _ssd_prepare(p, dt, conv_w, conv_b, grid):
    bsz, length, _ = p.shape
    z, xbc = p[..., :GROUP_W], p[..., GROUP_W:]
    xbc = jax.nn.silu(_dwconv(xbc, conv_w, conv_b, grid))
    xs, bm, cm = jnp.split(xbc, [GROUP_W, GROUP_W + SSD_GROUPS * SSD_STATE], axis=-1)
    xs = xs.reshape(bsz, length, SSD_HEADS, SSD_HEAD_DIM)
    rep = lambda t: jnp.repeat(t.reshape(bsz, length, SSD_GROUPS, SSD_STATE), SSD_HEADS // SSD_GROUPS, axis=2)
    return z, xs, rep(bm), rep(cm), dt.reshape(bsz, length, 2, SSD_HEADS)


def _ssd_chunk_states(x, dt, a, bm, init):
    bsz, length, h, p = x.shape
    nc = length // SSD_CHUNK
    xd = (x * dt[..., None]).reshape(bsz, nc, SSD_CHUNK, h, p)
    a_cum = jnp.cumsum((dt * a).reshape(bsz, nc, SSD_CHUNK, h), axis=2)
    bm = bm.reshape(bsz, nc, SSD_CHUNK, h, -1)
    decay_to_end = jnp.exp(a_cum[:, :, -1:] - a_cum)
    chunk_states = jnp.einsum('bclhn,bclhp->bchpn', bm * decay_to_end[..., None], xd)
    chunk_decay = jnp.exp(a_cum[:, :, -1])

    def step(s, inp):
        st, dec = inp
        return s * dec[:, :, None, None] + st, s

    final, prev = lax.scan(step, init, (jnp.moveaxis(chunk_states, 1, 0), jnp.moveaxis(chunk_decay, 1, 0)))
    return jnp.moveaxis(prev, 0, 1), final, xd, a_cum


def _ssd_chunk_outputs(cm, bm, xd, a_cum, prev):
    bsz, nc, l, h, p = xd.shape
    cm = cm.reshape(bsz, nc, l, h, -1)
    bm = bm.reshape(bsz, nc, l, h, -1)
    seg = a_cum[:, :, :, None, :] - a_cum[:, :, None, :, :]
    lower = jnp.tril(jnp.ones((l, l), bool))[None, None, :, :, None]
    decay = jnp.exp(jnp.where(lower, seg, -jnp.inf))
    scores = jnp.einsum('bclhn,bcshn->bclsh', cm, bm) * decay
    y = jnp.einsum('bclsh,bcshp->bclhp', scores, xd)
    y = y + jnp.einsum('bclhn,bchpn->bclhp', cm, prev) * jnp.exp(a_cum)[..., None]
    return y.reshape(bsz, nc * l, h, p)


def _gated_group_rmsnorm(y, z, g):
    bsz, length = y.shape[:2]
    v = (y.reshape(bsz, length, GROUP_W) * jax.nn.silu(z)).reshape(bsz, length, SSD_GROUPS, -1)
    v = v * lax.rsqrt(jnp.mean(v * v, axis=-1, keepdims=True) + EPS)
    return v.reshape(bsz, length, GROUP_W) * g


def _ssd_mixer(p_ctx, dt_ctx, p_lat, dt_lat, lw, ctx_out):
    conv_w, conv_b = lw['ssd_conv_w'], lw['ssd_conv_b']
    zc, xc, bc, cc, dtc = _ssd_prepare(p_ctx, dt_ctx, conv_w, conv_b, False)
    zl, xl, bl, cl, dtl = _ssd_prepare(p_lat, dt_lat, conv_w, conv_b, True)
    y_lat, y_ctx = None, None
    for d in range(2):
        f = (lambda t: jnp.flip(t, axis=1)) if d == 1 else (lambda t: t)
        a = -jnp.exp(lw['ssd_a_log'][d])
        bias = lw['ssd_dt_bias'][d]
        dt_c = f(jax.nn.softplus(dtc[:, :, d] + bias))
        dt_l = f(jax.nn.softplus(dtl[:, :, d] + bias))
        init = jnp.zeros((xc.shape[0], SSD_HEADS, SSD_HEAD_DIM, SSD_STATE), F32)
        bc_d, bl_d = f(bc), f(bl)
        prev_c, state_c, xd_c, acum_c = _ssd_chunk_states(f(xc), dt_c, a, bc_d, init)
        prev_l, _, xd_l, acum_l = _ssd_chunk_states(f(xl), dt_l, a, bl_d, state_c)
        skip = lw['ssd_d'][d][:, None]
        yl = f(_ssd_chunk_outputs(f(cl), bl_d, xd_l, acum_l, prev_l)) + skip * xl
        y_lat = yl if y_lat is None else y_lat + yl
        if ctx_out:
            yc = f(_ssd_chunk_outputs(f(cc), bc_d, xd_c, acum_c, prev_c)) + skip * xc
            y_ctx = yc if y_ctx is None else y_ctx + yc
    out_ctx = _gated_group_rmsnorm(y_ctx, zc, lw['ssd_norm_g']) if ctx_out else None
    return _gated_group_rmsnorm(y_lat, zl, lw['ssd_norm_g']), out_ctx


def kernel(x, c, ctx, c_ctx, w_mod, b_mod, norm1_g, norm2_g, w_in, b_in, sc_conv_w, cf_conv_w, cf_conv_b,
           cf_ln_g, cf_ln_b, sg_ln_g, sg_ln_b, sg_w, sg_b, ssd_conv_w, ssd_conv_b, ssd_dt_bias, ssd_a_log,
           ssd_d, ssd_norm_g, w_out, b_out, w_router, b_router, w_gate_up, b_gate_up, w_down, b_down,
           final_norm_g):
    cc = jnp.concatenate([c_ctx[None, :], c, jnp.zeros((MOD_ROWS - 1 - BATCH, D_MODEL), F32)], axis=0)
    mod_all = _modulation(cc, w_mod, b_mod).reshape(DEPTH, MOD_ROWS, 6, D_MODEL)

    x_all = jnp.concatenate([ctx.reshape(N_CTX, D_MODEL), x.reshape(N_LAT, D_MODEL)], axis=0)

    for i in range(DEPTH):
        last = i == DEPTH - 1
        lw = dict(sc_conv_w=sc_conv_w[i], cf_conv_w=cf_conv_w[i], cf_conv_b=cf_conv_b[i], cf_ln_g=cf_ln_g[i],
                  cf_ln_b=cf_ln_b[i], sg_ln_g=sg_ln_g[i], sg_ln_b=sg_ln_b[i], sg_w=sg_w[i], sg_b=sg_b[i],
                  ssd_conv_w=ssd_conv_w[i], ssd_conv_b=ssd_conv_b[i], ssd_dt_bias=ssd_dt_bias[i],
                  ssd_a_log=ssd_a_log[i], ssd_d=ssd_d[i], ssd_norm_g=ssd_norm_g[i])
        mod = mod_all[i]
        w_in_bf = w_in[i, :, :MAIN_COLS].astype(BF16)
        b_in_main = b_in[i, :MAIN_COLS].reshape(1, MAIN_COLS)
        wdt_bf = jnp.pad(w_in[i, :, MAIN_COLS:], ((0, 0), (0, LANE - DT_COLS))).astype(BF16)
        bdt = jnp.pad(b_in[i, MAIN_COLS:], (0, LANE - DT_COLS)).reshape(1, LANE)
        w_out_bf = w_out[i].astype(BF16)
        wr_bf = jnp.pad(w_router[i], ((0, 0), (0, LANE - N_EXPERTS))).astype(BF16)
        br = jnp.pad(b_router[i], (0, LANE - N_EXPERTS)).reshape(1, LANE)
        in_ctx_tiles = N_CTX // IN_TM
        proj = functools.partial(_in_projection, x_all, norm1_g[i], mod, w_in_bf, b_in_main, wdt_bf, bdt,
                                 ctx_tiles=in_ctx_tiles)

        if not last:
            p_all, dt_all = proj(row_tile0=0, n_row_tiles=(N_CTX + N_LAT) // IN_TM,
                                 col_tile0=0, n_col_tiles=MAIN_COLS // IN_TN)
            p_ctx = p_all[:N_CTX].reshape(BATCH, CTX_LEN, MAIN_COLS)
            p_lat = p_all[N_CTX:].reshape(BATCH, SEQ, MAIN_COLS)
            dt_ctx = dt_all[:N_CTX, :DT_COLS].reshape(BATCH, CTX_LEN, DT_COLS)
            dt_lat = dt_all[N_CTX:, :DT_COLS].reshape(BATCH, SEQ, DT_COLS)
            p_ctx_ssd = p_ctx[..., OFF_SSD:]
        else:
            p_lat, dt_lat = proj(row_tile0=in_ctx_tiles, n_row_tiles=N_LAT // IN_TM,
                                 col_tile0=0, n_col_tiles=MAIN_COLS // IN_TN)
            p_ctx_ssd, dt_ctx = proj(row_tile0=0, n_row_tiles=in_ctx_tiles,
                                     col_tile0=OFF_SSD // IN_TN, n_col_tiles=(MAIN_COLS - OFF_SSD) // IN_TN)
            p_lat = p_lat.reshape(BATCH, SEQ, MAIN_COLS)
            dt_lat = dt_lat[:, :DT_COLS].reshape(BATCH, SEQ, DT_COLS)
            p_ctx_ssd = p_ctx_ssd.reshape(BATCH, CTX_LEN, MAIN_COLS - OFF_SSD)
            dt_ctx = dt_ctx[:, :DT_COLS].reshape(BATCH, CTX_LEN, DT_COLS)

        ssd_lat, ssd_ctx = _ssd_mixer(p_ctx_ssd, dt_ctx, p_lat[..., OFF_SSD:], dt_lat, lw, not last)
        m_lat = jnp.concatenate(_local_mixers(p_lat, True, lw) + [ssd_lat], axis=-1).reshape(N_LAT, D_MODEL)
        out_ctx_tiles = N_CTX // OUT_TM
        if not last:
            m_ctx = jnp.concatenate(_local_mixers(p_ctx, False, lw) + [ssd_ctx], axis=-1).reshape(N_CTX, D_MODEL)
            m_all = jnp.concatenate([m_ctx, m_lat], axis=0)
            row_tile0 = 0
        else:
            m_all = m_lat
            row_tile0 = out_ctx_tiles
        x_mid, f_packed, logits = _out_projection(m_all, x_all, mod, norm2_g[i], w_out_bf,
                                                  b_out[i].reshape(1, D_MODEL), wr_bf, br,
                                                  row_tile0=row_tile0, ctx_tiles=out_ctx_tiles)
        yb, dest, gates = _moe_ffn(i, f_packed, logits[:, :N_EXPERTS], w_gate_up, b_gate_up, w_down, b_down)
        x_all = _combine(yb, dest, gates, x_mid, mod, final_norm_g if last else None,
                         row_tile0=row_tile0 * OUT_TM // CMB_TM, ctx_tiles=N_CTX // CMB_TM)

    return x_all.reshape(BATCH, SEQ, D_MODEL)
```

```python
import functools

import jax
import jax.numpy as jnp
from jax import lax
from jax.experimental import pallas as pl
from jax.experimental.pallas import tpu as pltpu

F32 = jnp.float32
BF16 = jnp.bfloat16

D_MODEL = 2048
BATCH = 4
SEQ = 2048
DEPTH = 2
GRID_W = 64
CTX_LEN = 256
EPS = 1e-6
GROUP_W = 512
SG_HEADS = 4
SG_CHUNK = 128
SG_HEAD_DIM = 128
SSD_HEAD_DIM = 64
SSD_HEADS = 8
SSD_GROUPS = 2
SSD_STATE = 128
SSD_CHUNK = 128
SSD_XBC = 1024
N_EXPERTS = 32
TOP_K = 4
D_FF = 2048
SWIGLU_LIMIT = 7.0
SWIGLU_ALPHA = 1.702
OFF_CF = 1536
OFF_SG = 2560
OFF_SSD = 3584
MAIN_COLS = 5120
DT_COLS = 2 * SSD_HEADS
LANE = 128
HALF_D = D_MODEL // 2

N_CTX = BATCH * CTX_LEN
N_LAT = BATCH * SEQ

VMEM_LIMIT = 56 * 1024 * 1024

MOD_ROWS = 8
MOD_TN = 1024


def _mod_kernel(c_ref, w_ref, b_ref, o_ref):
    c = c_ref[...]
    s = c * jax.nn.sigmoid(c)
    o_ref[...] = jnp.dot(s.astype(BF16), w_ref[...].astype(BF16), preferred_element_type=F32) + b_ref[...]


def _modulation(cc, w_mod, b_mod):
    n_out = 6 * D_MODEL
    return pl.pallas_call(
        _mod_kernel,
        grid=(DEPTH, n_out // MOD_TN),
        in_specs=[
            pl.BlockSpec((MOD_ROWS, D_MODEL), lambda l, n: (0, 0)),
            pl.BlockSpec((None, D_MODEL, MOD_TN), lambda l, n: (l, 0, n)),
            pl.BlockSpec((None, 1, MOD_TN), lambda l, n: (l, 0, n)),
        ],
        out_specs=pl.BlockSpec((None, MOD_ROWS, MOD_TN), lambda l, n: (l, 0, n)),
        out_shape=jax.ShapeDtypeStruct((DEPTH, MOD_ROWS, n_out), F32),
        compiler_params=pltpu.CompilerParams(
            dimension_semantics=("arbitrary", "arbitrary"), vmem_limit_bytes=VMEM_LIMIT),
        name="adaln_mod",
    )(cc, w_mod, b_mod.reshape(DEPTH, 1, n_out))


def _mod_row(tile, tile_rows, ctx_tiles):
    tiles_per_batch = SEQ // tile_rows
    return jnp.where(tile < ctx_tiles, 0, 1 + (tile - ctx_tiles) // tiles_per_batch)


IN_TM = 1024
IN_TN = 512
IN_PRO_ROWS = 256


def _inproj_kernel(x_ref, g_ref, mod_ref, w_ref, b_ref, wdt_ref, bdt_ref, o_ref, odt_ref, h_ref):
    @pl.when(pl.program_id(1) == 0)
    def _():
        g = g_ref[...]
        scale = 1.0 + mod_ref[1:2, :]
        shift = mod_ref[0:1, :]
        for r in range(IN_TM // IN_PRO_ROWS):
            rows = slice(r * IN_PRO_ROWS, (r + 1) * IN_PRO_ROWS)
            x = x_ref[rows, :]
            y = x * lax.rsqrt(jnp.mean(x * x, axis=-1, keepdims=True) + EPS)
            h_ref[rows, :] = ((y * g) * scale + shift).astype(BF16)
        odt_ref[...] = jnp.dot(h_ref[...], wdt_ref[...], preferred_element_type=F32) + bdt_ref[...]

    o_ref[...] = jnp.dot(h_ref[...], w_ref[...], preferred_element_type=F32) + b_ref[...]


def _in_projection(x_all, norm_g, mod, w_bf, b, wdt_bf, bdt, *, row_tile0, n_row_tiles, col_tile0, n_col_tiles,
                   ctx_tiles):
    rows = n_row_tiles * IN_TM
    return pl.pallas_call(
        _inproj_kernel,
        grid=(n_row_tiles, n_col_tiles),
        in_specs=[
            pl.BlockSpec((IN_TM, D_MODEL), lambda m, n: (m + row_tile0, 0)),
            pl.BlockSpec((1, D_MODEL), lambda m, n: (0, 0)),
            pl.BlockSpec((None, 6, D_MODEL), lambda m, n: (_mod_row(m + row_tile0, IN_TM, ctx_tiles), 0, 0)),
            pl.BlockSpec((D_MODEL, IN_TN), lambda m, n: (0, n + col_tile0)),
            pl.BlockSpec((1, IN_TN), lambda m, n: (0, n + col_tile0)),
            pl.BlockSpec((D_MODEL, LANE), lambda m, n: (0, 0)),
            pl.BlockSpec((1, LANE), lambda m, n: (0, 0)),
        ],
        out_specs=[
            pl.BlockSpec((IN_TM, IN_TN), lambda m, n: (m, n)),
            pl.BlockSpec((IN_TM, LANE), lambda m, n: (m, 0)),
        ],
        out_shape=[
            jax.ShapeDtypeStruct((rows, n_col_tiles * IN_TN), F32),
            jax.ShapeDtypeStruct((rows, LANE), F32),
        ],
        scratch_shapes=[pltpu.VMEM((IN_TM, D_MODEL), BF16)],
        compiler_params=pltpu.CompilerParams(
            dimension_semantics=("arbitrary", "arbitrary"), vmem_limit_bytes=VMEM_LIMIT),
        name="in_proj",
    )(x_all, norm_g.reshape(1, D_MODEL), mod, w_bf, b, wdt_bf, bdt)


OUT_TM = 256


def _outproj_kernel(m_ref, x_ref, mod_ref, g_ref, w_ref, b_ref, wr_ref, br_ref, xo_ref, f_ref, lg_ref):
    y = jnp.dot(m_ref[...].astype(BF16), w_ref[...], preferred_element_type=F32) + b_ref[...]
    xn = x_ref[...] + mod_ref[2:3, :] * y
    xo_ref[...] = xn
    r = lax.rsqrt(jnp.mean(xn * xn, axis=-1, keepdims=True) + EPS)
    f = ((xn * r) * g_ref[...]) * (1.0 + mod_ref[4:5, :]) + mod_ref[3:4, :]
    fb = f.astype(BF16)
    bits = lax.bitcast_convert_type(fb.astype(F32), jnp.uint32)
    f_ref[...] = (bits[:, HALF_D:] & jnp.uint32(0xFFFF0000)) | (bits[:, :HALF_D] >> 16)
    lg_ref[...] = jnp.dot(fb, wr_ref[...], preferred_element_type=F32) + br_ref[...]


def _out_projection(m, x_all, mod, norm_g, w_bf, b, wr_bf, br, *, row_tile0, ctx_tiles):
    rows = m.shape[0]
    n_tiles = rows // OUT_TM
    return pl.pallas_call(
        _outproj_kernel,
        grid=(n_tiles,),
        in_specs=[
            pl.BlockSpec((OUT_TM, D_MODEL), lambda t: (t, 0)),
            pl.BlockSpec((OUT_TM, D_MODEL), lambda t: (t + row_tile0, 0)),
            pl.BlockSpec((None, 6, D_MODEL), lambda t: (_mod_row(t + row_tile0, OUT_TM, ctx_tiles), 0, 0)),
            pl.BlockSpec((1, D_MODEL), lambda t: (0, 0)),
            pl.BlockSpec((D_MODEL, D_MODEL), lambda t: (0, 0)),
            pl.BlockSpec((1, D_MODEL), lambda t: (0, 0)),
            pl.BlockSpec((D_MODEL, LANE), lambda t: (0, 0)),
            pl.BlockSpec((1, LANE), lambda t: (0, 0)),
        ],
        out_specs=[
            pl.BlockSpec((OUT_TM, D_MODEL), lambda t: (t, 0)),
            pl.BlockSpec((OUT_TM, HALF_D), lambda t: (t, 0)),
            pl.BlockSpec((OUT_TM, LANE), lambda t: (t, 0)),
        ],
        out_shape=[
            jax.ShapeDtypeStruct((rows, D_MODEL), F32),
            jax.ShapeDtypeStruct((rows, HALF_D), jnp.uint32),
            jax.ShapeDtypeStruct((rows, LANE), F32),
        ],
        compiler_params=pltpu.CompilerParams(
            dimension_semantics=("arbitrary",), vmem_limit_bytes=VMEM_LIMIT),
        name="out_proj",
    )(m, x_all, mod, norm_g.reshape(1, D_MODEL), w_bf, b, wr_bf, br)


MOE_SUB = 256
MOE_CHUNK = 2048
MOE_NSUB = MOE_CHUNK // MOE_SUB
MOE_TF = 256
MOE_F_STEPS = D_FF // MOE_TF
MOE_N_STEPS = D_MODEL // MOE_TF
MOE_STEPS = MOE_F_STEPS + MOE_N_STEPS


def _moe_rows(n_tokens):
    slots = n_tokens * TOP_K
    padded = slots + N_EXPERTS * (MOE_SUB - 1)
    padded = -(-padded // MOE_SUB) * MOE_SUB
    return padded, padded


def _moe_chunks(n_tokens):
    padded, _ = _moe_rows(n_tokens)
    return padded // MOE_CHUNK + N_EXPERTS


def _moe_kernel(ce_ref, cs_ref, cn_ref, tok_ref, tok_next_ref, f_hbm, wg_ref, wu_ref, bg_ref, bu_ref, wd_ref,
                bd_ref, yb_hbm, xw, xs, hs, os_, sem_in, sem_out):
    c = pl.program_id(0)
    j = pl.program_id(1)
    nsub = cn_ref[c]
    start = pl.multiple_of(cs_ref[c], MOE_SUB)

    def gather_sub(i, toks):
        def issue(r, carry):
            row = i * MOE_SUB + r
            pltpu.make_async_copy(f_hbm.at[pl.ds(toks[0, row], 1)], xw.at[pl.ds(row, 1)], sem_in.at[i]).start()
            return carry

        lax.fori_loop(0, MOE_SUB, issue, 0, unroll=8)

    def gather_wait(i):
        _row_copy_wait(f_hbm, xw.at[pl.ds(i * MOE_SUB, MOE_SUB)], sem_in.at[i], MOE_SUB)

    def unpack(i):
        rows = slice(i * MOE_SUB, (i + 1) * MOE_SUB)
        w = xw[rows, :]
        xs[rows, :HALF_D] = lax.bitcast_convert_type(w << 16, F32).astype(BF16)
        xs[rows, HALF_D:] = lax.bitcast_convert_type(w & jnp.uint32(0xFFFF0000), F32).astype(BF16)

    def copy_out(slot, i, col):
        return pltpu.make_async_copy(
            os_.at[slot, pl.ds(i * MOE_SUB, MOE_SUB), :],
            yb_hbm.at[pl.ds(start + i * MOE_SUB, MOE_SUB), pl.ds(col, MOE_TF)],
            sem_out.at[slot])

    def for_valid_subs(fn):
        fn(0)
        for i in range(1, MOE_NSUB):
            @pl.when(i < nsub)
            def _(i=i):
                fn(i)

    @pl.when(nsub > 0)
    def _():
        @pl.when(jnp.logical_and(c == 0, j == 0))
        def _():
            for_valid_subs(lambda i: gather_sub(i, tok_ref))

        @pl.when(j == 0)
        def _():
            def land(i):
                gather_wait(i)
                unpack(i)

            for_valid_subs(land)

        nsub_next = cn_ref[jnp.minimum(c + 1, pl.num_programs(0) - 1)]
        for i in range(MOE_NSUB):
            @pl.when(jnp.logical_and(jnp.logical_and(j == i + 1, c + 1 < pl.num_programs(0)), i < nsub_next))
            def _(i=i):
                gather_sub(i, tok_next_ref)

        @pl.when(j < MOE_F_STEPS)
        def _():
            wg = wg_ref[...].astype(BF16)
            wu = wu_ref[...].astype(BF16)
            bg = bg_ref[...]
            bu = bu_ref[...]

            def gate_up(i):
                x = xs[i * MOE_SUB:(i + 1) * MOE_SUB, :]
                g = jnp.dot(x, wg, preferred_element_type=F32) + bg
                u = jnp.dot(x, wu, preferred_element_type=F32) + bu
                g = jnp.minimum(g, SWIGLU_LIMIT)
                u = jnp.clip(u, -SWIGLU_LIMIT, SWIGLU_LIMIT)
                h = (u + 1.0) * (g * jax.nn.sigmoid(SWIGLU_ALPHA * g))
                hs[j, i * MOE_SUB:(i + 1) * MOE_SUB, :] = h.astype(BF16)

            for_valid_subs(gate_up)

        @pl.when(j >= MOE_F_STEPS)
        def _():
            jn = j - MOE_F_STEPS
            slot = jn % 2
            col = pl.multiple_of(jn * MOE_TF, MOE_TF)
            wd = wd_ref[...].astype(BF16)
            bd = bd_ref[...]

            @pl.when(jn >= 2)
            def _():
                for_valid_subs(lambda i: copy_out(slot, i, col).wait())

            def down(i):
                h = jnp.concatenate(
                    [hs[f, i * MOE_SUB:(i + 1) * MOE_SUB, :] for f in range(MOE_F_STEPS)], axis=1)
                os_[slot, i * MOE_SUB:(i + 1) * MOE_SUB, :] = jnp.dot(h, wd, preferred_element_type=F32) + bd
                copy_out(slot, i, col).start()

            for_valid_subs(down)

            @pl.when(jn == MOE_N_STEPS - 1)
            def _():
                for_valid_subs(lambda i: copy_out(1 - slot, i, col).wait())
                for_valid_subs(lambda i: copy_out(slot, i, col).wait())


def _moe_experts(layer, chunk_e, chunk_start, chunk_nsub, chunk_tok, f_packed, rows, w_gate_up, b_gate_up,
                 w_down, b_down):
    n_chunks = chunk_e.shape[0]

    def gu_idx(half):
        def idx(c, j, ce, cs, cn):
            jj = jnp.where(cn[c] > 0, jnp.minimum(j, MOE_F_STEPS - 1), MOE_F_STEPS - 1)
            return (layer, ce[c], 0, half * MOE_F_STEPS + jj)
        return idx

    def d_idx(c, j, ce, cs, cn):
        jj = jnp.where(cn[c] > 0, jnp.maximum(j - MOE_F_STEPS, 0), MOE_N_STEPS - 1)
        return (layer, ce[c], 0, jj)

    grid_spec = pltpu.PrefetchScalarGridSpec(
        num_scalar_prefetch=3,
        grid=(n_chunks, MOE_STEPS),
        in_specs=[
            pl.BlockSpec((None, 1, MOE_CHUNK), lambda c, j, ce, cs, cn: (c, 0, 0), memory_space=pltpu.SMEM),
            pl.BlockSpec((None, 1, MOE_CHUNK), lambda c, j, ce, cs, cn: (jnp.minimum(c + 1, n_chunks - 1), 0, 0),
                         memory_space=pltpu.SMEM),
            pl.BlockSpec(memory_space=pl.ANY),
            pl.BlockSpec((None, None, D_MODEL, MOE_TF), gu_idx(0)),
            pl.BlockSpec((None, None, D_MODEL, MOE_TF), gu_idx(1)),
            pl.BlockSpec((None, None, 1, MOE_TF), gu_idx(0)),
            pl.BlockSpec((None, None, 1, MOE_TF), gu_idx(1)),
            pl.BlockSpec((None, None, D_FF, MOE_TF), d_idx),
            pl.BlockSpec((None, None, 1, MOE_TF), d_idx),
        ],
        out_specs=pl.BlockSpec(memory_space=pl.ANY),
        scratch_shapes=[
            pltpu.VMEM((MOE_CHUNK, HALF_D), jnp.uint32),
            pltpu.VMEM((MOE_CHUNK, D_MODEL), BF16),
            pltpu.VMEM((MOE_F_STEPS, MOE_CHUNK, MOE_TF), BF16),
            pltpu.VMEM((2, MOE_CHUNK, MOE_TF), F32),
            pltpu.SemaphoreType.DMA((MOE_NSUB,)),
            pltpu.SemaphoreType.DMA((2,)),
        ],
    )
    bgu = b_gate_up.reshape(DEPTH, N_EXPERTS, 1, 2 * D_FF)
    bd = b_down.reshape(DEPTH, N_EXPERTS, 1, D_MODEL)
    return pl.pallas_call(
        _moe_kernel,
        grid_spec=grid_spec,
        out_shape=jax.ShapeDtypeStruct((rows, D_MODEL), F32),
        compiler_params=pltpu.CompilerParams(
            dimension_semantics=("arbitrary", "arbitrary"), vmem_limit_bytes=VMEM_LIMIT),
        name="moe_experts",
    )(chunk_e, chunk_start, chunk_nsub, chunk_tok, chunk_tok, f_packed, w_gate_up, w_gate_up, bgu, bgu, w_down, bd)


def _row_copy_wait(src_hbm, dst, sem, rows):
    pltpu.make_async_copy(src_hbm.at[pl.ds(0, rows)], dst, sem).wait()


CMB_TM = 256


def _combine_kernel(idx_ref, yb_hbm, gates_ref, x_ref, mod_ref, *rest, final):
    if final:
        g_ref, o_ref, buf, sem = rest
    else:
        o_ref, buf, sem = rest

    for k in range(TOP_K):
        def issue(r, carry, k=k):
            pltpu.make_async_copy(yb_hbm.at[pl.ds(idx_ref[0, k * CMB_TM + r], 1)], buf.at[k, pl.ds(r, 1)],
                                  sem.at[k]).start()
            return carry

        lax.fori_loop(0, CMB_TM, issue, 0, unroll=8)

    gates = gates_ref[...]
    y = None
    for k in range(TOP_K):
        _row_copy_wait(yb_hbm, buf.at[k], sem.at[k], CMB_TM)
        t = gates[:, k:k + 1] * buf[k]
        y = t if y is None else y + t
    xn = x_ref[...] + mod_ref[5:6, :] * y
    if final:
        xn = xn * lax.rsqrt(jnp.mean(xn * xn, axis=-1, keepdims=True) + EPS) * g_ref[...]
    o_ref[...] = xn


def _combine(yb, dest, gates, x_mid, mod, final_g, *, row_tile0, ctx_tiles):
    n = x_mid.shape[0]
    n_tiles = n // CMB_TM
    final = final_g is not None
    idx = dest.reshape(n_tiles, CMB_TM, TOP_K).transpose(0, 2, 1).reshape(n_tiles, 1, TOP_K * CMB_TM)
    in_specs = [
        pl.BlockSpec((None, 1, TOP_K * CMB_TM), lambda t: (t, 0, 0), memory_space=pltpu.SMEM),
        pl.BlockSpec(memory_space=pl.ANY),
        pl.BlockSpec((CMB_TM, TOP_K), lambda t: (t, 0)),
        pl.BlockSpec((CMB_TM, D_MODEL), lambda t: (t, 0)),
        pl.BlockSpec((None, 6, D_MODEL), lambda t: (_mod_row(t + row_tile0, CMB_TM, ctx_tiles), 0, 0)),
    ]
    args = [idx, yb, gates, x_mid, mod]
    if final:
        in_specs.append(pl.BlockSpec((1, D_MODEL), lambda t: (0, 0)))
        args.append(final_g.reshape(1, D_MODEL))
    return pl.pallas_call(
        functools.partial(_combine_kernel, final=final),
        grid=(n_tiles,),
        in_specs=in_specs,
        out_specs=pl.BlockSpec((CMB_TM, D_MODEL), lambda t: (t, 0)),
        out_shape=jax.ShapeDtypeStruct((n, D_MODEL), F32),
        scratch_shapes=[pltpu.VMEM((TOP_K, CMB_TM, D_MODEL), F32), pltpu.SemaphoreType.DMA((TOP_K,))],
        compiler_params=pltpu.CompilerParams(dimension_semantics=("arbitrary",), vmem_limit_bytes=VMEM_LIMIT),
        name="moe_combine",
    )(*args)


def _moe_ffn(layer, f_packed, logits, w_gate_up, b_gate_up, w_down, b_down):
    n = f_packed.shape[0]
    slots = n * TOP_K
    padded_rows, buf_rows = _moe_rows(n)
    n_chunks = _moe_chunks(n)

    top_logit, top_e = lax.top_k(logits, TOP_K)
    gates = jax.nn.softmax(top_logit, axis=-1)
    flat_e = top_e.reshape(-1).astype(jnp.int32)
    order = jnp.argsort(flat_e)
    sorted_e = flat_e[order]
    counts = jnp.bincount(flat_e, length=N_EXPERTS).astype(jnp.int32)
    padded = (counts + MOE_SUB - 1) // MOE_SUB * MOE_SUB
    pad_end = jnp.cumsum(padded)
    pad_start = pad_end - padded
    start = jnp.cumsum(counts) - counts
    dest_sorted = pad_start[sorted_e] + jnp.arange(slots, dtype=jnp.int32) - start[sorted_e]
    buf_tok = jnp.zeros((buf_rows,), jnp.int32).at[dest_sorted].set((order // TOP_K).astype(jnp.int32))
    dest_of_slot = jnp.zeros((slots,), jnp.int32).at[order].set(dest_sorted)

    e_chunks = (padded + MOE_CHUNK - 1) // MOE_CHUNK
    chunk_end = jnp.cumsum(e_chunks)
    total_chunks = chunk_end[-1]
    cidx = jnp.arange(n_chunks, dtype=jnp.int32)
    ce = jnp.minimum(jnp.searchsorted(chunk_end, cidx, side='right'), N_EXPERTS - 1).astype(jnp.int32)
    local = cidx - (chunk_end[ce] - e_chunks[ce])
    valid = cidx < total_chunks
    c_start = jnp.where(valid, pad_start[ce] + local * MOE_CHUNK, 0).astype(jnp.int32)
    c_nsub = jnp.where(valid, jnp.minimum(MOE_CHUNK, padded[ce] - local * MOE_CHUNK) // MOE_SUB, 0).astype(jnp.int32)
    last_e = ce[jnp.maximum(total_chunks - 1, 0)]
    ce = jnp.where(valid, ce, last_e).astype(jnp.int32)

    row_of_chunk = c_start[:, None] + jnp.arange(MOE_CHUNK, dtype=jnp.int32)[None, :]
    chunk_tok = buf_tok[jnp.minimum(row_of_chunk, buf_rows - 1)].reshape(n_chunks, 1, MOE_CHUNK)
    yb = _moe_experts(layer, ce, c_start, c_nsub, chunk_tok, f_packed, padded_rows,
                      w_gate_up, b_gate_up, w_down, b_down)
    return yb, dest_of_slot.reshape(n, TOP_K), gates


def _layernorm(x, g, b):
    xc = x - jnp.mean(x, axis=-1, keepdims=True)
    y = xc * lax.rsqrt(jnp.mean(xc * xc, axis=-1, keepdims=True) + EPS)
    return y * g + b


def _dwconv(u, w, b, grid):
    bsz, length, ch = u.shape
    if grid:
        u = u.reshape(bsz * (length // GRID_W), GRID_W, ch)
    width = w.shape[0]
    y = lax.conv_general_dilated(u, w[:, None, :], (1,), [(width // 2, width // 2)],
                                 dimension_numbers=('NWC', 'WIO', 'NWC'), feature_group_count=ch)
    if b is not None:
        y = y + b
    return y.reshape(bsz, length, ch)


def _local_mixers(p, grid, lw):
    gate_b, gate_c, v = jnp.split(p[..., :OFF_CF], 3, axis=-1)
    m_sc = gate_b * _dwconv(gate_c * v, lw['sc_conv_w'], None, grid)
    a, g = jnp.split(p[..., OFF_CF:OFF_SG], 2, axis=-1)
    u = _dwconv(a * jax.nn.sigmoid(g), lw['cf_conv_w'], lw['cf_conv_b'], grid)
    m_cf = jax.nn.silu(_layernorm(u, lw['cf_ln_g'], lw['cf_ln_b']))
    bsz, length, _ = p.shape
    u2, v2 = jnp.split(jax.nn.gelu(p[..., OFF_SG:OFF_SSD], approximate=False), 2, axis=-1)
    v2 = _layernorm(v2, lw['sg_ln_g'], lw['sg_ln_b']).reshape(bsz, length // SG_CHUNK, SG_CHUNK, SG_HEADS, SG_HEAD_DIM)
    s = jnp.einsum('hts,bcshd->bcthd', lw['sg_w'], v2) + jnp.swapaxes(lw['sg_b'], 0, 1)[:, :, None]
    m_sg = u2 * s.reshape(bsz, length, GROUP_W)
    return [m_sc, m_cf, m_sg]


def _ssd_prepare(p, dt, conv_w, conv_b, grid):
    bsz, length, _ = p.shape
    z, xbc = p[..., :GROUP_W], p[..., GROUP_W:]
    xbc = jax.nn.silu(_dwconv(xbc, conv_w, conv_b, grid))
    xs, bm, cm = jnp.split(xbc, [GROUP_W, GROUP_W + SSD_GROUPS * SSD_STATE], axis=-1)
    xs = xs.reshape(bsz, length, SSD_HEADS, SSD_HEAD_DIM)
    rep = lambda t: jnp.repeat(t.reshape(bsz, length, SSD_GROUPS, SSD_STATE), SSD_HEADS // SSD_GROUPS, axis=2)
    return z, xs, rep(bm), rep(cm), dt.reshape(bsz, length, 2, SSD_HEADS)


def _ssd_chunk_states(x, dt, a, bm, init):
    bsz, length, h, p = x.shape
    nc = length // SSD_CHUNK
    xd = (x * dt[..., None]).reshape(bsz, nc, SSD_CHUNK, h, p)
    a_cum = jnp.cumsum((dt * a).reshape(bsz, nc, SSD_CHUNK, h), axis=2)
    bm = bm.reshape(bsz, nc, SSD_CHUNK, h, -1)
    decay_to_end = jnp.exp(a_cum[:, :, -1:] - a_cum)
    chunk_states = jnp.einsum('bclhn,bclhp->bchpn', bm * decay_to_end[..., None], xd)
    chunk_decay = jnp.exp(a_cum[:, :, -1])

    def step(s, inp):
        st, dec = inp
        return s * dec[:, :, None, None] + st, s

    final, prev = lax.scan(step, init, (jnp.moveaxis(chunk_states, 1, 0), jnp.moveaxis(chunk_decay, 1, 0)))
    return jnp.moveaxis(prev, 0, 1), final, xd, a_cum


def _ssd_chunk_outputs(cm, bm, xd, a_cum, prev):
    bsz, nc, l, h, p = xd.shape
    cm = cm.reshape(bsz, nc, l, h, -1)
    bm = bm.reshape(bsz, nc, l, h, -1)
    seg = a_cum[:, :, :, None, :] - a_cum[:, :, None, :, :]
    lower = jnp.tril(jnp.ones((l, l), bool))[None, None, :, :, None]
    decay = jnp.exp(jnp.where(lower, seg, -jnp.inf))
    scores = jnp.einsum('bclhn,bcshn->bclsh', cm, bm) * decay
    y = jnp.einsum('bclsh,bcshp->bclhp', scores, xd)
    y = y + jnp.einsum('bclhn,bchpn->bclhp', cm, prev) * jnp.exp(a_cum)[..., None]
    return y.reshape(bsz, nc * l, h, p)


def _gated_group_rmsnorm(y, z, g):
    bsz, length = y.shape[:2]
    v = (y.reshape(bsz, length, GROUP_W) * jax.nn.silu(z)).reshape(bsz, length, SSD_GROUPS, -1)
    v = v * lax.rsqrt(jnp.mean(v * v, axis=-1, keepdims=True) + EPS)
    return v.reshape(bsz, length, GROUP_W) * g


def _ssd_mixer(p_ctx, dt_ctx, p_lat, dt_lat, lw, ctx_out):
    conv_w, conv_b = lw['ssd_conv_w'], lw['ssd_conv_b']
    zc, xc, bc, cc, dtc = _ssd_prepare(p_ctx, dt_ctx, conv_w, conv_b, False)
    zl, xl, bl, cl, dtl = _ssd_prepare(p_lat, dt_lat, conv_w, conv_b, True)
    y_lat, y_ctx = None, None
    for d in range(2):
        f = (lambda t: jnp.flip(t, axis=1)) if d == 1 else (lambda t: t)
        a = -jnp.exp(lw['ssd_a_log'][d])
        bias = lw['ssd_dt_bias'][d]
        dt_c = f(jax.nn.softplus(dtc[:, :, d] + bias))
        dt_l = f(jax.nn.softplus(dtl[:, :, d] + bias))
        init = jnp.zeros((xc.shape[0], SSD_HEADS, SSD_HEAD_DIM, SSD_STATE), F32)
        bc_d, bl_d = f(bc), f(bl)
        prev_c, state_c, xd_c, acum_c = _ssd_chunk_states(f(xc), dt_c, a, bc_d, init)
        prev_l, _, xd_l, acum_l = _ssd_chunk_states(f(xl), dt_l, a, bl_d, state_c)
        skip = lw['ssd_d'][d][:, None]
        yl = f(_ssd_chunk_outputs(f(cl), bl_d, xd_l, acum_l, prev_l)) + skip * xl
        y_lat = yl if y_lat is None else y_lat + yl
        if ctx_out:
            yc = f(_ssd_chunk_outputs(f(cc), bc_d, xd_c, acum_c, prev_c)) + skip * xc
            y_ctx = yc if y_ctx is None else y_ctx + yc
    out_ctx = _gated_group_rmsnorm(y_ctx, zc, lw['ssd_norm_g']) if ctx_out else None
    return _gated_group_rmsnorm(y_lat, zl, lw['ssd_norm_g']), out_ctx


def kernel(x, c, ctx, c_ctx, w_mod, b_mod, norm1_g, norm2_g, w_in, b_in, sc_conv_w, cf_conv_w, cf_conv_b,
           cf_ln_g, cf_ln_b, sg_ln_g, sg_ln_b, sg_w, sg_b, ssd_conv_w, ssd_conv_b, ssd_dt_bias, ssd_a_log,
           ssd_d, ssd_norm_g, w_out, b_out, w_router, b_router, w_gate_up, b_gate_up, w_down, b_down,
           final_norm_g):
    cc = jnp.concatenate([c_ctx[None, :], c, jnp.zeros((MOD_ROWS - 1 - BATCH, D_MODEL), F32)], axis=0)
    mod_all = _modulation(cc, w_mod, b_mod).reshape(DEPTH, MOD_ROWS, 6, D_MODEL)

    x_all = jnp.concatenate([ctx.reshape(N_CTX, D_MODEL), x.reshape(N_LAT, D_MODEL)], axis=0)

    for i in range(DEPTH):
        last = i == DEPTH - 1
        lw = dict(sc_conv_w=sc_conv_w[i], cf_conv_w=cf_conv_w[i], cf_conv_b=cf_conv_b[i], cf_ln_g=cf_ln_g[i],
                  cf_ln_b=cf_ln_b[i], sg_ln_g=sg_ln_g[i], sg_ln_b=sg_ln_b[i], sg_w=sg_w[i], sg_b=sg_b[i],
                  ssd_conv_w=ssd_conv_w[i], ssd_conv_b=ssd_conv_b[i], ssd_dt_bias=ssd_dt_bias[i],
                  ssd_a_log=ssd_a_log[i], ssd_d=ssd_d[i], ssd_norm_g=ssd_norm_g[i])
        mod = mod_all[i]
        w_in_bf = w_in[i, :, :MAIN_COLS].astype(BF16)
        b_in_main = b_in[i, :MAIN_COLS].reshape(1, MAIN_COLS)
        wdt_bf = jnp.pad(w_in[i, :, MAIN_COLS:], ((0, 0), (0, LANE - DT_COLS))).astype(BF16)
        bdt = jnp.pad(b_in[i, MAIN_COLS:], (0, LANE - DT_COLS)).reshape(1, LANE)
        w_out_bf = w_out[i].astype(BF16)
        wr_bf = jnp.pad(w_router[i], ((0, 0), (0, LANE - N_EXPERTS))).astype(BF16)
        br = jnp.pad(b_router[i], (0, LANE - N_EXPERTS)).reshape(1, LANE)
        in_ctx_tiles = N_CTX // IN_TM
        proj = functools.partial(_in_projection, x_all, norm1_g[i], mod, w_in_bf, b_in_main, wdt_bf, bdt,
                                 ctx_tiles=in_ctx_tiles)

        if not last:
            p_all, dt_all = proj(row_tile0=0, n_row_tiles=(N_CTX + N_LAT) // IN_TM,
                                 col_tile0=0, n_col_tiles=MAIN_COLS // IN_TN)
            p_ctx = p_all[:N_CTX].reshape(BATCH, CTX_LEN, MAIN_COLS)
            p_lat = p_all[N_CTX:].reshape(BATCH, SEQ, MAIN_COLS)
            dt_ctx = dt_all[:N_CTX, :DT_COLS].reshape(BATCH, CTX_LEN, DT_COLS)
            dt_lat = dt_all[N_CTX:, :DT_COLS].reshape(BATCH, SEQ, DT_COLS)
            p_ctx_ssd = p_ctx[..., OFF_SSD:]
        else:
            p_lat, dt_lat = proj(row_tile0=in_ctx_tiles, n_row_tiles=N_LAT // IN_TM,
                                 col_tile0=0, n_col_tiles=MAIN_COLS // IN_TN)
            p_ctx_ssd, dt_ctx = proj(row_tile0=0, n_row_tiles=in_ctx_tiles,
                                     col_tile0=OFF_SSD // IN_TN, n_col_tiles=(MAIN_COLS - OFF_SSD) // IN_TN)
            p_lat = p_lat.reshape(BATCH, SEQ, MAIN_COLS)
            dt_lat = dt_lat[:, :DT_COLS].reshape(BATCH, SEQ, DT_COLS)
            p_ctx_ssd = p_ctx_ssd.reshape(BATCH, CTX_LEN, MAIN_COLS - OFF_SSD)
            dt_ctx = dt_ctx[:, :DT_COLS].reshape(BATCH, CTX_LEN, DT_COLS)

        ssd_lat, ssd_ctx = _ssd_mixer(p_ctx_ssd, dt_ctx, p_lat[..., OFF_SSD:], dt_lat, lw, not last)
        m_lat = jnp.concatenate(_local_mixers(p_lat, True, lw) + [ssd_lat], axis=-1).reshape(N_LAT, D_MODEL)
        out_ctx_tiles = N_CTX // OUT_TM
        if not last:
            m_ctx = jnp.concatenate(_local_mixers(p_ctx, False, lw) + [ssd_ctx], axis=-1).reshape(N_CTX, D_MODEL)
            m_all = jnp.concatenate([m_ctx, m_lat], axis=0)
            row_tile0 = 0
        else:
            m_all = m_lat
            row_tile0 = out_ctx_tiles
        x_mid, f_packed, logits = _out_projection(m_all, x_all, mod, norm2_g[i], w_out_bf,
                                                  b_out[i].reshape(1, D_MODEL), wr_bf, br,
                                                  row_tile0=row_tile0, ctx_tiles=out_ctx_tiles)
        yb, dest, gates = _moe_ffn(i, f_packed, logits[:, :N_EXPERTS], w_gate_up, b_gate_up, w_down, b_down)
        x_all = _combine(yb, dest, gates, x_mid, mod, final_norm_g if last else None,
                         row_tile0=row_tile0 * OUT_TM // CMB_TM, ctx_tiles=N_CTX // CMB_TM)

    return x_all.reshape(BATCH, SEQ, D_MODEL)
```

```python
import functools

import jax
import jax.numpy as jnp
from jax import lax
from jax.experimental import pallas as pl
from jax.experimental.pallas import tpu as pltpu

F32 = jnp.float32
BF16 = jnp.bfloat16

D_MODEL = 2048
BATCH = 4
SEQ = 2048
DEPTH = 2
GRID_W = 64
CTX_LEN = 256
EPS = 1e-6
GROUP_W = 512
SG_HEADS = 4
SG_CHUNK = 128
SG_HEAD_DIM = 128
SSD_HEAD_DIM = 64
SSD_HEADS = 8
SSD_GROUPS = 2
SSD_STATE = 128
SSD_CHUNK = 128
SSD_XBC = 1024
N_EXPERTS = 32
TOP_K = 4
D_FF = 2048
SWIGLU_LIMIT = 7.0
SWIGLU_ALPHA = 1.702
OFF_CF = 1536
OFF_SG = 2560
OFF_SSD = 3584
MAIN_COLS = 5120
DT_COLS = 2 * SSD_HEADS
LANE = 128
HALF_D = D_MODEL // 2

N_CTX = BATCH * CTX_LEN
N_LAT = BATCH * SEQ

VMEM_LIMIT = 56 * 1024 * 1024

MOD_ROWS = 8
MOD_TN = 1024


def _mod_kernel(c_ref, w_ref, b_ref, o_ref):
    c = c_ref[...]
    s = c * jax.nn.sigmoid(c)
    o_ref[...] = jnp.dot(s.astype(BF16), w_ref[...].astype(BF16), preferred_element_type=F32) + b_ref[...]


def _modulation(cc, w_mod, b_mod):
    n_out = 6 * D_MODEL
    return pl.pallas_call(
        _mod_kernel,
        grid=(DEPTH, n_out // MOD_TN),
        in_specs=[
            pl.BlockSpec((MOD_ROWS, D_MODEL), lambda l, n: (0, 0)),
            pl.BlockSpec((None, D_MODEL, MOD_TN), lambda l, n: (l, 0, n)),
            pl.BlockSpec((None, 1, MOD_TN), lambda l, n: (l, 0, n)),
        ],
        out_specs=pl.BlockSpec((None, MOD_ROWS, MOD_TN), lambda l, n: (l, 0, n)),
        out_shape=jax.ShapeDtypeStruct((DEPTH, MOD_ROWS, n_out), F32),
        compiler_params=pltpu.CompilerParams(
            dimension_semantics=("arbitrary", "arbitrary"), vmem_limit_bytes=VMEM_LIMIT),
        name="adaln_mod",
    )(cc, w_mod, b_mod.reshape(DEPTH, 1, n_out))


def _mod_row(tile, tile_rows, ctx_tiles):
    tiles_per_batch = SEQ // tile_rows
    return jnp.where(tile < ctx_tiles, 0, 1 + (tile - ctx_tiles) // tiles_per_batch)


IN_TM = 1024
IN_TN = 512
IN_PRO_ROWS = 256


def _inproj_kernel(x_ref, g_ref, mod_ref, w_ref, b_ref, wdt_ref, bdt_ref, o_ref, odt_ref, h_ref):
    @pl.when(pl.program_id(1) == 0)
    def _():
        g = g_ref[...]
        scale = 1.0 + mod_ref[1:2, :]
        shift = mod_ref[0:1, :]
        for r in range(IN_TM // IN_PRO_ROWS):
            rows = slice(r * IN_PRO_ROWS, (r + 1) * IN_PRO_ROWS)
            x = x_ref[rows, :]
            y = x * lax.rsqrt(jnp.mean(x * x, axis=-1, keepdims=True) + EPS)
            h_ref[rows, :] = ((y * g) * scale + shift).astype(BF16)
        odt_ref[...] = jnp.dot(h_ref[...], wdt_ref[...], preferred_element_type=F32) + bdt_ref[...]

    o_ref[...] = jnp.dot(h_ref[...], w_ref[...], preferred_element_type=F32) + b_ref[...]


def _in_projection(x_all, norm_g, mod, w_bf, b, wdt_bf, bdt, *, row_tile0, n_row_tiles, col_tile0, n_col_tiles,
                   ctx_tiles):
    rows = n_row_tiles * IN_TM
    return pl.pallas_call(
        _inproj_kernel,
        grid=(n_row_tiles, n_col_tiles),
        in_specs=[
            pl.BlockSpec((IN_TM, D_MODEL), lambda m, n: (m + row_tile0, 0)),
            pl.BlockSpec((1, D_MODEL), lambda m, n: (0, 0)),
            pl.BlockSpec((None, 6, D_MODEL), lambda m, n: (_mod_row(m + row_tile0, IN_TM, ctx_tiles), 0, 0)),
            pl.BlockSpec((D_MODEL, IN_TN), lambda m, n: (0, n + col_tile0)),
            pl.BlockSpec((1, IN_TN), lambda m, n: (0, n + col_tile0)),
            pl.BlockSpec((D_MODEL, LANE), lambda m, n: (0, 0)),
            pl.BlockSpec((1, LANE), lambda m, n: (0, 0)),
        ],
        out_specs=[
            pl.BlockSpec((IN_TM, IN_TN), lambda m, n: (m, n)),
            pl.BlockSpec((IN_TM, LANE), lambda m, n: (m, 0)),
        ],
        out_shape=[
            jax.ShapeDtypeStruct((rows, n_col_tiles * IN_TN), F32),
            jax.ShapeDtypeStruct((rows, LANE), F32),
        ],
        scratch_shapes=[pltpu.VMEM((IN_TM, D_MODEL), BF16)],
        compiler_params=pltpu.CompilerParams(
            dimension_semantics=("arbitrary", "arbitrary"), vmem_limit_bytes=VMEM_LIMIT),
        name="in_proj",
    )(x_all, norm_g.reshape(1, D_MODEL), mod, w_bf, b, wdt_bf, bdt)


OUT_TM = 256


def _outproj_kernel(m_ref, x_ref, mod_ref, g_ref, w_ref, b_ref, wr_ref, br_ref, xo_ref, f_ref, lg_ref):
    y = jnp.dot(m_ref[...].astype(BF16), w_ref[...], preferred_element_type=F32) + b_ref[...]
    xn = x_ref[...] + mod_ref[2:3, :] * y
    xo_ref[...] = xn
    r = lax.rsqrt(jnp.mean(xn * xn, axis=-1, keepdims=True) + EPS)
    f = ((xn * r) * g_ref[...]) * (1.0 + mod_ref[4:5, :]) + mod_ref[3:4, :]
    fb = f.astype(BF16)
    bits = lax.bitcast_convert_type(fb.astype(F32), jnp.uint32)
    f_ref[...] = (bits[:, HALF_D:] & jnp.uint32(0xFFFF0000)) | (bits[:, :HALF_D] >> 16)
    lg_ref[...] = jnp.dot(fb, wr_ref[...], preferred_element_type=F32) + br_ref[...]


def _out_projection(m, x_all, mod, norm_g, w_bf, b, wr_bf, br, *, row_tile0, ctx_tiles):
    rows = m.shape[0]
    n_tiles = rows // OUT_TM
    return pl.pallas_call(
        _outproj_kernel,
        grid=(n_tiles,),
        in_specs=[
            pl.BlockSpec((OUT_TM, D_MODEL), lambda t: (t, 0)),
            pl.BlockSpec((OUT_TM, D_MODEL), lambda t: (t + row_tile0, 0)),
            pl.BlockSpec((None, 6, D_MODEL), lambda t: (_mod_row(t + row_tile0, OUT_TM, ctx_tiles), 0, 0)),
            pl.BlockSpec((1, D_MODEL), lambda t: (0, 0)),
            pl.BlockSpec((D_MODEL, D_MODEL), lambda t: (0, 0)),
            pl.BlockSpec((1, D_MODEL), lambda t: (0, 0)),
            pl.BlockSpec((D_MODEL, LANE), lambda t: (0, 0)),
            pl.BlockSpec((1, LANE), lambda t: (0, 0)),
        ],
        out_specs=[
            pl.BlockSpec((OUT_TM, D_MODEL), lambda t: (t, 0)),
            pl.BlockSpec((OUT_TM, HALF_D), lambda t: (t, 0)),
            pl.BlockSpec((OUT_TM, LANE), lambda t: (t, 0)),
        ],
        out_shape=[
            jax.ShapeDtypeStruct((rows, D_MODEL), F32),
            jax.ShapeDtypeStruct((rows, HALF_D), jnp.uint32),
            jax.ShapeDtypeStruct((rows, LANE), F32),
        ],
        compiler_params=pltpu.CompilerParams(
            dimension_semantics=("arbitrary",), vmem_limit_bytes=VMEM_LIMIT),
        name="out_proj",
    )(m, x_all, mod, norm_g.reshape(1, D_MODEL), w_bf, b, wr_bf, br)


MOE_SUB = 256
MOE_CHUNK = 2048
MOE_NSUB = MOE_CHUNK // MOE_SUB
MOE_TF = 256
MOE_F_STEPS = D_FF // MOE_TF
MOE_N_STEPS = D_MODEL // MOE_TF
MOE_STEPS = MOE_F_STEPS + MOE_N_STEPS


def _moe_rows(n_tokens):
    slots = n_tokens * TOP_K
    padded = slots + N_EXPERTS * (MOE_SUB - 1)
    padded = -(-padded // MOE_SUB) * MOE_SUB
    return padded


def _moe_chunks(n_tokens):
    return _moe_rows(n_tokens) // MOE_CHUNK + N_EXPERTS


def _row_copy_wait(src_hbm, dst, sem, rows):
    pltpu.make_async_copy(src_hbm.at[pl.ds(0, rows)], dst, sem).wait()


def _moe_kernel(ce_ref, cs_ref, cn_ref, *refs):
    tok_refs = refs[:MOE_NSUB]
    (f_hbm, wg_ref, wu_ref, bg_ref, bu_ref, wd_ref, bd_ref, yb_hbm, xw, xs, hs, os_, sem_in, sem_out) = refs[MOE_NSUB:]
    c = pl.program_id(0)
    j = pl.program_id(1)
    n_chunks = pl.num_programs(0) - 1
    cur = jnp.maximum(c - 1, 0)
    nsub = jnp.where(c >= 1, cn_ref[cur], 0)
    start = pl.multiple_of(cs_ref[cur], MOE_SUB)
    nsub_next = jnp.where(c < n_chunks, cn_ref[jnp.minimum(c, n_chunks - 1)], 0)

    def gather_sub(i):
        def issue(r, carry):
            pltpu.make_async_copy(f_hbm.at[pl.ds(tok_refs[i][0, r], 1)], xw.at[pl.ds(i * MOE_SUB + r, 1)],
                                  sem_in.at[i]).start()
            return carry

        lax.fori_loop(0, MOE_SUB, issue, 0, unroll=8)

    def gather_wait(i):
        _row_copy_wait(f_hbm, xw.at[pl.ds(i * MOE_SUB, MOE_SUB)], sem_in.at[i], MOE_SUB)

    def unpack(i):
        rows = slice(i * MOE_SUB, (i + 1) * MOE_SUB)
        w = xw[rows, :]
        xs[rows, :HALF_D] = lax.bitcast_convert_type(w << 16, F32).astype(BF16)
        xs[rows, HALF_D:] = lax.bitcast_convert_type(w & jnp.uint32(0xFFFF0000), F32).astype(BF16)

    def copy_out(slot, i, col):
        return pltpu.make_async_copy(
            os_.at[slot, pl.ds(i * MOE_SUB, MOE_SUB), :],
            yb_hbm.at[pl.ds(start + i * MOE_SUB, MOE_SUB), pl.ds(col, MOE_TF)],
            sem_out.at[slot])

    def for_valid_subs(fn):
        fn(0)
        for i in range(1, MOE_NSUB):
            @pl.when(i < nsub)
            def _(i=i):
                fn(i)

    for i in range(MOE_NSUB):
        @pl.when(jnp.logical_and(j == i + 1, i < nsub_next))
        def _(i=i):
            gather_sub(i)

    @pl.when(nsub > 0)
    def _():
        @pl.when(j == 0)
        def _():
            def land(i):
                gather_wait(i)
                unpack(i)

            for_valid_subs(land)

        @pl.when(j < MOE_F_STEPS)
        def _():
            wg = wg_ref[...].astype(BF16)
            wu = wu_ref[...].astype(BF16)
            bg = bg_ref[...]
            bu = bu_ref[...]

            def gate_up(i):
                x = xs[i * MOE_SUB:(i + 1) * MOE_SUB, :]
                g = jnp.dot(x, wg, preferred_element_type=F32) + bg
                u = jnp.dot(x, wu, preferred_element_type=F32) + bu
                g = jnp.minimum(g, SWIGLU_LIMIT)
                u = jnp.clip(u, -SWIGLU_LIMIT, SWIGLU_LIMIT)
                h = (u + 1.0) * (g * jax.nn.sigmoid(SWIGLU_ALPHA * g))
                hs[j, i * MOE_SUB:(i + 1) * MOE_SUB, :] = h.astype(BF16)

            for_valid_subs(gate_up)

        @pl.when(j >= MOE_F_STEPS)
        def _():
            jn = j - MOE_F_STEPS
            slot = jn % 2
            col = pl.multiple_of(jn * MOE_TF, MOE_TF)
            wd = wd_ref[...].astype(BF16)
            bd = bd_ref[...]

            @pl.when(jn >= 2)
            def _():
                for_valid_subs(lambda i: copy_out(slot, i, col).wait())

            def down(i):
                h = jnp.concatenate(
                    [hs[f, i * MOE_SUB:(i + 1) * MOE_SUB, :] for f in range(MOE_F_STEPS)], axis=1)
                os_[slot, i * MOE_SUB:(i + 1) * MOE_SUB, :] = jnp.dot(h, wd, preferred_element_type=F32) + bd
                copy_out(slot, i, col).start()

            for_valid_subs(down)

            @pl.when(jn == MOE_N_STEPS - 1)
            def _():
                for_valid_subs(lambda i: copy_out(1 - slot, i, col).wait())
                for_valid_subs(lambda i: copy_out(slot, i, col).wait())


def _moe_experts(layer, chunk_e, chunk_start, chunk_nsub, buf_tok, f_packed, w_gate_up, b_gate_up, w_down, b_down):
    n_chunks = chunk_e.shape[0]
    rows = buf_tok.shape[0]
    n_sub_blocks = rows // MOE_SUB

    def gu_idx(half):
        def idx(c, j, ce, cs, cn):
            cur = jnp.maximum(c - 1, 0)
            jj = jnp.where(cn[cur] > 0, jnp.minimum(j, MOE_F_STEPS - 1), MOE_F_STEPS - 1)
            jj = jnp.where(c == 0, 0, jj)
            return (layer, ce[cur], 0, half * MOE_F_STEPS + jj)
        return idx

    def d_idx(c, j, ce, cs, cn):
        cur = jnp.maximum(c - 1, 0)
        jj = jnp.where(cn[cur] > 0, jnp.maximum(j - MOE_F_STEPS, 0), MOE_N_STEPS - 1)
        jj = jnp.where(c == 0, 0, jj)
        return (layer, ce[cur], 0, jj)

    def tok_idx(i):
        def idx(c, j, ce, cs, cn):
            nxt = jnp.minimum(c, n_chunks - 1)
            return (jnp.minimum(cs[nxt] // MOE_SUB + i, n_sub_blocks - 1), 0, 0)
        return idx

    grid_spec = pltpu.PrefetchScalarGridSpec(
        num_scalar_prefetch=3,
        grid=(n_chunks + 1, MOE_STEPS),
        in_specs=[pl.BlockSpec((None, 1, MOE_SUB), tok_idx(i), memory_space=pltpu.SMEM) for i in range(MOE_NSUB)] + [
            pl.BlockSpec(memory_space=pl.ANY),
            pl.BlockSpec((None, None, D_MODEL, MOE_TF), gu_idx(0)),
            pl.BlockSpec((None, None, D_MODEL, MOE_TF), gu_idx(1)),
            pl.BlockSpec((None, None, 1, MOE_TF), gu_idx(0)),
            pl.BlockSpec((None, None, 1, MOE_TF), gu_idx(1)),
            pl.BlockSpec((None, None, D_FF, MOE_TF), d_idx),
            pl.BlockSpec((None, None, 1, MOE_TF), d_idx),
        ],
        out_specs=pl.BlockSpec(memory_space=pl.ANY),
        scratch_shapes=[
            pltpu.VMEM((MOE_CHUNK, HALF_D), jnp.uint32),
            pltpu.VMEM((MOE_CHUNK, D_MODEL), BF16),
            pltpu.VMEM((MOE_F_STEPS, MOE_CHUNK, MOE_TF), BF16),
            pltpu.VMEM((2, MOE_CHUNK, MOE_TF), F32),
            pltpu.SemaphoreType.DMA((MOE_NSUB,)),
            pltpu.SemaphoreType.DMA((2,)),
        ],
    )
    bgu = b_gate_up.reshape(DEPTH, N_EXPERTS, 1, 2 * D_FF)
    bd = b_down.reshape(DEPTH, N_EXPERTS, 1, D_MODEL)
    return pl.pallas_call(
        _moe_kernel,
        grid_spec=grid_spec,
        out_shape=jax.ShapeDtypeStruct((rows, D_MODEL), F32),
        compiler_params=pltpu.CompilerParams(
            dimension_semantics=("arbitrary", "arbitrary"), vmem_limit_bytes=VMEM_LIMIT),
        name="moe_experts",
    )(chunk_e, chunk_start, chunk_nsub, *([buf_tok.reshape(n_sub_blocks, 1, MOE_SUB)] * MOE_NSUB), f_packed,
      w_gate_up, w_gate_up, bgu, bgu, w_down, bd)


CMB_TM = 256


def _combine_kernel(idx_ref, yb_hbm, gates_ref, x_ref, mod_ref, *rest, final):
    if final:
        g_ref, o_ref, buf, sem = rest
    else:
        o_ref, buf, sem = rest

    for k in range(TOP_K):
        def issue(r, carry, k=k):
            pltpu.make_async_copy(yb_hbm.at[pl.ds(idx_ref[0, k * CMB_TM + r], 1)], buf.at[k, pl.ds(r, 1)],
                                  sem.at[k]).start()
            return carry

        lax.fori_loop(0, CMB_TM, issue, 0, unroll=8)

    gates = gates_ref[...]
    y = None
    for k in range(TOP_K):
        _row_copy_wait(yb_hbm, buf.at[k], sem.at[k], CMB_TM)
        t = gates[:, k:k + 1] * buf[k]
        y = t if y is None else y + t
    xn = x_ref[...] + mod_ref[5:6, :] * y
    if final:
        xn = xn * lax.rsqrt(jnp.mean(xn * xn, axis=-1, keepdims=True) + EPS) * g_ref[...]
    o_ref[...] = xn


def _combine(yb, dest, gates, x_mid, mod, final_g, *, row_tile0, ctx_tiles):
    n = x_mid.shape[0]
    n_tiles = n // CMB_TM
    final = final_g is not None
    idx = dest.reshape(n_tiles, CMB_TM, TOP_K).transpose(0, 2, 1).reshape(n_tiles, 1, TOP_K * CMB_TM)
    in_specs = [
        pl.BlockSpec((None, 1, TOP_K * CMB_TM), lambda t: (t, 0, 0), memory_space=pltpu.SMEM),
        pl.BlockSpec(memory_space=pl.ANY),
        pl.BlockSpec((CMB_TM, TOP_K), lambda t: (t, 0)),
        pl.BlockSpec((CMB_TM, D_MODEL), lambda t: (t, 0)),
        pl.BlockSpec((None, 6, D_MODEL), lambda t: (_mod_row(t + row_tile0, CMB_TM, ctx_tiles), 0, 0)),
    ]
    args = [idx, yb, gates, x_mid, mod]
    if final:
        in_specs.append(pl.BlockSpec((1, D_MODEL), lambda t: (0, 0)))
        args.append(final_g.reshape(1, D_MODEL))
    return pl.pallas_call(
        functools.partial(_combine_kernel, final=final),
        grid=(n_tiles,),
        in_specs=in_specs,
        out_specs=pl.BlockSpec((CMB_TM, D_MODEL), lambda t: (t, 0)),
        out_shape=jax.ShapeDtypeStruct((n, D_MODEL), F32),
        scratch_shapes=[pltpu.VMEM((TOP_K, CMB_TM, D_MODEL), F32), pltpu.SemaphoreType.DMA((TOP_K,))],
        compiler_params=pltpu.CompilerParams(dimension_semantics=("arbitrary",), vmem_limit_bytes=VMEM_LIMIT),
        name="moe_combine",
    )(*args)


def _moe_ffn(layer, f_packed, logits, w_gate_up, b_gate_up, w_down, b_down):
    n = f_packed.shape[0]
    slots = n * TOP_K
    buf_rows = _moe_rows(n)
    n_chunks = _moe_chunks(n)

    top_logit, top_e = lax.top_k(logits, TOP_K)
    gates = jax.nn.softmax(top_logit, axis=-1)
    flat_e = top_e.reshape(-1).astype(jnp.int32)
    onehot = (flat_e[:, None] == jnp.arange(N_EXPERTS, dtype=jnp.int32)[None, :]).astype(jnp.int32)
    running = jnp.cumsum(onehot, axis=0)
    counts = running[-1]
    padded = (counts + MOE_SUB - 1) // MOE_SUB * MOE_SUB
    pad_end = jnp.cumsum(padded)
    pad_start = pad_end - padded
    dest_of_slot = jnp.sum(onehot * (running - 1 + pad_start[None, :]), axis=1)
    buf_tok = jnp.zeros((buf_rows,), jnp.int32).at[dest_of_slot].set(jnp.arange(slots, dtype=jnp.int32) // TOP_K)

    e_chunks = (padded + MOE_CHUNK - 1) // MOE_CHUNK
    chunk_end = jnp.cumsum(e_chunks)
    total_chunks = chunk_end[-1]
    cidx = jnp.arange(n_chunks, dtype=jnp.int32)
    ce = jnp.minimum(jnp.searchsorted(chunk_end, cidx, side='right'), N_EXPERTS - 1).astype(jnp.int32)
    local = cidx - (chunk_end[ce] - e_chunks[ce])
    valid = cidx < total_chunks
    c_start = jnp.where(valid, pad_start[ce] + local * MOE_CHUNK, 0).astype(jnp.int32)
    c_nsub = jnp.where(valid, jnp.minimum(MOE_CHUNK, padded[ce] - local * MOE_CHUNK) // MOE_SUB, 0).astype(jnp.int32)
    last_e = ce[jnp.maximum(total_chunks - 1, 0)]
    ce = jnp.where(valid, ce, last_e).astype(jnp.int32)

    yb = _moe_experts(layer, ce, c_start, c_nsub, buf_tok, f_packed, w_gate_up, b_gate_up, w_down, b_down)
    return yb, dest_of_slot.reshape(n, TOP_K), gates


MIX_TB = 256
CONV_HALO = 16
CONV_PIECE = 64
NEG_INF = float("-inf")


def _conv_pitch(rowlen):
    return rowlen + 2 * CONV_HALO


def _dwconv_block(u, pad_ref, w_ref, rowlen):
    taps = w_ref.shape[0]
    half = taps // 2
    pitch = _conv_pitch(rowlen)
    ch = u.shape[1]
    zeros = jnp.zeros((CONV_HALO, ch), F32)
    for r in range(MIX_TB // rowlen):
        base = r * pitch
        pad_ref[base:base + CONV_HALO, :] = zeros
        pad_ref[base + CONV_HALO:base + CONV_HALO + rowlen, :] = u[r * rowlen:(r + 1) * rowlen, :]
        pad_ref[base + CONV_HALO + rowlen:base + pitch, :] = zeros
    outs = []
    for r in range(MIX_TB // rowlen):
        for piece in range(rowlen // CONV_PIECE):
            acc = None
            for j in range(taps):
                off = r * pitch + CONV_HALO + piece * CONV_PIECE + j - half
                term = pad_ref[off:off + CONV_PIECE, :] * w_ref[j:j + 1, :]
                acc = term if acc is None else acc + term
            outs.append(acc)
    return jnp.concatenate(outs, axis=0)


def _ln_rows(x, g, b):
    xc = x - jnp.mean(x, axis=-1, keepdims=True)
    return xc * lax.rsqrt(jnp.mean(xc * xc, axis=-1, keepdims=True) + EPS) * g + b


def _silu(x):
    return x * jax.nn.sigmoid(x)


SSD_PAIRS = SSD_HEADS // 2
SSD_PAIR_W = 2 * SSD_HEAD_DIM


def _ssd_kernel(xbc_ref, dt_ref, init_ref, cw_ref, cb_ref, dtb_ref, a_ref, skip_ref, y_ref, fin_ref,
                pad_ref, st_ref, *, rowlen, rev, dcol):
    s = pl.program_id(1)

    @pl.when(s == 0)
    def _():
        st_ref[...] = init_ref[...]

    xa = _silu(_dwconv_block(xbc_ref[...], pad_ref, cw_ref, rowlen) + cb_ref[...])
    dt_all = dt_ref[...] + dtb_ref[...]
    dt_all = jnp.maximum(dt_all, 0.0) + jnp.log1p(jnp.exp(-jnp.abs(dt_all)))
    da_all = dt_all * a_ref[...]

    row_i = lax.broadcasted_iota(jnp.int32, (SSD_CHUNK, SSD_CHUNK), 0)
    col_i = lax.broadcasted_iota(jnp.int32, (SSD_CHUNK, SSD_CHUNK), 1)
    tri = (col_i >= row_i) if rev else (col_i <= row_i)
    tri_f = tri.astype(F32)
    first_half = lax.broadcasted_iota(jnp.int32, (SSD_CHUNK, SSD_PAIR_W), 1) < SSD_HEAD_DIM
    first_half_row = first_half[0:1, :]

    chunks = range(MIX_TB // SSD_CHUNK)
    for ci in (reversed(chunks) if rev else chunks):
        rows = slice(ci * SSD_CHUNK, (ci + 1) * SSD_CHUNK)
        acc = jnp.dot(tri_f, da_all[rows, :], preferred_element_type=F32, precision=lax.Precision.HIGHEST)
        acc_t = acc.T
        tot = acc[0:1, :] if rev else acc[SSD_CHUNK - 1:SSD_CHUNK, :]
        to_end = jnp.exp(tot - acc)
        from_start = jnp.exp(acc)
        chunk_decay = jnp.exp(tot)
        dt_c = dt_all[rows, :]
        for g in range(SSD_GROUPS):
            bg = xa[rows, GROUP_W + g * SSD_STATE:GROUP_W + (g + 1) * SSD_STATE]
            cg = xa[rows, GROUP_W + (SSD_GROUPS + g) * SSD_STATE:GROUP_W + (SSD_GROUPS + g + 1) * SSD_STATE]
            scores = lax.dot_general(cg.astype(BF16), bg.astype(BF16), (((1,), (1,)), ((), ())),
                                     preferred_element_type=F32)
            pairs_per_group = SSD_PAIRS // SSD_GROUPS
            for p in range(g * pairs_per_group, (g + 1) * pairs_per_group):
                c0 = dcol + 2 * p
                c1 = c0 + 1
                lhs, bw = [], []
                for col in (c0, c1):
                    seg = acc[:, col:col + 1] - acc_t[col:col + 1, :]
                    lhs.append(scores * jnp.exp(jnp.where(tri, seg, NEG_INF)))
                for col in (c0, c1):
                    lhs.append(cg * from_start[:, col:col + 1])
                    bw.append((bg * to_end[:, col:col + 1]).T)
                xp = xa[rows, p * SSD_PAIR_W:(p + 1) * SSD_PAIR_W]
                xd = xp * jnp.where(first_half, dt_c[:, c0:c0 + 1], dt_c[:, c1:c1 + 1])
                xd_top = jnp.where(first_half, xd, 0.0).astype(BF16)
                xd_bot = jnp.where(first_half, 0.0, xd).astype(BF16)
                st = st_ref[p]
                st_top = jnp.where(first_half, st, 0.0).astype(BF16)
                st_bot = jnp.where(first_half, 0.0, st).astype(BF16)
                y = jnp.dot(jnp.concatenate(lhs, axis=1).astype(BF16),
                            jnp.concatenate([xd_top, xd_bot, st_top, st_bot], axis=0),
                            preferred_element_type=F32)
                y_ref[rows, p * SSD_PAIR_W:(p + 1) * SSD_PAIR_W] = y + skip_ref[:, p * SSD_PAIR_W:(p + 1) * SSD_PAIR_W] * xp
                upd = jnp.dot(jnp.concatenate(bw, axis=1).astype(BF16), jnp.concatenate([xd_top, xd_bot], axis=0),
                              preferred_element_type=F32)
                decay_lane = jnp.where(first_half_row, chunk_decay[:, c0:c0 + 1], chunk_decay[:, c1:c1 + 1])
                st_ref[p] = st * decay_lane + upd

    @pl.when(s == pl.num_programs(1) - 1)
    def _():
        fin_ref[...] = st_ref[...]


def _ssd_sweep(xbc_src, xbc_col_block, dt_src, row_block0, n_blocks, init, lw_ssd, *, rowlen, rev, direction):
    cw, cb, dtb_row, a_row, skip_row = lw_ssd[direction]

    def blk(b, s):
        return row_block0 + b * n_blocks + ((n_blocks - 1 - s) if rev else s)

    def out_blk(b, s):
        return b * n_blocks + ((n_blocks - 1 - s) if rev else s)

    state_shape = (BATCH, SSD_PAIRS, SSD_STATE, SSD_PAIR_W)
    return pl.pallas_call(
        functools.partial(_ssd_kernel, rowlen=rowlen, rev=rev, dcol=direction * SSD_HEADS),
        grid=(BATCH, n_blocks),
        in_specs=[
            pl.BlockSpec((MIX_TB, SSD_XBC), lambda b, s: (blk(b, s), xbc_col_block)),
            pl.BlockSpec((MIX_TB, LANE), lambda b, s: (blk(b, s), 0)),
            pl.BlockSpec((None, SSD_PAIRS, SSD_STATE, SSD_PAIR_W), lambda b, s: (b, 0, 0, 0)),
            pl.BlockSpec((3, SSD_XBC), lambda b, s: (0, 0)),
            pl.BlockSpec((1, SSD_XBC), lambda b, s: (0, 0)),
            pl.BlockSpec((1, LANE), lambda b, s: (0, 0)),
            pl.BlockSpec((1, LANE), lambda b, s: (0, 0)),
            pl.BlockSpec((1, GROUP_W), lambda b, s: (0, 0)),
        ],
        out_specs=[
            pl.BlockSpec((MIX_TB, GROUP_W), lambda b, s: (out_blk(b, s), 0)),
            pl.BlockSpec((None, SSD_PAIRS, SSD_STATE, SSD_PAIR_W), lambda b, s: (b, 0, 0, 0)),
        ],
        out_shape=[
            jax.ShapeDtypeStruct((BATCH * n_blocks * MIX_TB, GROUP_W), F32),
            jax.ShapeDtypeStruct(state_shape, F32),
        ],
        scratch_shapes=[
            pltpu.VMEM(((MIX_TB // rowlen) * _conv_pitch(rowlen), SSD_XBC), F32),
            pltpu.VMEM((SSD_PAIRS, SSD_STATE, SSD_PAIR_W), F32),
        ],
        compiler_params=pltpu.CompilerParams(
            dimension_semantics=("arbitrary", "arbitrary"), vmem_limit_bytes=VMEM_LIMIT),
        name="ssd_sweep",
    )(xbc_src, dt_src, init, cw, cb, dtb_row, a_row, skip_row)


def _ssd_params(conv_w, conv_b, dt_bias, a_log, d_skip):
    out = []
    pad = LANE - 2 * SSD_HEADS
    dtb_row = jnp.pad(dt_bias.reshape(-1), (0, pad)).reshape(1, LANE)
    a_row = jnp.pad(-jnp.exp(a_log.reshape(-1)), (0, pad)).reshape(1, LANE)
    for d in range(2):
        skip_row = jnp.repeat(d_skip[d], SSD_HEAD_DIM).reshape(1, GROUP_W)
        out.append((conv_w, conv_b.reshape(1, SSD_XBC), dtb_row, a_row, skip_row))
    return out


def _local_kernel(p_ref, z_ref, y0_ref, y1_ref, scw_ref, cfw_ref, cfb_ref, cfg_ref, cfbeta_ref, sgg_ref,
                  sgbeta_ref, sgw_ref, sgb_ref, ng_ref, m_ref, pad_ref, *, rowlen):
    gate_b = p_ref[:, 0:GROUP_W]
    u = p_ref[:, GROUP_W:2 * GROUP_W] * p_ref[:, 2 * GROUP_W:3 * GROUP_W]
    m_ref[:, 0:GROUP_W] = gate_b * _dwconv_block(u, pad_ref, scw_ref, rowlen)
    u = p_ref[:, OFF_CF:OFF_CF + GROUP_W] * jax.nn.sigmoid(p_ref[:, OFF_CF + GROUP_W:OFF_SG])
    u = _dwconv_block(u, pad_ref, cfw_ref, rowlen) + cfb_ref[...]
    m_ref[:, GROUP_W:2 * GROUP_W] = _silu(_ln_rows(u, cfg_ref[...], cfbeta_ref[...]))
    q = p_ref[:, OFF_SG:OFF_SSD]
    q = 0.5 * q * (1.0 + lax.erf(q * (2.0 ** -0.5)))
    v = _ln_rows(q[:, GROUP_W:], sgg_ref[...], sgbeta_ref[...]).astype(BF16)
    n_chunks = MIX_TB // SG_CHUNK
    for h in range(SG_HEADS):
        cols = slice(h * SG_HEAD_DIM, (h + 1) * SG_HEAD_DIM)
        rhs = jnp.concatenate([v[c * SG_CHUNK:(c + 1) * SG_CHUNK, cols] for c in range(n_chunks)], axis=1)
        sres = jnp.dot(sgw_ref[h].astype(BF16), rhs, preferred_element_type=F32) + sgb_ref[:, h:h + 1]
        for c in range(n_chunks):
            rows = slice(c * SG_CHUNK, (c + 1) * SG_CHUNK)
            m_ref[rows, 2 * GROUP_W + h * SG_HEAD_DIM:2 * GROUP_W + (h + 1) * SG_HEAD_DIM] = (
                q[rows, cols] * sres[:, c * SG_HEAD_DIM:(c + 1) * SG_HEAD_DIM])
    yv = (y0_ref[...] + y1_ref[...]) * _silu(z_ref[...])
    gw = GROUP_W // SSD_GROUPS
    for g in range(SSD_GROUPS):
        vg = yv[:, g * gw:(g + 1) * gw]
        vg = vg * lax.rsqrt(jnp.mean(vg * vg, axis=-1, keepdims=True) + EPS)
        m_ref[:, 3 * GROUP_W + g * gw:3 * GROUP_W + (g + 1) * gw] = vg * ng_ref[:, g * gw:(g + 1) * gw]


def _local_mixers_call(p_src, z_col_block, row_block0, n_blocks, y0, y1, lw, *, rowlen):
    vec = lambda a: a.reshape(1, GROUP_W)
    args = [lw['sc_conv_w'], lw['cf_conv_w'], vec(lw['cf_conv_b']), vec(lw['cf_ln_g']), vec(lw['cf_ln_b']),
            vec(lw['sg_ln_g']), vec(lw['sg_ln_b']), lw['sg_w'], lw['sg_b'].T, vec(lw['ssd_norm_g'])]
    full = lambda a: pl.BlockSpec(a.shape, lambda t, nd=a.ndim: (0,) * nd)
    return pl.pallas_call(
        functools.partial(_local_kernel, rowlen=rowlen),
        grid=(n_blocks,),
        in_specs=[
            pl.BlockSpec((MIX_TB, OFF_SSD), lambda t: (t + row_block0, 0)),
            pl.BlockSpec((MIX_TB, GROUP_W), lambda t: (t + row_block0, z_col_block)),
            pl.BlockSpec((MIX_TB, GROUP_W), lambda t: (t, 0)),
            pl.BlockSpec((MIX_TB, GROUP_W), lambda t: (t, 0)),
        ] + [full(a) for a in args],
        out_specs=pl.BlockSpec((MIX_TB, D_MODEL), lambda t: (t, 0)),
        out_shape=jax.ShapeDtypeStruct((n_blocks * MIX_TB, D_MODEL), F32),
        scratch_shapes=[pltpu.VMEM(((MIX_TB // rowlen) * _conv_pitch(rowlen), GROUP_W), F32)],
        compiler_params=pltpu.CompilerParams(dimension_semantics=("arbitrary",), vmem_limit_bytes=VMEM_LIMIT),
        name="local_mixers",
    )(p_src, p_src, y0, y1, *args)


def _token_mixers(p_ctx_src, ctx_cols, p_lat_src, lat_cols, dt_ctx, dt_lat, lat_row_block0, lw, ctx_out):
    prm = _ssd_params(lw['ssd_conv_w'], lw['ssd_conv_b'], lw['ssd_dt_bias'], lw['ssd_a_log'], lw['ssd_d'])
    zero_state = jnp.zeros((BATCH, SSD_PAIRS, SSD_STATE, SSD_PAIR_W), F32)
    ctx_blocks = CTX_LEN // MIX_TB
    lat_blocks = SEQ // MIX_TB
    y_ctx, y_lat = [], []
    for d in range(2):
        rev = d == 1
        yc, state = _ssd_sweep(p_ctx_src, ctx_cols[1], dt_ctx, 0, ctx_blocks, zero_state, prm,
                               rowlen=CTX_LEN, rev=rev, direction=d)
        yl, _ = _ssd_sweep(p_lat_src, lat_cols[1], dt_lat, lat_row_block0, lat_blocks, state, prm,
                           rowlen=GRID_W, rev=rev, direction=d)
        y_ctx.append(yc)
        y_lat.append(yl)
    m_lat = _local_mixers_call(p_lat_src, lat_cols[0], lat_row_block0, BATCH * lat_blocks, y_lat[0], y_lat[1], lw,
                               rowlen=GRID_W)
    m_ctx = None
    if ctx_out:
        m_ctx = _local_mixers_call(p_ctx_src, ctx_cols[0], 0, BATCH * ctx_blocks, y_ctx[0], y_ctx[1], lw,
                                   rowlen=CTX_LEN)
    return m_lat, m_ctx


def kernel(x, c, ctx, c_ctx, w_mod, b_mod, norm1_g, norm2_g, w_in, b_in, sc_conv_w, cf_conv_w, cf_conv_b,
           cf_ln_g, cf_ln_b, sg_ln_g, sg_ln_b, sg_w, sg_b, ssd_conv_w, ssd_conv_b, ssd_dt_bias, ssd_a_log,
           ssd_d, ssd_norm_g, w_out, b_out, w_router, b_router, w_gate_up, b_gate_up, w_down, b_down,
           final_norm_g):
    cc = jnp.concatenate([c_ctx[None, :], c, jnp.zeros((MOD_ROWS - 1 - BATCH, D_MODEL), F32)], axis=0)
    mod_all = _modulation(cc, w_mod, b_mod).reshape(DEPTH, MOD_ROWS, 6, D_MODEL)

    x_all = jnp.concatenate([ctx.reshape(N_CTX, D_MODEL), x.reshape(N_LAT, D_MODEL)], axis=0)

    for i in range(DEPTH):
        last = i == DEPTH - 1
        lw = dict(sc_conv_w=sc_conv_w[i], cf_conv_w=cf_conv_w[i], cf_conv_b=cf_conv_b[i], cf_ln_g=cf_ln_g[i],
                  cf_ln_b=cf_ln_b[i], sg_ln_g=sg_ln_g[i], sg_ln_b=sg_ln_b[i], sg_w=sg_w[i], sg_b=sg_b[i],
                  ssd_conv_w=ssd_conv_w[i], ssd_conv_b=ssd_conv_b[i], ssd_dt_bias=ssd_dt_bias[i],
                  ssd_a_log=ssd_a_log[i], ssd_d=ssd_d[i], ssd_norm_g=ssd_norm_g[i])
        mod = mod_all[i]
        w_in_bf = w_in[i, :, :MAIN_COLS].astype(BF16)
        b_in_main = b_in[i, :MAIN_COLS].reshape(1, MAIN_COLS)
        wdt_bf = jnp.pad(w_in[i, :, MAIN_COLS:], ((0, 0), (0, LANE - DT_COLS))).astype(BF16)
        bdt = jnp.pad(b_in[i, MAIN_COLS:], (0, LANE - DT_COLS)).reshape(1, LANE)
        w_out_bf = w_out[i].astype(BF16)
        wr_bf = jnp.pad(w_router[i], ((0, 0), (0, LANE - N_EXPERTS))).astype(BF16)
        br = jnp.pad(b_router[i], (0, LANE - N_EXPERTS)).reshape(1, LANE)
        in_ctx_tiles = N_CTX // IN_TM
        proj = functools.partial(_in_projection, x_all, norm1_g[i], mod, w_in_bf, b_in_main, wdt_bf, bdt,
                                 ctx_tiles=in_ctx_tiles)

        z_xbc_cols = (OFF_SSD // GROUP_W, (OFF_SSD + GROUP_W) // SSD_XBC)
        if not last:
            p_all, dt_all = proj(row_tile0=0, n_row_tiles=(N_CTX + N_LAT) // IN_TM,
                                 col_tile0=0, n_col_tiles=MAIN_COLS // IN_TN)
            m_lat, m_ctx = _token_mixers(p_all, z_xbc_cols, p_all, z_xbc_cols, dt_all, dt_all, N_CTX // MIX_TB, lw, True)
        else:
            p_lat, dt_lat = proj(row_tile0=in_ctx_tiles, n_row_tiles=N_LAT // IN_TM,
                                 col_tile0=0, n_col_tiles=MAIN_COLS // IN_TN)
            ctx_col0 = MAIN_COLS - 2 * SSD_XBC
            p_ctx, dt_ctx = proj(row_tile0=0, n_row_tiles=in_ctx_tiles,
                                 col_tile0=ctx_col0 // IN_TN, n_col_tiles=(MAIN_COLS - ctx_col0) // IN_TN)
            ctx_cols = ((OFF_SSD - ctx_col0) // GROUP_W, (OFF_SSD + GROUP_W - ctx_col0) // SSD_XBC)
            m_lat, m_ctx = _token_mixers(p_ctx, ctx_cols, p_lat, z_xbc_cols, dt_ctx, dt_lat, 0, lw, False)
        out_ctx_tiles = N_CTX // OUT_TM
        if not last:
            m_all = jnp.concatenate([m_ctx, m_lat], axis=0)
            row_tile0 = 0
        else:
            m_all = m_lat
            row_tile0 = out_ctx_tiles
        x_mid, f_packed, logits = _out_projection(m_all, x_all, mod, norm2_g[i], w_out_bf,
                                                  b_out[i].reshape(1, D_MODEL), wr_bf, br,
                                                  row_tile0=row_tile0, ctx_tiles=out_ctx_tiles)
        yb, dest, gates = _moe_ffn(i, f_packed, logits[:, :N_EXPERTS], w_gate_up, b_gate_up, w_down, b_down)
        x_all = _combine(yb, dest, gates, x_mid, mod, final_norm_g if last else None,
                         row_tile0=row_tile0 * OUT_TM // CMB_TM, ctx_tiles=N_CTX // CMB_TM)

    return x_all.reshape(BATCH, SEQ, D_MODEL)
```

```python
import functools

import jax
import jax.numpy as jnp
from jax import lax
from jax.experimental import pallas as pl
from jax.experimental.pallas import tpu as pltpu

F32 = jnp.float32
BF16 = jnp.bfloat16

D_MODEL = 2048
BATCH = 4
SEQ = 2048
DEPTH = 2
GRID_W = 64
CTX_LEN = 256
EPS = 1e-6
GROUP_W = 512
SG_HEADS = 4
SG_CHUNK = 128
SG_HEAD_DIM = 128
SSD_HEAD_DIM = 64
SSD_HEADS = 8
SSD_GROUPS = 2
SSD_STATE = 128
SSD_CHUNK = 128
SSD_XBC = 1024
N_EXPERTS = 32
TOP_K = 4
D_FF = 2048
SWIGLU_LIMIT = 7.0
SWIGLU_ALPHA = 1.702
OFF_CF = 1536
OFF_SG = 2560
OFF_SSD = 3584
MAIN_COLS = 5120
DT_COLS = 2 * SSD_HEADS
LANE = 128
HALF_D = D_MODEL // 2

N_CTX = BATCH * CTX_LEN
N_LAT = BATCH * SEQ

VMEM_LIMIT = 56 * 1024 * 1024

MOD_ROWS = 8
MOD_TN = 1024


def _mod_kernel(c_ref, w_ref, b_ref, o_ref):
    c = c_ref[...]
    s = c * jax.nn.sigmoid(c)
    o_ref[...] = jnp.dot(s.astype(BF16), w_ref[...].astype(BF16), preferred_element_type=F32) + b_ref[...]


def _modulation(cc, w_mod, b_mod):
    n_out = 6 * D_MODEL
    return pl.pallas_call(
        _mod_kernel,
        grid=(DEPTH, n_out // MOD_TN),
        in_specs=[
            pl.BlockSpec((MOD_ROWS, D_MODEL), lambda l, n: (0, 0)),
            pl.BlockSpec((None, D_MODEL, MOD_TN), lambda l, n: (l, 0, n)),
            pl.BlockSpec((None, 1, MOD_TN), lambda l, n: (l, 0, n)),
        ],
        out_specs=pl.BlockSpec((None, MOD_ROWS, MOD_TN), lambda l, n: (l, 0, n)),
        out_shape=jax.ShapeDtypeStruct((DEPTH, MOD_ROWS, n_out), F32),
        compiler_params=pltpu.CompilerParams(
            dimension_semantics=("arbitrary", "arbitrary"), vmem_limit_bytes=VMEM_LIMIT),
        name="adaln_mod",
    )(cc, w_mod, b_mod.reshape(DEPTH, 1, n_out))


def _mod_row(tile, tile_rows, ctx_tiles):
    tiles_per_batch = SEQ // tile_rows
    return jnp.where(tile < ctx_tiles, 0, 1 + (tile - ctx_tiles) // tiles_per_batch)


IN_TM = 1024
IN_TN = 1024
IN_PRO_ROWS = 256


def _inproj_kernel(x_ref, g_ref, mod_ref, w_ref, b_ref, wdt_ref, bdt_ref, o_ref, odt_ref, h_ref):
    @pl.when(pl.program_id(1) == 0)
    def _():
        g = g_ref[...]
        scale = 1.0 + mod_ref[1:2, :]
        shift = mod_ref[0:1, :]
        for r in range(IN_TM // IN_PRO_ROWS):
            rows = slice(r * IN_PRO_ROWS, (r + 1) * IN_PRO_ROWS)
            x = x_ref[rows, :]
            y = x * lax.rsqrt(jnp.mean(x * x, axis=-1, keepdims=True) + EPS)
            h_ref[rows, :] = ((y * g) * scale + shift).astype(BF16)
        odt_ref[...] = jnp.dot(h_ref[...], wdt_ref[...], preferred_element_type=F32) + bdt_ref[...]

    o_ref[...] = jnp.dot(h_ref[...], w_ref[...], preferred_element_type=F32) + b_ref[...]


def _in_projection(x_all, norm_g, mod, w_bf, b, wdt_bf, bdt, *, row_tile0, n_row_tiles, col_tile0, n_col_tiles,
                   ctx_tiles):
    rows = n_row_tiles * IN_TM
    return pl.pallas_call(
        _inproj_kernel,
        grid=(n_row_tiles, n_col_tiles),
        in_specs=[
            pl.BlockSpec((IN_TM, D_MODEL), lambda m, n: (m + row_tile0, 0)),
            pl.BlockSpec((1, D_MODEL), lambda m, n: (0, 0)),
            pl.BlockSpec((None, 6, D_MODEL), lambda m, n: (_mod_row(m + row_tile0, IN_TM, ctx_tiles), 0, 0)),
            pl.BlockSpec((D_MODEL, IN_TN), lambda m, n: (0, n + col_tile0)),
            pl.BlockSpec((1, IN_TN), lambda m, n: (0, n + col_tile0)),
            pl.BlockSpec((D_MODEL, LANE), lambda m, n: (0, 0)),
            pl.BlockSpec((1, LANE), lambda m, n: (0, 0)),
        ],
        out_specs=[
            pl.BlockSpec((IN_TM, IN_TN), lambda m, n: (m, n)),
            pl.BlockSpec((IN_TM, LANE), lambda m, n: (m, 0)),
        ],
        out_shape=[
            jax.ShapeDtypeStruct((rows, n_col_tiles * IN_TN), F32),
            jax.ShapeDtypeStruct((rows, LANE), F32),
        ],
        scratch_shapes=[pltpu.VMEM((IN_TM, D_MODEL), BF16)],
        compiler_params=pltpu.CompilerParams(
            dimension_semantics=("arbitrary", "arbitrary"), vmem_limit_bytes=VMEM_LIMIT),
        name="in_proj",
    )(x_all, norm_g.reshape(1, D_MODEL), mod, w_bf, b, wdt_bf, bdt)


OUT_TM = 256


def _outproj_kernel(m_ref, x_ref, mod_ref, g_ref, w_ref, b_ref, wr_ref, br_ref, xo_ref, f_ref, lg_ref):
    y = jnp.dot(m_ref[...].astype(BF16), w_ref[...], preferred_element_type=F32) + b_ref[...]
    xn = x_ref[...] + mod_ref[2:3, :] * y
    xo_ref[...] = xn
    r = lax.rsqrt(jnp.mean(xn * xn, axis=-1, keepdims=True) + EPS)
    f = ((xn * r) * g_ref[...]) * (1.0 + mod_ref[4:5, :]) + mod_ref[3:4, :]
    fb = f.astype(BF16)
    bits = lax.bitcast_convert_type(fb.astype(F32), jnp.uint32)
    f_ref[...] = (bits[:, HALF_D:] & jnp.uint32(0xFFFF0000)) | (bits[:, :HALF_D] >> 16)
    lg_ref[...] = jnp.dot(fb, wr_ref[...], preferred_element_type=F32) + br_ref[...]


def _out_projection(m, x_all, mod, norm_g, w_bf, b, wr_bf, br, *, row_tile0, ctx_tiles):
    rows = m.shape[0]
    n_tiles = rows // OUT_TM
    return pl.pallas_call(
        _outproj_kernel,
        grid=(n_tiles,),
        in_specs=[
            pl.BlockSpec((OUT_TM, D_MODEL), lambda t: (t, 0)),
            pl.BlockSpec((OUT_TM, D_MODEL), lambda t: (t + row_tile0, 0)),
            pl.BlockSpec((None, 6, D_MODEL), lambda t: (_mod_row(t + row_tile0, OUT_TM, ctx_tiles), 0, 0)),
            pl.BlockSpec((1, D_MODEL), lambda t: (0, 0)),
            pl.BlockSpec((D_MODEL, D_MODEL), lambda t: (0, 0)),
            pl.BlockSpec((1, D_MODEL), lambda t: (0, 0)),
            pl.BlockSpec((D_MODEL, LANE), lambda t: (0, 0)),
            pl.BlockSpec((1, LANE), lambda t: (0, 0)),
        ],
        out_specs=[
            pl.BlockSpec((OUT_TM, D_MODEL), lambda t: (t, 0)),
            pl.BlockSpec((OUT_TM, HALF_D), lambda t: (t, 0)),
            pl.BlockSpec((OUT_TM, LANE), lambda t: (t, 0)),
        ],
        out_shape=[
            jax.ShapeDtypeStruct((rows, D_MODEL), F32),
            jax.ShapeDtypeStruct((rows, HALF_D), jnp.uint32),
            jax.ShapeDtypeStruct((rows, LANE), F32),
        ],
        compiler_params=pltpu.CompilerParams(
            dimension_semantics=("arbitrary",), vmem_limit_bytes=VMEM_LIMIT),
        name="out_proj",
    )(m, x_all, mod, norm_g.reshape(1, D_MODEL), w_bf, b, wr_bf, br)


MOE_SUB = 256
MOE_CHUNK = 2048
MOE_NSUB = MOE_CHUNK // MOE_SUB
MOE_TF = 256
MOE_F_STEPS = D_FF // MOE_TF
MOE_N_STEPS = D_MODEL // MOE_TF
MOE_STEPS = MOE_F_STEPS + MOE_N_STEPS


def _moe_rows(n_tokens):
    slots = n_tokens * TOP_K
    padded = slots + N_EXPERTS * (MOE_SUB - 1)
    padded = -(-padded // MOE_SUB) * MOE_SUB
    return padded


def _moe_chunks(n_tokens):
    return _moe_rows(n_tokens) // MOE_CHUNK + N_EXPERTS


def _row_copy_wait(src_hbm, dst, sem, rows):
    pltpu.make_async_copy(src_hbm.at[pl.ds(0, rows)], dst, sem).wait()


def _moe_kernel(ce_ref, cs_ref, cn_ref, *refs):
    tok_refs = refs[:MOE_NSUB]
    (f_hbm, wg_ref, wu_ref, bg_ref, bu_ref, wd_ref, bd_ref, yb_hbm, xw, xs, hs, os_, sem_in, sem_out) = refs[MOE_NSUB:]
    c = pl.program_id(0)
    j = pl.program_id(1)
    n_chunks = pl.num_programs(0) - 1
    cur = jnp.maximum(c - 1, 0)
    nsub = jnp.where(c >= 1, cn_ref[cur], 0)
    start = pl.multiple_of(cs_ref[cur], MOE_SUB)
    nsub_next = jnp.where(c < n_chunks, cn_ref[jnp.minimum(c, n_chunks - 1)], 0)

    def gather_sub(i):
        def issue(r, carry):
            pltpu.make_async_copy(f_hbm.at[pl.ds(tok_refs[i][0, r], 1)], xw.at[pl.ds(i * MOE_SUB + r, 1)],
                                  sem_in.at[i]).start()
            return carry

        lax.fori_loop(0, MOE_SUB, issue, 0, unroll=8)

    def gather_wait(i):
        _row_copy_wait(f_hbm, xw.at[pl.ds(i * MOE_SUB, MOE_SUB)], sem_in.at[i], MOE_SUB)

    def unpack(i):
        rows = slice(i * MOE_SUB, (i + 1) * MOE_SUB)
        w = xw[rows, :]
        xs[rows, :HALF_D] = lax.bitcast_convert_type(w << 16, F32).astype(BF16)
        xs[rows, HALF_D:] = lax.bitcast_convert_type(w & jnp.uint32(0xFFFF0000), F32).astype(BF16)

    def copy_out(slot, off, n, col):
        return pltpu.make_async_copy(
            os_.at[slot, pl.ds(off, n * MOE_SUB), :],
            yb_hbm.at[pl.ds(start + off, n * MOE_SUB), pl.ds(col, MOE_TF)],
            sem_out.at[slot])

    def for_valid_subs(fn):
        fn(0)
        for i in range(1, MOE_NSUB):
            @pl.when(i < nsub)
            def _(i=i):
                fn(i)

    def for_pieces(fn):
        quads = nsub // 4
        rem = nsub - 4 * quads
        for q in range(MOE_NSUB // 4):
            @pl.when(q < quads)
            def _(q=q):
                fn(q * 4 * MOE_SUB, 4)
        base = pl.multiple_of(quads * (4 * MOE_SUB), MOE_SUB)

        @pl.when(rem >= 2)
        def _():
            fn(base, 2)

        @pl.when(rem % 2 == 1)
        def _():
            fn(pl.multiple_of(base + (rem // 2) * (2 * MOE_SUB), MOE_SUB), 1)

    for i in range(MOE_NSUB):
        @pl.when(jnp.logical_and(j == i + 1, i < nsub_next))
        def _(i=i):
            gather_sub(i)

    @pl.when(nsub > 0)
    def _():
        @pl.when(j == 0)
        def _():
            def land(i):
                gather_wait(i)
                unpack(i)

            for_valid_subs(land)

        @pl.when(j < MOE_F_STEPS)
        def _():
            wg = wg_ref[...].astype(BF16)
            wu = wu_ref[...].astype(BF16)
            bg = bg_ref[...]
            bu = bu_ref[...]

            def gate_up(off, n):
                rows = pl.ds(off, n * MOE_SUB)
                x = xs[rows, :]
                g = jnp.dot(x, wg, preferred_element_type=F32) + bg
                u = jnp.dot(x, wu, preferred_element_type=F32) + bu
                g = jnp.minimum(g, SWIGLU_LIMIT)
                u = jnp.clip(u, -SWIGLU_LIMIT, SWIGLU_LIMIT)
                h = (u + 1.0) * (g * jax.nn.sigmoid(SWIGLU_ALPHA * g))
                hs[j, rows, :] = h.astype(BF16)

            for_pieces(gate_up)

        @pl.when(j >= MOE_F_STEPS)
        def _():
            jn = j - MOE_F_STEPS
            slot = jn % 2
            col = pl.multiple_of(jn * MOE_TF, MOE_TF)
            wd = wd_ref[...].astype(BF16)
            bd = bd_ref[...]

            @pl.when(jn >= 2)
            def _():
                for_pieces(lambda off, n: copy_out(slot, off, n, col).wait())

            def down(off, n):
                rows = pl.ds(off, n * MOE_SUB)
                h = jnp.concatenate([hs[f, rows, :] for f in range(MOE_F_STEPS)], axis=1)
                os_[slot, rows, :] = jnp.dot(h, wd, preferred_element_type=F32) + bd
                copy_out(slot, off, n, col).start()

            for_pieces(down)

            @pl.when(jn == MOE_N_STEPS - 1)
            def _():
                for_pieces(lambda off, n: copy_out(1 - slot, off, n, col).wait())
                for_pieces(lambda off, n: copy_out(slot, off, n, col).wait())


def _moe_experts(layer, chunk_e, chunk_start, chunk_nsub, used_chunks, buf_tok, f_packed, w_gate_up, b_gate_up,
                 w_down, b_down):
    n_chunks = chunk_e.shape[0]
    rows = buf_tok.shape[0]
    n_sub_blocks = rows // MOE_SUB

    def gu_idx(half):
        def idx(c, j, ce, cs, cn):
            cur = jnp.maximum(c - 1, 0)
            jj = jnp.where(cn[cur] > 0, jnp.minimum(j, MOE_F_STEPS - 1), MOE_F_STEPS - 1)
            jj = jnp.where(c == 0, 0, jj)
            return (layer, ce[cur], 0, half * MOE_F_STEPS + jj)
        return idx

    def d_idx(c, j, ce, cs, cn):
        cur = jnp.maximum(c - 1, 0)
        jj = jnp.where(cn[cur] > 0, jnp.maximum(j - MOE_F_STEPS, 0), MOE_N_STEPS - 1)
        jj = jnp.where(c == 0, 0, jj)
        return (layer, ce[cur], 0, jj)

    def tok_idx(i):
        def idx(c, j, ce, cs, cn):
            nxt = jnp.minimum(c, n_chunks - 1)
            return (jnp.minimum(cs[nxt] // MOE_SUB + i, n_sub_blocks - 1), 0, 0)
        return idx

    grid_spec = pltpu.PrefetchScalarGridSpec(
        num_scalar_prefetch=3,
        grid=(used_chunks + 1, MOE_STEPS),
        in_specs=[pl.BlockSpec((None, 1, MOE_SUB), tok_idx(i), memory_space=pltpu.SMEM) for i in range(MOE_NSUB)] + [
            pl.BlockSpec(memory_space=pl.ANY),
            pl.BlockSpec((None, None, D_MODEL, MOE_TF), gu_idx(0)),
            pl.BlockSpec((None, None, D_MODEL, MOE_TF), gu_idx(1)),
            pl.BlockSpec((None, None, 1, MOE_TF), gu_idx(0)),
            pl.BlockSpec((None, None, 1, MOE_TF), gu_idx(1)),
            pl.BlockSpec((None, None, D_FF, MOE_TF), d_idx),
            pl.BlockSpec((None, None, 1, MOE_TF), d_idx),
        ],
        out_specs=pl.BlockSpec(memory_space=pl.ANY),
        scratch_shapes=[
            pltpu.VMEM((MOE_CHUNK, HALF_D), jnp.uint32),
            pltpu.VMEM((MOE_CHUNK, D_MODEL), BF16),
            pltpu.VMEM((MOE_F_STEPS, MOE_CHUNK, MOE_TF), BF16),
            pltpu.VMEM((2, MOE_CHUNK, MOE_TF), F32),
            pltpu.SemaphoreType.DMA((MOE_NSUB,)),
            pltpu.SemaphoreType.DMA((2,)),
        ],
    )
    bgu = b_gate_up.reshape(DEPTH, N_EXPERTS, 1, 2 * D_FF)
    bd = b_down.reshape(DEPTH, N_EXPERTS, 1, D_MODEL)
    return pl.pallas_call(
        _moe_kernel,
        grid_spec=grid_spec,
        out_shape=jax.ShapeDtypeStruct((rows, D_MODEL), F32),
        compiler_params=pltpu.CompilerParams(
            dimension_semantics=("arbitrary", "arbitrary"), vmem_limit_bytes=VMEM_LIMIT),
        name="moe_experts",
    )(chunk_e, chunk_start, chunk_nsub, *([buf_tok.reshape(n_sub_blocks, 1, MOE_SUB)] * MOE_NSUB), f_packed,
      w_gate_up, w_gate_up, bgu, bgu, w_down, bd)


CMB_TM = 256


def _combine_kernel(idx_ref, yb_hbm, gates_ref, x_ref, mod_ref, *rest, final):
    if final:
        g_ref, o_ref, buf, sem = rest
    else:
        o_ref, buf, sem = rest

    for k in range(TOP_K):
        def issue(r, carry, k=k):
            pltpu.make_async_copy(yb_hbm.at[pl.ds(idx_ref[0, k * CMB_TM + r], 1)], buf.at[k, pl.ds(r, 1)],
                                  sem.at[k]).start()
            return carry

        lax.fori_loop(0, CMB_TM, issue, 0, unroll=8)

    gates = gates_ref[...]
    y = None
    for k in range(TOP_K):
        _row_copy_wait(yb_hbm, buf.at[k], sem.at[k], CMB_TM)
        t = gates[:, k:k + 1] * buf[k]
        y = t if y is None else y + t
    xn = x_ref[...] + mod_ref[5:6, :] * y
    if final:
        xn = xn * lax.rsqrt(jnp.mean(xn * xn, axis=-1, keepdims=True) + EPS) * g_ref[...]
    o_ref[...] = xn


def _combine(yb, dest, gates, x_mid, mod, final_g, *, row_tile0, ctx_tiles):
    n = x_mid.shape[0]
    n_tiles = n // CMB_TM
    final = final_g is not None
    idx = dest.reshape(n_tiles, CMB_TM, TOP_K).transpose(0, 2, 1).reshape(n_tiles, 1, TOP_K * CMB_TM)
    in_specs = [
        pl.BlockSpec((None, 1, TOP_K * CMB_TM), lambda t: (t, 0, 0), memory_space=pltpu.SMEM),
        pl.BlockSpec(memory_space=pl.ANY),
        pl.BlockSpec((CMB_TM, TOP_K), lambda t: (t, 0)),
        pl.BlockSpec((CMB_TM, D_MODEL), lambda t: (t, 0)),
        pl.BlockSpec((None, 6, D_MODEL), lambda t: (_mod_row(t + row_tile0, CMB_TM, ctx_tiles), 0, 0)),
    ]
    args = [idx, yb, gates, x_mid, mod]
    if final:
        in_specs.append(pl.BlockSpec((1, D_MODEL), lambda t: (0, 0)))
        args.append(final_g.reshape(1, D_MODEL))
    return pl.pallas_call(
        functools.partial(_combine_kernel, final=final),
        grid=(n_tiles,),
        in_specs=in_specs,
        out_specs=pl.BlockSpec((CMB_TM, D_MODEL), lambda t: (t, 0)),
        out_shape=jax.ShapeDtypeStruct((n, D_MODEL), F32),
        scratch_shapes=[pltpu.VMEM((TOP_K, CMB_TM, D_MODEL), F32), pltpu.SemaphoreType.DMA((TOP_K,))],
        compiler_params=pltpu.CompilerParams(dimension_semantics=("arbitrary",), vmem_limit_bytes=VMEM_LIMIT),
        name="moe_combine",
    )(*args)


def _moe_ffn(layer, f_packed, logits, w_gate_up, b_gate_up, w_down, b_down):
    n = f_packed.shape[0]
    slots = n * TOP_K
    buf_rows = _moe_rows(n)
    n_chunks = _moe_chunks(n)

    top_logit, top_e = lax.top_k(logits, TOP_K)
    gates = jax.nn.softmax(top_logit, axis=-1)
    flat_e = top_e.reshape(-1).astype(jnp.int32)
    onehot = (flat_e[:, None] == jnp.arange(N_EXPERTS, dtype=jnp.int32)[None, :]).astype(jnp.int32)
    running = jnp.cumsum(onehot, axis=0)
    counts = running[-1]
    padded = (counts + MOE_SUB - 1) // MOE_SUB * MOE_SUB
    pad_end = jnp.cumsum(padded)
    pad_start = pad_end - padded
    dest_of_slot = jnp.sum(onehot * (running - 1 + pad_start[None, :]), axis=1)
    buf_tok = jnp.zeros((buf_rows,), jnp.int32).at[dest_of_slot].set(jnp.arange(slots, dtype=jnp.int32) // TOP_K)

    e_chunks = (padded + MOE_CHUNK - 1) // MOE_CHUNK
    chunk_end = jnp.cumsum(e_chunks)
    total_chunks = chunk_end[-1]
    cidx = jnp.arange(n_chunks, dtype=jnp.int32)
    ce = jnp.minimum(jnp.searchsorted(chunk_end, cidx, side='right'), N_EXPERTS - 1).astype(jnp.int32)
    local = cidx - (chunk_end[ce] - e_chunks[ce])
    valid = cidx < total_chunks
    c_start = jnp.where(valid, pad_start[ce] + local * MOE_CHUNK, 0).astype(jnp.int32)
    c_nsub = jnp.where(valid, jnp.minimum(MOE_CHUNK, padded[ce] - local * MOE_CHUNK) // MOE_SUB, 0).astype(jnp.int32)
    last_e = ce[jnp.maximum(total_chunks - 1, 0)]
    ce = jnp.where(valid, ce, last_e).astype(jnp.int32)

    yb = _moe_experts(layer, ce, c_start, c_nsub, total_chunks.astype(jnp.int32), buf_tok, f_packed,
                      w_gate_up, b_gate_up, w_down, b_down)
    return yb, dest_of_slot.reshape(n, TOP_K), gates


MIX_TB = 256
CONV_HALO = 16
CONV_PIECE = 64
NEG_INF = float("-inf")


def _conv_pitch(rowlen):
    return rowlen + 2 * CONV_HALO


def _dwconv_block(u, pad_ref, w_ref, rowlen):
    taps = w_ref.shape[0]
    half = taps // 2
    pitch = _conv_pitch(rowlen)
    ch = u.shape[1]
    zeros = jnp.zeros((CONV_HALO, ch), F32)
    for r in range(MIX_TB // rowlen):
        base = r * pitch
        pad_ref[base:base + CONV_HALO, :] = zeros
        pad_ref[base + CONV_HALO:base + CONV_HALO + rowlen, :] = u[r * rowlen:(r + 1) * rowlen, :]
        pad_ref[base + CONV_HALO + rowlen:base + pitch, :] = zeros
    outs = []
    for r in range(MIX_TB // rowlen):
        for piece in range(rowlen // CONV_PIECE):
            acc = None
            for j in range(taps):
                off = r * pitch + CONV_HALO + piece * CONV_PIECE + j - half
                term = pad_ref[off:off + CONV_PIECE, :] * w_ref[j:j + 1, :]
                acc = term if acc is None else acc + term
            outs.append(acc)
    return jnp.concatenate(outs, axis=0)


def _ln_rows(x, g, b):
    xc = x - jnp.mean(x, axis=-1, keepdims=True)
    return xc * lax.rsqrt(jnp.mean(xc * xc, axis=-1, keepdims=True) + EPS) * g + b


def _silu(x):
    return x * jax.nn.sigmoid(x)


SSD_PAIRS = SSD_HEADS // 2
SSD_PAIR_W = 2 * SSD_HEAD_DIM


def _ssd_kernel(xbc_ref, dt_ref, init_ref, cw_ref, cb_ref, dtb_ref, a_ref, skip_ref, y_ref, fin_ref,
                pad_ref, st_ref, *, rowlen, rev, dcol):
    s = pl.program_id(1)

    @pl.when(s == 0)
    def _():
        st_ref[...] = init_ref[...]

    xa = _silu(_dwconv_block(xbc_ref[...], pad_ref, cw_ref, rowlen) + cb_ref[...])
    dt_all = dt_ref[...] + dtb_ref[...]
    dt_all = jnp.maximum(dt_all, 0.0) + jnp.log1p(jnp.exp(-jnp.abs(dt_all)))
    da_all = dt_all * a_ref[...]

    row_i = lax.broadcasted_iota(jnp.int32, (SSD_CHUNK, SSD_CHUNK), 0)
    col_i = lax.broadcasted_iota(jnp.int32, (SSD_CHUNK, SSD_CHUNK), 1)
    tri = (col_i >= row_i) if rev else (col_i <= row_i)
    tri_f = tri.astype(F32)
    first_half = lax.broadcasted_iota(jnp.int32, (SSD_CHUNK, SSD_PAIR_W), 1) < SSD_HEAD_DIM
    first_half_row = first_half[0:1, :]

    chunks = range(MIX_TB // SSD_CHUNK)
    for ci in (reversed(chunks) if rev else chunks):
        rows = slice(ci * SSD_CHUNK, (ci + 1) * SSD_CHUNK)
        acc = jnp.dot(tri_f, da_all[rows, :], preferred_element_type=F32, precision=lax.Precision.HIGHEST)
        acc_t = acc.T
        tot = acc[0:1, :] if rev else acc[SSD_CHUNK - 1:SSD_CHUNK, :]
        to_end = jnp.exp(tot - acc)
        from_start = jnp.exp(acc)
        chunk_decay = jnp.exp(tot)
        dt_c = dt_all[rows, :]
        for g in range(SSD_GROUPS):
            bg = xa[rows, GROUP_W + g * SSD_STATE:GROUP_W + (g + 1) * SSD_STATE]
            cg = xa[rows, GROUP_W + (SSD_GROUPS + g) * SSD_STATE:GROUP_W + (SSD_GROUPS + g + 1) * SSD_STATE]
            scores = lax.dot_general(cg.astype(BF16), bg.astype(BF16), (((1,), (1,)), ((), ())),
                                     preferred_element_type=F32)
            pairs_per_group = SSD_PAIRS // SSD_GROUPS
            for p in range(g * pairs_per_group, (g + 1) * pairs_per_group):
                c0 = dcol + 2 * p
                c1 = c0 + 1
                lhs, bw = [], []
                for col in (c0, c1):
                    seg = acc[:, col:col + 1] - acc_t[col:col + 1, :]
                    lhs.append(scores * jnp.exp(jnp.where(tri, seg, NEG_INF)))
                for col in (c0, c1):
                    lhs.append(cg * from_start[:, col:col + 1])
                    bw.append((bg * to_end[:, col:col + 1]).T)
                xp = xa[rows, p * SSD_PAIR_W:(p + 1) * SSD_PAIR_W]
                xd = xp * jnp.where(first_half, dt_c[:, c0:c0 + 1], dt_c[:, c1:c1 + 1])
                xd_top = jnp.where(first_half, xd, 0.0).astype(BF16)
                xd_bot = jnp.where(first_half, 0.0, xd).astype(BF16)
                st = st_ref[p]
                st_top = jnp.where(first_half, st, 0.0).astype(BF16)
                st_bot = jnp.where(first_half, 0.0, st).astype(BF16)
                y = jnp.dot(jnp.concatenate(lhs, axis=1).astype(BF16),
                            jnp.concatenate([xd_top, xd_bot, st_top, st_bot], axis=0),
                            preferred_element_type=F32)
                y_ref[rows, p * SSD_PAIR_W:(p + 1) * SSD_PAIR_W] = y + skip_ref[:, p * SSD_PAIR_W:(p + 1) * SSD_PAIR_W] * xp
                upd = jnp.dot(jnp.concatenate(bw, axis=1).astype(BF16), jnp.concatenate([xd_top, xd_bot], axis=0),
                              preferred_element_type=F32)
                decay_lane = jnp.where(first_half_row, chunk_decay[:, c0:c0 + 1], chunk_decay[:, c1:c1 + 1])
                st_ref[p] = st * decay_lane + upd

    @pl.when(s == pl.num_programs(1) - 1)
    def _():
        fin_ref[...] = st_ref[...]


def _ssd_sweep(xbc_src, xbc_col_block, dt_src, row_block0, n_blocks, init, lw_ssd, *, rowlen, rev, direction):
    cw, cb, dtb_row, a_row, skip_row = lw_ssd[direction]

    def blk(b, s):
        return row_block0 + b * n_blocks + ((n_blocks - 1 - s) if rev else s)

    def out_blk(b, s):
        return b * n_blocks + ((n_blocks - 1 - s) if rev else s)

    state_shape = (BATCH, SSD_PAIRS, SSD_STATE, SSD_PAIR_W)
    return pl.pallas_call(
        functools.partial(_ssd_kernel, rowlen=rowlen, rev=rev, dcol=direction * SSD_HEADS),
        grid=(BATCH, n_blocks),
        in_specs=[
            pl.BlockSpec((MIX_TB, SSD_XBC), lambda b, s: (blk(b, s), xbc_col_block)),
            pl.BlockSpec((MIX_TB, LANE), lambda b, s: (blk(b, s), 0)),
            pl.BlockSpec((None, SSD_PAIRS, SSD_STATE, SSD_PAIR_W), lambda b, s: (b, 0, 0, 0)),
            pl.BlockSpec((3, SSD_XBC), lambda b, s: (0, 0)),
            pl.BlockSpec((1, SSD_XBC), lambda b, s: (0, 0)),
            pl.BlockSpec((1, LANE), lambda b, s: (0, 0)),
            pl.BlockSpec((1, LANE), lambda b, s: (0, 0)),
            pl.BlockSpec((1, GROUP_W), lambda b, s: (0, 0)),
        ],
        out_specs=[
            pl.BlockSpec((MIX_TB, GROUP_W), lambda b, s: (out_blk(b, s), 0)),
            pl.BlockSpec((None, SSD_PAIRS, SSD_STATE, SSD_PAIR_W), lambda b, s: (b, 0, 0, 0)),
        ],
        out_shape=[
            jax.ShapeDtypeStruct((BATCH * n_blocks * MIX_TB, GROUP_W), F32),
            jax.ShapeDtypeStruct(state_shape, F32),
        ],
        scratch_shapes=[
            pltpu.VMEM(((MIX_TB // rowlen) * _conv_pitch(rowlen), SSD_XBC), F32),
            pltpu.VMEM((SSD_PAIRS, SSD_STATE, SSD_PAIR_W), F32),
        ],
        compiler_params=pltpu.CompilerParams(
            dimension_semantics=("arbitrary", "arbitrary"), vmem_limit_bytes=VMEM_LIMIT),
        name="ssd_sweep",
    )(xbc_src, dt_src, init, cw, cb, dtb_row, a_row, skip_row)


def _ssd_params(conv_w, conv_b, dt_bias, a_log, d_skip):
    out = []
    pad = LANE - 2 * SSD_HEADS
    dtb_row = jnp.pad(dt_bias.reshape(-1), (0, pad)).reshape(1, LANE)
    a_row = jnp.pad(-jnp.exp(a_log.reshape(-1)), (0, pad)).reshape(1, LANE)
    for d in range(2):
        skip_row = jnp.repeat(d_skip[d], SSD_HEAD_DIM).reshape(1, GROUP_W)
        out.append((conv_w, conv_b.reshape(1, SSD_XBC), dtb_row, a_row, skip_row))
    return out


def _local_kernel(p_ref, z_ref, y0_ref, y1_ref, scw_ref, cfw_ref, cfb_ref, cfg_ref, cfbeta_ref, sgg_ref,
                  sgbeta_ref, sgw_ref, sgb_ref, ng_ref, m_ref, pad_ref, *, rowlen):
    gate_b = p_ref[:, 0:GROUP_W]
    u = p_ref[:, GROUP_W:2 * GROUP_W] * p_ref[:, 2 * GROUP_W:3 * GROUP_W]
    m_ref[:, 0:GROUP_W] = gate_b * _dwconv_block(u, pad_ref, scw_ref, rowlen)
    u = p_ref[:, OFF_CF:OFF_CF + GROUP_W] * jax.nn.sigmoid(p_ref[:, OFF_CF + GROUP_W:OFF_SG])
    u = _dwconv_block(u, pad_ref, cfw_ref, rowlen) + cfb_ref[...]
    m_ref[:, GROUP_W:2 * GROUP_W] = _silu(_ln_rows(u, cfg_ref[...], cfbeta_ref[...]))
    q = p_ref[:, OFF_SG:OFF_SSD]
    q = 0.5 * q * (1.0 + lax.erf(q * (2.0 ** -0.5)))
    v = _ln_rows(q[:, GROUP_W:], sgg_ref[...], sgbeta_ref[...]).astype(BF16)
    n_chunks = MIX_TB // SG_CHUNK
    for h in range(SG_HEADS):
        cols = slice(h * SG_HEAD_DIM, (h + 1) * SG_HEAD_DIM)
        rhs = jnp.concatenate([v[c * SG_CHUNK:(c + 1) * SG_CHUNK, cols] for c in range(n_chunks)], axis=1)
        sres = jnp.dot(sgw_ref[h].astype(BF16), rhs, preferred_element_type=F32) + sgb_ref[:, h:h + 1]
        for c in range(n_chunks):
            rows = slice(c * SG_CHUNK, (c + 1) * SG_CHUNK)
            m_ref[rows, 2 * GROUP_W + h * SG_HEAD_DIM:2 * GROUP_W + (h + 1) * SG_HEAD_DIM] = (
                q[rows, cols] * sres[:, c * SG_HEAD_DIM:(c + 1) * SG_HEAD_DIM])
    yv = (y0_ref[...] + y1_ref[...]) * _silu(z_ref[...])
    gw = GROUP_W // SSD_GROUPS
    for g in range(SSD_GROUPS):
        vg = yv[:, g * gw:(g + 1) * gw]
        vg = vg * lax.rsqrt(jnp.mean(vg * vg, axis=-1, keepdims=True) + EPS)
        m_ref[:, 3 * GROUP_W + g * gw:3 * GROUP_W + (g + 1) * gw] = vg * ng_ref[:, g * gw:(g + 1) * gw]


def _local_mixers_call(p_src, z_col_block, row_block0, n_blocks, y0, y1, lw, *, rowlen):
    vec = lambda a: a.reshape(1, GROUP_W)
    args = [lw['sc_conv_w'], lw['cf_conv_w'], vec(lw['cf_conv_b']), vec(lw['cf_ln_g']), vec(lw['cf_ln_b']),
            vec(lw['sg_ln_g']), vec(lw['sg_ln_b']), lw['sg_w'], lw['sg_b'].T, vec(lw['ssd_norm_g'])]
    full = lambda a: pl.BlockSpec(a.shape, lambda t, nd=a.ndim: (0,) * nd)
    return pl.pallas_call(
        functools.partial(_local_kernel, rowlen=rowlen),
        grid=(n_blocks,),
        in_specs=[
            pl.BlockSpec((MIX_TB, OFF_SSD), lambda t: (t + row_block0, 0)),
            pl.BlockSpec((MIX_TB, GROUP_W), lambda t: (t + row_block0, z_col_block)),
            pl.BlockSpec((MIX_TB, GROUP_W), lambda t: (t, 0)),
            pl.BlockSpec((MIX_TB, GROUP_W), lambda t: (t, 0)),
        ] + [full(a) for a in args],
        out_specs=pl.BlockSpec((MIX_TB, D_MODEL), lambda t: (t, 0)),
        out_shape=jax.ShapeDtypeStruct((n_blocks * MIX_TB, D_MODEL), F32),
        scratch_shapes=[pltpu.VMEM(((MIX_TB // rowlen) * _conv_pitch(rowlen), GROUP_W), F32)],
        compiler_params=pltpu.CompilerParams(dimension_semantics=("arbitrary",), vmem_limit_bytes=VMEM_LIMIT),
        name="local_mixers",
    )(p_src, p_src, y0, y1, *args)


def _token_mixers(p_ctx_src, ctx_cols, p_lat_src, lat_cols, dt_ctx, dt_lat, lat_row_block0, lw, ctx_out):
    prm = _ssd_params(lw['ssd_conv_w'], lw['ssd_conv_b'], lw['ssd_dt_bias'], lw['ssd_a_log'], lw['ssd_d'])
    zero_state = jnp.zeros((BATCH, SSD_PAIRS, SSD_STATE, SSD_PAIR_W), F32)
    ctx_blocks = CTX_LEN // MIX_TB
    lat_blocks = SEQ // MIX_TB
    y_ctx, y_lat = [], []
    for d in range(2):
        rev = d == 1
        yc, state = _ssd_sweep(p_ctx_src, ctx_cols[1], dt_ctx, 0, ctx_blocks, zero_state, prm,
                               rowlen=CTX_LEN, rev=rev, direction=d)
        yl, _ = _ssd_sweep(p_lat_src, lat_cols[1], dt_lat, lat_row_block0, lat_blocks, state, prm,
                           rowlen=GRID_W, rev=rev, direction=d)
        y_ctx.append(yc)
        y_lat.append(yl)
    m_lat = _local_mixers_call(p_lat_src, lat_cols[0], lat_row_block0, BATCH * lat_blocks, y_lat[0], y_lat[1], lw,
                               rowlen=GRID_W)
    m_ctx = None
    if ctx_out:
        m_ctx = _local_mixers_call(p_ctx_src, ctx_cols[0], 0, BATCH * ctx_blocks, y_ctx[0], y_ctx[1], lw,
                                   rowlen=CTX_LEN)
    return m_lat, m_ctx


def kernel(x, c, ctx, c_ctx, w_mod, b_mod, norm1_g, norm2_g, w_in, b_in, sc_conv_w, cf_conv_w, cf_conv_b,
           cf_ln_g, cf_ln_b, sg_ln_g, sg_ln_b, sg_w, sg_b, ssd_conv_w, ssd_conv_b, ssd_dt_bias, ssd_a_log,
           ssd_d, ssd_norm_g, w_out, b_out, w_router, b_router, w_gate_up, b_gate_up, w_down, b_down,
           final_norm_g):
    cc = jnp.concatenate([c_ctx[None, :], c, jnp.zeros((MOD_ROWS - 1 - BATCH, D_MODEL), F32)], axis=0)
    mod_all = _modulation(cc, w_mod, b_mod).reshape(DEPTH, MOD_ROWS, 6, D_MODEL)

    x_all = jnp.concatenate([ctx.reshape(N_CTX, D_MODEL), x.reshape(N_LAT, D_MODEL)], axis=0)

    for i in range(DEPTH):
        last = i == DEPTH - 1
        lw = dict(sc_conv_w=sc_conv_w[i], cf_conv_w=cf_conv_w[i], cf_conv_b=cf_conv_b[i], cf_ln_g=cf_ln_g[i],
                  cf_ln_b=cf_ln_b[i], sg_ln_g=sg_ln_g[i], sg_ln_b=sg_ln_b[i], sg_w=sg_w[i], sg_b=sg_b[i],
                  ssd_conv_w=ssd_conv_w[i], ssd_conv_b=ssd_conv_b[i], ssd_dt_bias=ssd_dt_bias[i],
                  ssd_a_log=ssd_a_log[i], ssd_d=ssd_d[i], ssd_norm_g=ssd_norm_g[i])
        mod = mod_all[i]
        w_in_bf = w_in[i, :, :MAIN_COLS].astype(BF16)
        b_in_main = b_in[i, :MAIN_COLS].reshape(1, MAIN_COLS)
        wdt_bf = jnp.pad(w_in[i, :, MAIN_COLS:], ((0, 0), (0, LANE - DT_COLS))).astype(BF16)
        bdt = jnp.pad(b_in[i, MAIN_COLS:], (0, LANE - DT_COLS)).reshape(1, LANE)
        w_out_bf = w_out[i].astype(BF16)
        wr_bf = jnp.pad(w_router[i], ((0, 0), (0, LANE - N_EXPERTS))).astype(BF16)
        br = jnp.pad(b_router[i], (0, LANE - N_EXPERTS)).reshape(1, LANE)
        in_ctx_tiles = N_CTX // IN_TM
        proj = functools.partial(_in_projection, x_all, norm1_g[i], mod, w_in_bf, b_in_main, wdt_bf, bdt,
                                 ctx_tiles=in_ctx_tiles)

        z_xbc_cols = (OFF_SSD // GROUP_W, (OFF_SSD + GROUP_W) // SSD_XBC)
        if not last:
            p_all, dt_all = proj(row_tile0=0, n_row_tiles=(N_CTX + N_LAT) // IN_TM,
                                 col_tile0=0, n_col_tiles=MAIN_COLS // IN_TN)
            m_lat, m_ctx = _token_mixers(p_all, z_xbc_cols, p_all, z_xbc_cols, dt_all, dt_all, N_CTX // MIX_TB, lw, True)
        else:
            p_lat, dt_lat = proj(row_tile0=in_ctx_tiles, n_row_tiles=N_LAT // IN_TM,
                                 col_tile0=0, n_col_tiles=MAIN_COLS // IN_TN)
            ctx_col0 = MAIN_COLS - 2 * SSD_XBC
            p_ctx, dt_ctx = proj(row_tile0=0, n_row_tiles=in_ctx_tiles,
                                 col_tile0=ctx_col0 // IN_TN, n_col_tiles=(MAIN_COLS - ctx_col0) // IN_TN)
            ctx_cols = ((OFF_SSD - ctx_col0) // GROUP_W, (OFF_SSD + GROUP_W - ctx_col0) // SSD_XBC)
            m_lat, m_ctx = _token_mixers(p_ctx, ctx_cols, p_lat, z_xbc_cols, dt_ctx, dt_lat, 0, lw, False)
        out_ctx_tiles = N_CTX // OUT_TM
        if not last:
            m_all = jnp.concatenate([m_ctx, m_lat], axis=0)
            row_tile0 = 0
        else:
            m_all = m_lat
            row_tile0 = out_ctx_tiles
        x_mid, f_packed, logits = _out_projection(m_all, x_all, mod, norm2_g[i], w_out_bf,
                                                  b_out[i].reshape(1, D_MODEL), wr_bf, br,
                                                  row_tile0=row_tile0, ctx_tiles=out_ctx_tiles)
        yb, dest, gates = _moe_ffn(i, f_packed, logits[:, :N_EXPERTS], w_gate_up, b_gate_up, w_down, b_down)
        x_all = _combine(yb, dest, gates, x_mid, mod, final_norm_g if last else None,
                         row_tile0=row_tile0 * OUT_TM // CMB_TM, ctx_tiles=N_CTX // CMB_TM)

    return x_all.reshape(BATCH, SEQ, D_MODEL)
```

```python
import functools

import jax
import jax.numpy as jnp
from jax import lax
from jax.experimental import pallas as pl
from jax.experimental.pallas import tpu as pltpu

F32 = jnp.float32
BF16 = jnp.bfloat16

D_MODEL = 2048
BATCH = 4
SEQ = 2048
DEPTH = 2
GRID_W = 64
CTX_LEN = 256
EPS = 1e-6
GROUP_W = 512
SG_HEADS = 4
SG_CHUNK = 128
SG_HEAD_DIM = 128
SSD_HEAD_DIM = 64
SSD_HEADS = 8
SSD_GROUPS = 2
SSD_STATE = 128
SSD_CHUNK = 128
SSD_XBC = 1024
N_EXPERTS = 32
TOP_K = 4
D_FF = 2048
SWIGLU_LIMIT = 7.0
SWIGLU_ALPHA = 1.702
OFF_CF = 1536
OFF_SG = 2560
OFF_SSD = 3584
MAIN_COLS = 5120
DT_COLS = 2 * SSD_HEADS
LANE = 128
HALF_D = D_MODEL // 2

N_CTX = BATCH * CTX_LEN
N_LAT = BATCH * SEQ

VMEM_LIMIT = 56 * 1024 * 1024

MOD_ROWS = 8
MOD_TN = 1024


def _mod_kernel(c_ref, w_ref, b_ref, o_ref):
    c = c_ref[...]
    s = c * jax.nn.sigmoid(c)
    o_ref[...] = jnp.dot(s.astype(BF16), w_ref[...].astype(BF16), preferred_element_type=F32) + b_ref[...]


def _modulation(cc, w_mod, b_mod):
    n_out = 6 * D_MODEL
    return pl.pallas_call(
        _mod_kernel,
        grid=(DEPTH, n_out // MOD_TN),
        in_specs=[
            pl.BlockSpec((MOD_ROWS, D_MODEL), lambda l, n: (0, 0)),
            pl.BlockSpec((None, D_MODEL, MOD_TN), lambda l, n: (l, 0, n)),
            pl.BlockSpec((None, 1, MOD_TN), lambda l, n: (l, 0, n)),
        ],
        out_specs=pl.BlockSpec((None, MOD_ROWS, MOD_TN), lambda l, n: (l, 0, n)),
        out_shape=jax.ShapeDtypeStruct((DEPTH, MOD_ROWS, n_out), F32),
        compiler_params=pltpu.CompilerParams(
            dimension_semantics=("arbitrary", "arbitrary"), vmem_limit_bytes=VMEM_LIMIT),
        name="adaln_mod",
    )(cc, w_mod, b_mod.reshape(DEPTH, 1, n_out))


def _mod_row(tile, tile_rows, ctx_tiles):
    tiles_per_batch = SEQ // tile_rows
    return jnp.where(tile < ctx_tiles, 0, 1 + (tile - ctx_tiles) // tiles_per_batch)


IN_TM = 1024
IN_TN = 1024
IN_PRO_ROWS = 256


def _inproj_kernel(x_ref, g_ref, mod_ref, w_ref, b_ref, wdt_ref, bdt_ref, o_ref, odt_ref, h_ref):
    @pl.when(pl.program_id(1) == 0)
    def _():
        g = g_ref[...]
        scale = 1.0 + mod_ref[1:2, :]
        shift = mod_ref[0:1, :]
        for r in range(IN_TM // IN_PRO_ROWS):
            rows = slice(r * IN_PRO_ROWS, (r + 1) * IN_PRO_ROWS)
            x = x_ref[rows, :]
            y = x * lax.rsqrt(jnp.mean(x * x, axis=-1, keepdims=True) + EPS)
            h_ref[rows, :] = ((y * g) * scale + shift).astype(BF16)
        odt_ref[...] = jnp.dot(h_ref[...], wdt_ref[...], preferred_element_type=F32) + bdt_ref[...]

    o_ref[...] = jnp.dot(h_ref[...], w_ref[...], preferred_element_type=F32) + b_ref[...]


def _in_projection(x_all, norm_g, mod, w_bf, b, wdt_bf, bdt, *, row_tile0, n_row_tiles, col_tile0, n_col_tiles,
                   ctx_tiles):
    rows = n_row_tiles * IN_TM
    return pl.pallas_call(
        _inproj_kernel,
        grid=(n_row_tiles, n_col_tiles),
        in_specs=[
            pl.BlockSpec((IN_TM, D_MODEL), lambda m, n: (m + row_tile0, 0)),
            pl.BlockSpec((1, D_MODEL), lambda m, n: (0, 0)),
            pl.BlockSpec((None, 6, D_MODEL), lambda m, n: (_mod_row(m + row_tile0, IN_TM, ctx_tiles), 0, 0)),
            pl.BlockSpec((D_MODEL, IN_TN), lambda m, n: (0, n + col_tile0)),
            pl.BlockSpec((1, IN_TN), lambda m, n: (0, n + col_tile0)),
            pl.BlockSpec((D_MODEL, LANE), lambda m, n: (0, 0)),
            pl.BlockSpec((1, LANE), lambda m, n: (0, 0)),
        ],
        out_specs=[
            pl.BlockSpec((IN_TM, IN_TN), lambda m, n: (m, n)),
            pl.BlockSpec((IN_TM, LANE), lambda m, n: (m, 0)),
        ],
        out_shape=[
            jax.ShapeDtypeStruct((rows, n_col_tiles * IN_TN), F32),
            jax.ShapeDtypeStruct((rows, LANE), F32),
        ],
        scratch_shapes=[pltpu.VMEM((IN_TM, D_MODEL), BF16)],
        compiler_params=pltpu.CompilerParams(
            dimension_semantics=("arbitrary", "arbitrary"), vmem_limit_bytes=VMEM_LIMIT),
        name="in_proj",
    )(x_all, norm_g.reshape(1, D_MODEL), mod, w_bf, b, wdt_bf, bdt)


OUT_TM = 256


def _outproj_kernel(m_ref, x_ref, mod_ref, g_ref, w_ref, b_ref, wr_ref, br_ref, xo_ref, f_ref, lg_ref):
    y = jnp.dot(m_ref[...].astype(BF16), w_ref[...], preferred_element_type=F32) + b_ref[...]
    xn = x_ref[...] + mod_ref[2:3, :] * y
    xo_ref[...] = xn
    r = lax.rsqrt(jnp.mean(xn * xn, axis=-1, keepdims=True) + EPS)
    f = ((xn * r) * g_ref[...]) * (1.0 + mod_ref[4:5, :]) + mod_ref[3:4, :]
    fb = f.astype(BF16)
    bits = lax.bitcast_convert_type(fb.astype(F32), jnp.uint32)
    f_ref[...] = (bits[:, HALF_D:] & jnp.uint32(0xFFFF0000)) | (bits[:, :HALF_D] >> 16)
    lg_ref[...] = jnp.dot(fb, wr_ref[...], preferred_element_type=F32) + br_ref[...]


def _out_projection(m, x_all, mod, norm_g, w_bf, b, wr_bf, br, *, row_tile0, ctx_tiles):
    rows = m.shape[0]
    n_tiles = rows // OUT_TM
    return pl.pallas_call(
        _outproj_kernel,
        grid=(n_tiles,),
        in_specs=[
            pl.BlockSpec((OUT_TM, D_MODEL), lambda t: (t, 0)),
            pl.BlockSpec((OUT_TM, D_MODEL), lambda t: (t + row_tile0, 0)),
            pl.BlockSpec((None, 6, D_MODEL), lambda t: (_mod_row(t + row_tile0, OUT_TM, ctx_tiles), 0, 0)),
            pl.BlockSpec((1, D_MODEL), lambda t: (0, 0)),
            pl.BlockSpec((D_MODEL, D_MODEL), lambda t: (0, 0)),
            pl.BlockSpec((1, D_MODEL), lambda t: (0, 0)),
            pl.BlockSpec((D_MODEL, LANE), lambda t: (0, 0)),
            pl.BlockSpec((1, LANE), lambda t: (0, 0)),
        ],
        out_specs=[
            pl.BlockSpec((OUT_TM, D_MODEL), lambda t: (t, 0)),
            pl.BlockSpec((OUT_TM, HALF_D), lambda t: (t, 0)),
            pl.BlockSpec((OUT_TM, LANE), lambda t: (t, 0)),
        ],
        out_shape=[
            jax.ShapeDtypeStruct((rows, D_MODEL), F32),
            jax.ShapeDtypeStruct((rows, HALF_D), jnp.uint32),
            jax.ShapeDtypeStruct((rows, LANE), F32),
        ],
        compiler_params=pltpu.CompilerParams(
            dimension_semantics=("arbitrary",), vmem_limit_bytes=VMEM_LIMIT),
        name="out_proj",
    )(m, x_all, mod, norm_g.reshape(1, D_MODEL), w_bf, b, wr_bf, br)


MOE_SUB = 256
MOE_CHUNK = 2048
MOE_NSUB = MOE_CHUNK // MOE_SUB
MOE_TF = 256
MOE_F_STEPS = D_FF // MOE_TF
MOE_N_STEPS = D_MODEL // MOE_TF
MOE_STEPS = MOE_F_STEPS + MOE_N_STEPS


def _moe_rows(n_tokens):
    slots = n_tokens * TOP_K
    padded = slots + N_EXPERTS * (MOE_SUB - 1)
    padded = -(-padded // MOE_SUB) * MOE_SUB
    return padded


def _moe_chunks(n_tokens):
    return _moe_rows(n_tokens) // MOE_CHUNK + N_EXPERTS


def _row_copy_wait(src_hbm, dst, sem, rows):
    pltpu.make_async_copy(src_hbm.at[pl.ds(0, rows)], dst, sem).wait()


def _moe_kernel(ce_ref, cs_ref, cn_ref, *refs):
    (tok_ref, f_hbm, wg_ref, wu_ref, bg_ref, bu_ref, wd_ref, bd_ref, yb_hbm, xw, xs, hs, os_, sem_in, sem_out) = refs
    c = pl.program_id(0)
    j = pl.program_id(1)
    cur = jnp.maximum(c - 1, 0)
    nsub = jnp.where(c >= 1, cn_ref[cur], 0)
    start = pl.multiple_of(cs_ref[cur], MOE_SUB)

    def gather_issue():
        base = pl.multiple_of(j * MOE_SUB, MOE_SUB)
        for r in range(MOE_SUB):
            pltpu.make_async_copy(f_hbm.at[pl.ds(tok_ref[0, r], 1)], xw.at[pl.ds(base + r, 1)], sem_in.at[j]).start()

    def gather_wait(i):
        _row_copy_wait(f_hbm, xw.at[pl.ds(i * MOE_SUB, MOE_SUB)], sem_in.at[i], MOE_SUB)

    def unpack(i):
        rows = slice(i * MOE_SUB, (i + 1) * MOE_SUB)
        w = xw[rows, :]
        xs[rows, :HALF_D] = lax.bitcast_convert_type(w << 16, F32).astype(BF16)
        xs[rows, HALF_D:] = lax.bitcast_convert_type(w & jnp.uint32(0xFFFF0000), F32).astype(BF16)

    def copy_out(slot, off, n, col):
        return pltpu.make_async_copy(
            os_.at[slot, pl.ds(off, n * MOE_SUB), :],
            yb_hbm.at[pl.ds(start + off, n * MOE_SUB), pl.ds(col, MOE_TF)],
            sem_out.at[slot])

    def for_valid_subs(fn):
        fn(0)
        for i in range(1, MOE_NSUB):
            @pl.when(i < nsub)
            def _(i=i):
                fn(i)

    def for_pieces(fn):
        quads = nsub // 4
        rem = nsub - 4 * quads
        for q in range(MOE_NSUB // 4):
            @pl.when(q < quads)
            def _(q=q):
                fn(q * 4 * MOE_SUB, 4, q == 0)
        base = pl.multiple_of(quads * (4 * MOE_SUB), MOE_SUB)

        @pl.when(rem >= 2)
        def _():
            fn(base, 2, False)

        @pl.when(rem % 2 == 1)
        def _():
            fn(pl.multiple_of(base + (rem // 2) * (2 * MOE_SUB), MOE_SUB), 1, False)

    @pl.when(jnp.logical_and(nsub > 0, j == 0))
    def _():
        for i in range(MOE_NSUB):
            gather_wait(i)
        for_valid_subs(unpack)

    @pl.when(jnp.logical_and(j < MOE_F_STEPS, nsub < 4))
    def _():
        gather_issue()

    @pl.when(nsub > 0)
    def _():
        @pl.when(j < MOE_F_STEPS)
        def _():
            bg = bg_ref[...]
            bu = bu_ref[...]

            def gate_up(off, n, first_quad):
                if first_quad:
                    gather_issue()
                rows = pl.ds(off, n * MOE_SUB)
                x = xs[rows, :]
                g = jnp.dot(x, wg_ref[...].astype(BF16), preferred_element_type=F32) + bg
                u = jnp.dot(x, wu_ref[...].astype(BF16), preferred_element_type=F32) + bu
                g = jnp.minimum(g, SWIGLU_LIMIT)
                u = jnp.clip(u, -SWIGLU_LIMIT, SWIGLU_LIMIT)
                h = (u + 1.0) * (g * jax.nn.sigmoid(SWIGLU_ALPHA * g))
                hs[j, rows, :] = h.astype(BF16)

            for_pieces(gate_up)

        @pl.when(j >= MOE_F_STEPS)
        def _():
            jn = j - MOE_F_STEPS
            slot = jn % 2
            col = pl.multiple_of(jn * MOE_TF, MOE_TF)
            bd = bd_ref[...]

            @pl.when(jn >= 2)
            def _():
                for_pieces(lambda off, n, _: copy_out(slot, off, n, col).wait())

            def down(off, n, _):
                rows = pl.ds(off, n * MOE_SUB)
                h = jnp.concatenate([hs[f, rows, :] for f in range(MOE_F_STEPS)], axis=1)
                os_[slot, rows, :] = jnp.dot(h, wd_ref[...].astype(BF16), preferred_element_type=F32) + bd
                copy_out(slot, off, n, col).start()

            for_pieces(down)

            @pl.when(jn == MOE_N_STEPS - 1)
            def _():
                for_pieces(lambda off, n, _: copy_out(1 - slot, off, n, col).wait())
                for_pieces(lambda off, n, _: copy_out(slot, off, n, col).wait())

    @pl.when(jnp.logical_and(c == pl.num_programs(0) - 1, j == MOE_STEPS - 1))
    def _():
        for i in range(MOE_NSUB):
            gather_wait(i)


def _moe_experts(layer, chunk_e, chunk_start, chunk_nsub, used_chunks, buf_tok, f_packed, w_gate_up, b_gate_up,
                 w_down, b_down):
    n_chunks = chunk_e.shape[0]
    rows = buf_tok.shape[0]
    n_sub_blocks = rows // MOE_SUB

    def gu_idx(half):
        def idx(c, j, ce, cs, cn):
            cur = jnp.maximum(c - 1, 0)
            jj = jnp.where(cn[cur] > 0, jnp.minimum(j, MOE_F_STEPS - 1), MOE_F_STEPS - 1)
            jj = jnp.where(c == 0, 0, jj)
            return (layer, ce[cur], 0, half * MOE_F_STEPS + jj)
        return idx

    def d_idx(c, j, ce, cs, cn):
        cur = jnp.maximum(c - 1, 0)
        jj = jnp.where(cn[cur] > 0, jnp.maximum(j - MOE_F_STEPS, 0), MOE_N_STEPS - 1)
        jj = jnp.where(c == 0, 0, jj)
        return (layer, ce[cur], 0, jj)

    def tok_idx(c, j, ce, cs, cn):
        nxt = jnp.minimum(c, n_chunks - 1)
        return (jnp.minimum(cs[nxt] // MOE_SUB + jnp.minimum(j, MOE_F_STEPS - 1), n_sub_blocks - 1), 0, 0)

    grid_spec = pltpu.PrefetchScalarGridSpec(
        num_scalar_prefetch=3,
        grid=(used_chunks + 1, MOE_STEPS),
        in_specs=[
            pl.BlockSpec((None, 1, MOE_SUB), tok_idx, memory_space=pltpu.SMEM),
            pl.BlockSpec(memory_space=pl.ANY),
            pl.BlockSpec((None, None, D_MODEL, MOE_TF), gu_idx(0)),
            pl.BlockSpec((None, None, D_MODEL, MOE_TF), gu_idx(1)),
            pl.BlockSpec((None, None, 1, MOE_TF), gu_idx(0)),
            pl.BlockSpec((None, None, 1, MOE_TF), gu_idx(1)),
            pl.BlockSpec((None, None, D_FF, MOE_TF), d_idx),
            pl.BlockSpec((None, None, 1, MOE_TF), d_idx),
        ],
        out_specs=pl.BlockSpec(memory_space=pl.ANY),
        scratch_shapes=[
            pltpu.VMEM((MOE_CHUNK, HALF_D), jnp.uint32),
            pltpu.VMEM((MOE_CHUNK, D_MODEL), BF16),
            pltpu.VMEM((MOE_F_STEPS, MOE_CHUNK, MOE_TF), BF16),
            pltpu.VMEM((2, MOE_CHUNK, MOE_TF), F32),
            pltpu.SemaphoreType.DMA((MOE_NSUB,)),
            pltpu.SemaphoreType.DMA((2,)),
        ],
    )
    bgu = b_gate_up.reshape(DEPTH, N_EXPERTS, 1, 2 * D_FF)
    bd = b_down.reshape(DEPTH, N_EXPERTS, 1, D_MODEL)
    return pl.pallas_call(
        _moe_kernel,
        grid_spec=grid_spec,
        out_shape=jax.ShapeDtypeStruct((rows, D_MODEL), F32),
        compiler_params=pltpu.CompilerParams(
            dimension_semantics=("arbitrary", "arbitrary"), vmem_limit_bytes=VMEM_LIMIT),
        name="moe_experts",
    )(chunk_e, chunk_start, chunk_nsub, buf_tok.reshape(n_sub_blocks, 1, MOE_SUB), f_packed,
      w_gate_up, w_gate_up, bgu, bgu, w_down, bd)


CMB_TM = 256


def _combine_kernel(idx_ref, yb_hbm, gates_ref, x_ref, mod_ref, *rest, final):
    if final:
        g_ref, o_ref, buf, sem = rest
    else:
        o_ref, buf, sem = rest

    for k in range(TOP_K):
        def issue(r, carry, k=k):
            pltpu.make_async_copy(yb_hbm.at[pl.ds(idx_ref[0, k * CMB_TM + r], 1)], buf.at[k, pl.ds(r, 1)],
                                  sem.at[k]).start()
            return carry

        lax.fori_loop(0, CMB_TM, issue, 0, unroll=8)

    gates = gates_ref[...]
    y = None
    for k in range(TOP_K):
        _row_copy_wait(yb_hbm, buf.at[k], sem.at[k], CMB_TM)
        t = gates[:, k:k + 1] * buf[k]
        y = t if y is None else y + t
    xn = x_ref[...] + mod_ref[5:6, :] * y
    if final:
        xn = xn * lax.rsqrt(jnp.mean(xn * xn, axis=-1, keepdims=True) + EPS) * g_ref[...]
    o_ref[...] = xn


def _combine(yb, dest, gates, x_mid, mod, final_g, *, row_tile0, ctx_tiles):
    n = x_mid.shape[0]
    n_tiles = n // CMB_TM
    final = final_g is not None
    idx = dest.reshape(n_tiles, CMB_TM, TOP_K).transpose(0, 2, 1).reshape(n_tiles, 1, TOP_K * CMB_TM)
    in_specs = [
        pl.BlockSpec((None, 1, TOP_K * CMB_TM), lambda t: (t, 0, 0), memory_space=pltpu.SMEM),
        pl.BlockSpec(memory_space=pl.ANY),
        pl.BlockSpec((CMB_TM, TOP_K), lambda t: (t, 0)),
        pl.BlockSpec((CMB_TM, D_MODEL), lambda t: (t, 0)),
        pl.BlockSpec((None, 6, D_MODEL), lambda t: (_mod_row(t + row_tile0, CMB_TM, ctx_tiles), 0, 0)),
    ]
    args = [idx, yb, gates, x_mid, mod]
    if final:
        in_specs.append(pl.BlockSpec((1, D_MODEL), lambda t: (0, 0)))
        args.append(final_g.reshape(1, D_MODEL))
    return pl.pallas_call(
        functools.partial(_combine_kernel, final=final),
        grid=(n_tiles,),
        in_specs=in_specs,
        out_specs=pl.BlockSpec((CMB_TM, D_MODEL), lambda t: (t, 0)),
        out_shape=jax.ShapeDtypeStruct((n, D_MODEL), F32),
        scratch_shapes=[pltpu.VMEM((TOP_K, CMB_TM, D_MODEL), F32), pltpu.SemaphoreType.DMA((TOP_K,))],
        compiler_params=pltpu.CompilerParams(dimension_semantics=("arbitrary",), vmem_limit_bytes=VMEM_LIMIT),
        name="moe_combine",
    )(*args)


def _moe_ffn(layer, f_packed, logits, w_gate_up, b_gate_up, w_down, b_down):
    n = f_packed.shape[0]
    slots = n * TOP_K
    buf_rows = _moe_rows(n)
    n_chunks = _moe_chunks(n)

    top_logit, top_e = lax.top_k(logits, TOP_K)
    gates = jax.nn.softmax(top_logit, axis=-1)
    flat_e = top_e.reshape(-1).astype(jnp.int32)
    onehot = (flat_e[:, None] == jnp.arange(N_EXPERTS, dtype=jnp.int32)[None, :]).astype(jnp.int32)
    running = jnp.cumsum(onehot, axis=0)
    counts = running[-1]
    padded = (counts + MOE_SUB - 1) // MOE_SUB * MOE_SUB
    pad_end = jnp.cumsum(padded)
    pad_start = pad_end - padded
    dest_of_slot = jnp.sum(onehot * (running - 1 + pad_start[None, :]), axis=1)
    buf_tok = jnp.zeros((buf_rows,), jnp.int32).at[dest_of_slot].set(jnp.arange(slots, dtype=jnp.int32) // TOP_K)

    e_chunks = (padded + MOE_CHUNK - 1) // MOE_CHUNK
    chunk_end = jnp.cumsum(e_chunks)
    total_chunks = chunk_end[-1]
    cidx = jnp.arange(n_chunks, dtype=jnp.int32)
    ce = jnp.minimum(jnp.searchsorted(chunk_end, cidx, side='right'), N_EXPERTS - 1).astype(jnp.int32)
    local = cidx - (chunk_end[ce] - e_chunks[ce])
    valid = cidx < total_chunks
    c_start = jnp.where(valid, pad_start[ce] + local * MOE_CHUNK, 0).astype(jnp.int32)
    c_nsub = jnp.where(valid, jnp.minimum(MOE_CHUNK, padded[ce] - local * MOE_CHUNK) // MOE_SUB, 0).astype(jnp.int32)
    last_e = ce[jnp.maximum(total_chunks - 1, 0)]
    ce = jnp.where(valid, ce, last_e).astype(jnp.int32)

    yb = _moe_experts(layer, ce, c_start, c_nsub, total_chunks.astype(jnp.int32), buf_tok, f_packed,
                      w_gate_up, b_gate_up, w_down, b_down)
    return yb, dest_of_slot.reshape(n, TOP_K), gates


MIX_TB = 256
CONV_HALO = 16
CONV_PIECE = 64
NEG_INF = float("-inf")


def _conv_pitch(rowlen):
    return rowlen + 2 * CONV_HALO


def _dwconv_block(u, pad_ref, w_ref, rowlen):
    taps = w_ref.shape[0]
    half = taps // 2
    pitch = _conv_pitch(rowlen)
    ch = u.shape[1]
    zeros = jnp.zeros((CONV_HALO, ch), F32)
    for r in range(MIX_TB // rowlen):
        base = r * pitch
        pad_ref[base:base + CONV_HALO, :] = zeros
        pad_ref[base + CONV_HALO:base + CONV_HALO + rowlen, :] = u[r * rowlen:(r + 1) * rowlen, :]
        pad_ref[base + CONV_HALO + rowlen:base + pitch, :] = zeros
    outs = []
    for r in range(MIX_TB // rowlen):
        for piece in range(rowlen // CONV_PIECE):
            acc = None
            for j in range(taps):
                off = r * pitch + CONV_HALO + piece * CONV_PIECE + j - half
                term = pad_ref[off:off + CONV_PIECE, :] * w_ref[j:j + 1, :]
                acc = term if acc is None else acc + term
            outs.append(acc)
    return jnp.concatenate(outs, axis=0)


def _ln_rows(x, g, b):
    xc = x - jnp.mean(x, axis=-1, keepdims=True)
    return xc * lax.rsqrt(jnp.mean(xc * xc, axis=-1, keepdims=True) + EPS) * g + b


def _silu(x):
    return x * jax.nn.sigmoid(x)


SSD_PAIRS = SSD_HEADS // 2
SSD_PAIR_W = 2 * SSD_HEAD_DIM


def _ssd_kernel(xbc_ref, dt_ref, init_ref, cw_ref, cb_ref, dtb_ref, a_ref, skip_ref, y_ref, fin_ref,
                pad_ref, st_ref, *, rowlen, rev, dcol):
    s = pl.program_id(1)

    @pl.when(s == 0)
    def _():
        st_ref[...] = init_ref[...]

    xa = _silu(_dwconv_block(xbc_ref[...], pad_ref, cw_ref, rowlen) + cb_ref[...])
    dt_all = dt_ref[...] + dtb_ref[...]
    dt_all = jnp.maximum(dt_all, 0.0) + jnp.log1p(jnp.exp(-jnp.abs(dt_all)))
    da_all = dt_all * a_ref[...]

    row_i = lax.broadcasted_iota(jnp.int32, (SSD_CHUNK, SSD_CHUNK), 0)
    col_i = lax.broadcasted_iota(jnp.int32, (SSD_CHUNK, SSD_CHUNK), 1)
    tri = (col_i >= row_i) if rev else (col_i <= row_i)
    tri_f = tri.astype(F32)
    first_half = lax.broadcasted_iota(jnp.int32, (SSD_CHUNK, SSD_PAIR_W), 1) < SSD_HEAD_DIM
    first_half_row = first_half[0:1, :]

    chunks = range(MIX_TB // SSD_CHUNK)
    for ci in (reversed(chunks) if rev else chunks):
        rows = slice(ci * SSD_CHUNK, (ci + 1) * SSD_CHUNK)
        acc = jnp.dot(tri_f, da_all[rows, :], preferred_element_type=F32, precision=lax.Precision.HIGHEST)
        acc_t = acc.T
        tot = acc[0:1, :] if rev else acc[SSD_CHUNK - 1:SSD_CHUNK, :]
        to_end = jnp.exp(tot - acc)
        from_start = jnp.exp(acc)
        chunk_decay = jnp.exp(tot)
        dt_c = dt_all[rows, :]
        for g in range(SSD_GROUPS):
            bg = xa[rows, GROUP_W + g * SSD_STATE:GROUP_W + (g + 1) * SSD_STATE]
            cg = xa[rows, GROUP_W + (SSD_GROUPS + g) * SSD_STATE:GROUP_W + (SSD_GROUPS + g + 1) * SSD_STATE]
            scores = lax.dot_general(cg.astype(BF16), bg.astype(BF16), (((1,), (1,)), ((), ())),
                                     preferred_element_type=F32)
            pairs_per_group = SSD_PAIRS // SSD_GROUPS
            for p in range(g * pairs_per_group, (g + 1) * pairs_per_group):
                c0 = dcol + 2 * p
                c1 = c0 + 1
                lhs, bw = [], []
                for col in (c0, c1):
                    seg = acc[:, col:col + 1] - acc_t[col:col + 1, :]
                    lhs.append(scores * jnp.exp(jnp.where(tri, seg, NEG_INF)))
                for col in (c0, c1):
                    lhs.append(cg * from_start[:, col:col + 1])
                    bw.append((bg * to_end[:, col:col + 1]).T)
                xp = xa[rows, p * SSD_PAIR_W:(p + 1) * SSD_PAIR_W]
                xd = xp * jnp.where(first_half, dt_c[:, c0:c0 + 1], dt_c[:, c1:c1 + 1])
                xd_top = jnp.where(first_half, xd, 0.0).astype(BF16)
                xd_bot = jnp.where(first_half, 0.0, xd).astype(BF16)
                st = st_ref[p]
                st_top = jnp.where(first_half, st, 0.0).astype(BF16)
                st_bot = jnp.where(first_half, 0.0, st).astype(BF16)
                y = jnp.dot(jnp.concatenate(lhs, axis=1).astype(BF16),
                            jnp.concatenate([xd_top, xd_bot, st_top, st_bot], axis=0),
                            preferred_element_type=F32)
                y_ref[rows, p * SSD_PAIR_W:(p + 1) * SSD_PAIR_W] = y + skip_ref[:, p * SSD_PAIR_W:(p + 1) * SSD_PAIR_W] * xp
                upd = jnp.dot(jnp.concatenate(bw, axis=1).astype(BF16), jnp.concatenate([xd_top, xd_bot], axis=0),
                              preferred_element_type=F32)
                decay_lane = jnp.where(first_half_row, chunk_decay[:, c0:c0 + 1], chunk_decay[:, c1:c1 + 1])
                st_ref[p] = st * decay_lane + upd

    @pl.when(s == pl.num_programs(1) - 1)
    def _():
        fin_ref[...] = st_ref[...]


def _ssd_sweep(xbc_src, xbc_col_block, dt_src, row_block0, n_blocks, init, lw_ssd, *, rowlen, rev, direction):
    cw, cb, dtb_row, a_row, skip_row = lw_ssd[direction]

    def blk(b, s):
        return row_block0 + b * n_blocks + ((n_blocks - 1 - s) if rev else s)

    def out_blk(b, s):
        return b * n_blocks + ((n_blocks - 1 - s) if rev else s)

    state_shape = (BATCH, SSD_PAIRS, SSD_STATE, SSD_PAIR_W)
    return pl.pallas_call(
        functools.partial(_ssd_kernel, rowlen=rowlen, rev=rev, dcol=direction * SSD_HEADS),
        grid=(BATCH, n_blocks),
        in_specs=[
            pl.BlockSpec((MIX_TB, SSD_XBC), lambda b, s: (blk(b, s), xbc_col_block)),
            pl.BlockSpec((MIX_TB, LANE), lambda b, s: (blk(b, s), 0)),
            pl.BlockSpec((None, SSD_PAIRS, SSD_STATE, SSD_PAIR_W), lambda b, s: (b, 0, 0, 0)),
            pl.BlockSpec((3, SSD_XBC), lambda b, s: (0, 0)),
            pl.BlockSpec((1, SSD_XBC), lambda b, s: (0, 0)),
            pl.BlockSpec((1, LANE), lambda b, s: (0, 0)),
            pl.BlockSpec((1, LANE), lambda b, s: (0, 0)),
            pl.BlockSpec((1, GROUP_W), lambda b, s: (0, 0)),
        ],
        out_specs=[
            pl.BlockSpec((MIX_TB, GROUP_W), lambda b, s: (out_blk(b, s), 0)),
            pl.BlockSpec((None, SSD_PAIRS, SSD_STATE, SSD_PAIR_W), lambda b, s: (b, 0, 0, 0)),
        ],
        out_shape=[
            jax.ShapeDtypeStruct((BATCH * n_blocks * MIX_TB, GROUP_W), F32),
            jax.ShapeDtypeStruct(state_shape, F32),
        ],
        scratch_shapes=[
            pltpu.VMEM(((MIX_TB // rowlen) * _conv_pitch(rowlen), SSD_XBC), F32),
            pltpu.VMEM((SSD_PAIRS, SSD_STATE, SSD_PAIR_W), F32),
        ],
        compiler_params=pltpu.CompilerParams(
            dimension_semantics=("arbitrary", "arbitrary"), vmem_limit_bytes=VMEM_LIMIT),
        name="ssd_sweep",
    )(xbc_src, dt_src, init, cw, cb, dtb_row, a_row, skip_row)


def _ssd_params(conv_w, conv_b, dt_bias, a_log, d_skip):
    out = []
    pad = LANE - 2 * SSD_HEADS
    dtb_row = jnp.pad(dt_bias.reshape(-1), (0, pad)).reshape(1, LANE)
    a_row = jnp.pad(-jnp.exp(a_log.reshape(-1)), (0, pad)).reshape(1, LANE)
    for d in range(2):
        skip_row = jnp.repeat(d_skip[d], SSD_HEAD_DIM).reshape(1, GROUP_W)
        out.append((conv_w, conv_b.reshape(1, SSD_XBC), dtb_row, a_row, skip_row))
    return out


def _local_kernel(p_ref, z_ref, y0_ref, y1_ref, scw_ref, cfw_ref, cfb_ref, cfg_ref, cfbeta_ref, sgg_ref,
                  sgbeta_ref, sgw_ref, sgb_ref, ng_ref, m_ref, pad_ref, *, rowlen):
    gate_b = p_ref[:, 0:GROUP_W]
    u = p_ref[:, GROUP_W:2 * GROUP_W] * p_ref[:, 2 * GROUP_W:3 * GROUP_W]
    m_ref[:, 0:GROUP_W] = gate_b * _dwconv_block(u, pad_ref, scw_ref, rowlen)
    u = p_ref[:, OFF_CF:OFF_CF + GROUP_W] * jax.nn.sigmoid(p_ref[:, OFF_CF + GROUP_W:OFF_SG])
    u = _dwconv_block(u, pad_ref, cfw_ref, rowlen) + cfb_ref[...]
    m_ref[:, GROUP_W:2 * GROUP_W] = _silu(_ln_rows(u, cfg_ref[...], cfbeta_ref[...]))
    q = p_ref[:, OFF_SG:OFF_SSD]
    q = 0.5 * q * (1.0 + lax.erf(q * (2.0 ** -0.5)))
    v = _ln_rows(q[:, GROUP_W:], sgg_ref[...], sgbeta_ref[...]).astype(BF16)
    n_chunks = MIX_TB // SG_CHUNK
    for h in range(SG_HEADS):
        cols = slice(h * SG_HEAD_DIM, (h + 1) * SG_HEAD_DIM)
        rhs = jnp.concatenate([v[c * SG_CHUNK:(c + 1) * SG_CHUNK, cols] for c in range(n_chunks)], axis=1)
        sres = jnp.dot(sgw_ref[h].astype(BF16), rhs, preferred_element_type=F32) + sgb_ref[:, h:h + 1]
        for c in range(n_chunks):
            rows = slice(c * SG_CHUNK, (c + 1) * SG_CHUNK)
            m_ref[rows, 2 * GROUP_W + h * SG_HEAD_DIM:2 * GROUP_W + (h + 1) * SG_HEAD_DIM] = (
                q[rows, cols] * sres[:, c * SG_HEAD_DIM:(c + 1) * SG_HEAD_DIM])
    yv = (y0_ref[...] + y1_ref[...]) * _silu(z_ref[...])
    gw = GROUP_W // SSD_GROUPS
    for g in range(SSD_GROUPS):
        vg = yv[:, g * gw:(g + 1) * gw]
        vg = vg * lax.rsqrt(jnp.mean(vg * vg, axis=-1, keepdims=True) + EPS)
        m_ref[:, 3 * GROUP_W + g * gw:3 * GROUP_W + (g + 1) * gw] = vg * ng_ref[:, g * gw:(g + 1) * gw]


def _local_mixers_call(p_src, z_col_block, row_block0, n_blocks, y0, y1, lw, *, rowlen):
    vec = lambda a: a.reshape(1, GROUP_W)
    args = [lw['sc_conv_w'], lw['cf_conv_w'], vec(lw['cf_conv_b']), vec(lw['cf_ln_g']), vec(lw['cf_ln_b']),
            vec(lw['sg_ln_g']), vec(lw['sg_ln_b']), lw['sg_w'], lw['sg_b'].T, vec(lw['ssd_norm_g'])]
    full = lambda a: pl.BlockSpec(a.shape, lambda t, nd=a.ndim: (0,) * nd)
    return pl.pallas_call(
        functools.partial(_local_kernel, rowlen=rowlen),
        grid=(n_blocks,),
        in_specs=[
            pl.BlockSpec((MIX_TB, OFF_SSD), lambda t: (t + row_block0, 0)),
            pl.BlockSpec((MIX_TB, GROUP_W), lambda t: (t + row_block0, z_col_block)),
            pl.BlockSpec((MIX_TB, GROUP_W), lambda t: (t, 0)),
            pl.BlockSpec((MIX_TB, GROUP_W), lambda t: (t, 0)),
        ] + [full(a) for a in args],
        out_specs=pl.BlockSpec((MIX_TB, D_MODEL), lambda t: (t, 0)),
        out_shape=jax.ShapeDtypeStruct((n_blocks * MIX_TB, D_MODEL), F32),
        scratch_shapes=[pltpu.VMEM(((MIX_TB // rowlen) * _conv_pitch(rowlen), GROUP_W), F32)],
        compiler_params=pltpu.CompilerParams(dimension_semantics=("arbitrary",), vmem_limit_bytes=VMEM_LIMIT),
        name="local_mixers",
    )(p_src, p_src, y0, y1, *args)


def _token_mixers(p_ctx_src, ctx_cols, p_lat_src, lat_cols, dt_ctx, dt_lat, lat_row_block0, lw, ctx_out):
    prm = _ssd_params(lw['ssd_conv_w'], lw['ssd_conv_b'], lw['ssd_dt_bias'], lw['ssd_a_log'], lw['ssd_d'])
    zero_state = jnp.zeros((BATCH, SSD_PAIRS, SSD_STATE, SSD_PAIR_W), F32)
    ctx_blocks = CTX_LEN // MIX_TB
    lat_blocks = SEQ // MIX_TB
    y_ctx, y_lat = [], []
    for d in range(2):
        rev = d == 1
        yc, state = _ssd_sweep(p_ctx_src, ctx_cols[1], dt_ctx, 0, ctx_blocks, zero_state, prm,
                               rowlen=CTX_LEN, rev=rev, direction=d)
        yl, _ = _ssd_sweep(p_lat_src, lat_cols[1], dt_lat, lat_row_block0, lat_blocks, state, prm,
                           rowlen=GRID_W, rev=rev, direction=d)
        y_ctx.append(yc)
        y_lat.append(yl)
    m_lat = _local_mixers_call(p_lat_src, lat_cols[0], lat_row_block0, BATCH * lat_blocks, y_lat[0], y_lat[1], lw,
                               rowlen=GRID_W)
    m_ctx = None
    if ctx_out:
        m_ctx = _local_mixers_call(p_ctx_src, ctx_cols[0], 0, BATCH * ctx_blocks, y_ctx[0], y_ctx[1], lw,
                                   rowlen=CTX_LEN)
    return m_lat, m_ctx


def kernel(x, c, ctx, c_ctx, w_mod, b_mod, norm1_g, norm2_g, w_in, b_in, sc_conv_w, cf_conv_w, cf_conv_b,
           cf_ln_g, cf_ln_b, sg_ln_g, sg_ln_b, sg_w, sg_b, ssd_conv_w, ssd_conv_b, ssd_dt_bias, ssd_a_log,
           ssd_d, ssd_norm_g, w_out, b_out, w_router, b_router, w_gate_up, b_gate_up, w_down, b_down,
           final_norm_g):
    cc = jnp.concatenate([c_ctx[None, :], c, jnp.zeros((MOD_ROWS - 1 - BATCH, D_MODEL), F32)], axis=0)
    mod_all = _modulation(cc, w_mod, b_mod).reshape(DEPTH, MOD_ROWS, 6, D_MODEL)

    x_all = jnp.concatenate([ctx.reshape(N_CTX, D_MODEL), x.reshape(N_LAT, D_MODEL)], axis=0)

    for i in range(DEPTH):
        last = i == DEPTH - 1
        lw = dict(sc_conv_w=sc_conv_w[i], cf_conv_w=cf_conv_w[i], cf_conv_b=cf_conv_b[i], cf_ln_g=cf_ln_g[i],
                  cf_ln_b=cf_ln_b[i], sg_ln_g=sg_ln_g[i], sg_ln_b=sg_ln_b[i], sg_w=sg_w[i], sg_b=sg_b[i],
                  ssd_conv_w=ssd_conv_w[i], ssd_conv_b=ssd_conv_b[i], ssd_dt_bias=ssd_dt_bias[i],
                  ssd_a_log=ssd_a_log[i], ssd_d=ssd_d[i], ssd_norm_g=ssd_norm_g[i])
        mod = mod_all[i]
        w_in_bf = w_in[i, :, :MAIN_COLS].astype(BF16)
        b_in_main = b_in[i, :MAIN_COLS].reshape(1, MAIN_COLS)
        wdt_bf = jnp.pad(w_in[i, :, MAIN_COLS:], ((0, 0), (0, LANE - DT_COLS))).astype(BF16)
        bdt = jnp.pad(b_in[i, MAIN_COLS:], (0, LANE - DT_COLS)).reshape(1, LANE)
        w_out_bf = w_out[i].astype(BF16)
        wr_bf = jnp.pad(w_router[i], ((0, 0), (0, LANE - N_EXPERTS))).astype(BF16)
        br = jnp.pad(b_router[i], (0, LANE - N_EXPERTS)).reshape(1, LANE)
        in_ctx_tiles = N_CTX // IN_TM
        proj = functools.partial(_in_projection, x_all, norm1_g[i], mod, w_in_bf, b_in_main, wdt_bf, bdt,
                                 ctx_tiles=in_ctx_tiles)

        z_xbc_cols = (OFF_SSD // GROUP_W, (OFF_SSD + GROUP_W) // SSD_XBC)
        if not last:
            p_all, dt_all = proj(row_tile0=0, n_row_tiles=(N_CTX + N_LAT) // IN_TM,
                                 col_tile0=0, n_col_tiles=MAIN_COLS // IN_TN)
            m_lat, m_ctx = _token_mixers(p_all, z_xbc_cols, p_all, z_xbc_cols, dt_all, dt_all, N_CTX // MIX_TB, lw, True)
        else:
            p_lat, dt_lat = proj(row_tile0=in_ctx_tiles, n_row_tiles=N_LAT // IN_TM,
                                 col_tile0=0, n_col_tiles=MAIN_COLS // IN_TN)
            ctx_col0 = MAIN_COLS - 2 * SSD_XBC
            p_ctx, dt_ctx = proj(row_tile0=0, n_row_tiles=in_ctx_tiles,
                                 col_tile0=ctx_col0 // IN_TN, n_col_tiles=(MAIN_COLS - ctx_col0) // IN_TN)
            ctx_cols = ((OFF_SSD - ctx_col0) // GROUP_W, (OFF_SSD + GROUP_W - ctx_col0) // SSD_XBC)
            m_lat, m_ctx = _token_mixers(p_ctx, ctx_cols, p_lat, z_xbc_cols, dt_ctx, dt_lat, 0, lw, False)
        out_ctx_tiles = N_CTX // OUT_TM
        if not last:
            m_all = jnp.concatenate([m_ctx, m_lat], axis=0)
            row_tile0 = 0
        else:
            m_all = m_lat
            row_tile0 = out_ctx_tiles
        x_mid, f_packed, logits = _out_projection(m_all, x_all, mod, norm2_g[i], w_out_bf,
                                                  b_out[i].reshape(1, D_MODEL), wr_bf, br,
                                                  row_tile0=row_tile0, ctx_tiles=out_ctx_tiles)
        yb, dest, gates = _moe_ffn(i, f_packed, logits[:, :N_EXPERTS], w_gate_up, b_gate_up, w_down, b_down)
        x_all = _combine(yb, dest, gates, x_mid, mod, final_norm_g if last else None,
                         row_tile0=row_tile0 * OUT_TM // CMB_TM, ctx_tiles=N_CTX // CMB_TM)

    return x_all.reshape(BATCH, SEQ, D_MODEL)
```

```python
import functools

import jax
import jax.numpy as jnp
from jax import lax
from jax.experimental import pallas as pl
from jax.experimental.pallas import tpu as pltpu

F32 = jnp.float32
BF16 = jnp.bfloat16

D_MODEL = 2048
BATCH = 4
SEQ = 2048
DEPTH = 2
GRID_W = 64
CTX_LEN = 256
EPS = 1e-6
GROUP_W = 512
SG_HEADS = 4
SG_CHUNK = 128
SG_HEAD_DIM = 128
SSD_HEAD_DIM = 64
SSD_HEADS = 8
SSD_GROUPS = 2
SSD_STATE = 128
SSD_CHUNK = 128
SSD_XBC = 1024
N_EXPERTS = 32
TOP_K = 4
D_FF = 2048
SWIGLU_LIMIT = 7.0
SWIGLU_ALPHA = 1.702
OFF_CF = 1536
OFF_SG = 2560
OFF_SSD = 3584
MAIN_COLS = 5120
DT_COLS = 2 * SSD_HEADS
LANE = 128
HALF_D = D_MODEL // 2

N_CTX = BATCH * CTX_LEN
N_LAT = BATCH * SEQ

VMEM_LIMIT = 56 * 1024 * 1024

MOD_ROWS = 8
MOD_TN = 1024


def _mod_kernel(c_ref, w_ref, b_ref, o_ref):
    c = c_ref[...]
    s = c * jax.nn.sigmoid(c)
    o_ref[...] = jnp.dot(s.astype(BF16), w_ref[...].astype(BF16), preferred_element_type=F32) + b_ref[...]


def _modulation(cc, w_mod, b_mod):
    n_out = 6 * D_MODEL
    return pl.pallas_call(
        _mod_kernel,
        grid=(DEPTH, n_out // MOD_TN),
        in_specs=[
            pl.BlockSpec((MOD_ROWS, D_MODEL), lambda l, n: (0, 0)),
            pl.BlockSpec((None, D_MODEL, MOD_TN), lambda l, n: (l, 0, n)),
            pl.BlockSpec((None, 1, MOD_TN), lambda l, n: (l, 0, n)),
        ],
        out_specs=pl.BlockSpec((None, MOD_ROWS, MOD_TN), lambda l, n: (l, 0, n)),
        out_shape=jax.ShapeDtypeStruct((DEPTH, MOD_ROWS, n_out), F32),
        compiler_params=pltpu.CompilerParams(
            dimension_semantics=("arbitrary", "arbitrary"), vmem_limit_bytes=VMEM_LIMIT),
        name="adaln_mod",
    )(cc, w_mod, b_mod.reshape(DEPTH, 1, n_out))


def _mod_row(tile, tile_rows, ctx_tiles):
    tiles_per_batch = SEQ // tile_rows
    return jnp.where(tile < ctx_tiles, 0, 1 + (tile - ctx_tiles) // tiles_per_batch)


IN_TM = 1024
IN_TN = 1024
IN_PRO_ROWS = 256


def _inproj_kernel(x_ref, g_ref, mod_ref, w_ref, b_ref, wdt_ref, bdt_ref, o_ref, odt_ref, h_ref):
    @pl.when(pl.program_id(1) == 0)
    def _():
        g = g_ref[...]
        scale = 1.0 + mod_ref[1:2, :]
        shift = mod_ref[0:1, :]
        for r in range(IN_TM // IN_PRO_ROWS):
            rows = slice(r * IN_PRO_ROWS, (r + 1) * IN_PRO_ROWS)
            x = x_ref[rows, :]
            y = x * lax.rsqrt(jnp.mean(x * x, axis=-1, keepdims=True) + EPS)
            h_ref[rows, :] = ((y * g) * scale + shift).astype(BF16)
        odt_ref[...] = jnp.dot(h_ref[...], wdt_ref[...], preferred_element_type=F32) + bdt_ref[...]

    o_ref[...] = jnp.dot(h_ref[...], w_ref[...], preferred_element_type=F32) + b_ref[...]


def _in_projection(x_all, norm_g, mod, w_bf, b, wdt_bf, bdt, *, row_tile0, n_row_tiles, col_tile0, n_col_tiles,
                   ctx_tiles):
    rows = n_row_tiles * IN_TM
    return pl.pallas_call(
        _inproj_kernel,
        grid=(n_row_tiles, n_col_tiles),
        in_specs=[
            pl.BlockSpec((IN_TM, D_MODEL), lambda m, n: (m + row_tile0, 0)),
            pl.BlockSpec((1, D_MODEL), lambda m, n: (0, 0)),
            pl.BlockSpec((None, 6, D_MODEL), lambda m, n: (_mod_row(m + row_tile0, IN_TM, ctx_tiles), 0, 0)),
            pl.BlockSpec((D_MODEL, IN_TN), lambda m, n: (0, n + col_tile0)),
            pl.BlockSpec((1, IN_TN), lambda m, n: (0, n + col_tile0)),
            pl.BlockSpec((D_MODEL, LANE), lambda m, n: (0, 0)),
            pl.BlockSpec((1, LANE), lambda m, n: (0, 0)),
        ],
        out_specs=[
            pl.BlockSpec((IN_TM, IN_TN), lambda m, n: (m, n)),
            pl.BlockSpec((IN_TM, LANE), lambda m, n: (m, 0)),
        ],
        out_shape=[
            jax.ShapeDtypeStruct((rows, n_col_tiles * IN_TN), F32),
            jax.ShapeDtypeStruct((rows, LANE), F32),
        ],
        scratch_shapes=[pltpu.VMEM((IN_TM, D_MODEL), BF16)],
        compiler_params=pltpu.CompilerParams(
            dimension_semantics=("arbitrary", "arbitrary"), vmem_limit_bytes=VMEM_LIMIT),
        name="in_proj",
    )(x_all, norm_g.reshape(1, D_MODEL), mod, w_bf, b, wdt_bf, bdt)


OUT_TM = 256


def _outproj_kernel(m_ref, x_ref, mod_ref, g_ref, w_ref, b_ref, wr_ref, br_ref, xo_ref, f_ref, lg_ref):
    y = jnp.dot(m_ref[...].astype(BF16), w_ref[...], preferred_element_type=F32) + b_ref[...]
    xn = x_ref[...] + mod_ref[2:3, :] * y
    xo_ref[...] = xn
    r = lax.rsqrt(jnp.mean(xn * xn, axis=-1, keepdims=True) + EPS)
    f = ((xn * r) * g_ref[...]) * (1.0 + mod_ref[4:5, :]) + mod_ref[3:4, :]
    fb = f.astype(BF16)
    bits = lax.bitcast_convert_type(fb.astype(F32), jnp.uint32)
    f_ref[...] = (bits[:, HALF_D:] & jnp.uint32(0xFFFF0000)) | (bits[:, :HALF_D] >> 16)
    lg_ref[...] = jnp.dot(fb, wr_ref[...], preferred_element_type=F32) + br_ref[...]


def _out_projection(m, x_all, mod, norm_g, w_bf, b, wr_bf, br, *, row_tile0, ctx_tiles):
    rows = m.shape[0]
    n_tiles = rows // OUT_TM
    return pl.pallas_call(
        _outproj_kernel,
        grid=(n_tiles,),
        in_specs=[
            pl.BlockSpec((OUT_TM, D_MODEL), lambda t: (t, 0)),
            pl.BlockSpec((OUT_TM, D_MODEL), lambda t: (t + row_tile0, 0)),
            pl.BlockSpec((None, 6, D_MODEL), lambda t: (_mod_row(t + row_tile0, OUT_TM, ctx_tiles), 0, 0)),
            pl.BlockSpec((1, D_MODEL), lambda t: (0, 0)),
            pl.BlockSpec((D_MODEL, D_MODEL), lambda t: (0, 0)),
            pl.BlockSpec((1, D_MODEL), lambda t: (0, 0)),
            pl.BlockSpec((D_MODEL, LANE), lambda t: (0, 0)),
            pl.BlockSpec((1, LANE), lambda t: (0, 0)),
        ],
        out_specs=[
            pl.BlockSpec((OUT_TM, D_MODEL), lambda t: (t, 0)),
            pl.BlockSpec((OUT_TM, HALF_D), lambda t: (t, 0)),
            pl.BlockSpec((OUT_TM, LANE), lambda t: (t, 0)),
        ],
        out_shape=[
            jax.ShapeDtypeStruct((rows, D_MODEL), F32),
            jax.ShapeDtypeStruct((rows, HALF_D), jnp.uint32),
            jax.ShapeDtypeStruct((rows, LANE), F32),
        ],
        compiler_params=pltpu.CompilerParams(
            dimension_semantics=("arbitrary",), vmem_limit_bytes=VMEM_LIMIT),
        name="out_proj",
    )(m, x_all, mod, norm_g.reshape(1, D_MODEL), w_bf, b, wr_bf, br)


MOE_UNIT = 128
MOE_SUB = 2 * MOE_UNIT
MOE_CHUNK = 2048
MOE_NSUB = MOE_CHUNK // MOE_SUB
MOE_PIECES = (8, 4, 2, 1)
MOE_TF = 256
MOE_F_STEPS = D_FF // MOE_TF
MOE_N_STEPS = D_MODEL // MOE_TF
MOE_STEPS = MOE_F_STEPS + MOE_N_STEPS


def _moe_rows(n_tokens):
    slots = n_tokens * TOP_K
    padded = slots + N_EXPERTS * (MOE_UNIT - 1)
    padded = -(-padded // MOE_UNIT) * MOE_UNIT
    return padded


def _moe_chunks(n_tokens):
    return _moe_rows(n_tokens) // MOE_CHUNK + N_EXPERTS


def _row_copy_wait(src_hbm, dst, sem, rows):
    pltpu.make_async_copy(src_hbm.at[pl.ds(0, rows)], dst, sem).wait()


def _moe_kernel(ce_ref, cs_ref, cn_ref, *refs):
    (tok_ref, tok2_ref, f_hbm, wg_ref, wu_ref, bg_ref, bu_ref, wd_ref, bd_ref, yb_hbm, xw, xs, hs, os_, sem_in,
     sem_out) = refs
    c = pl.program_id(0)
    j = pl.program_id(1)
    cur = jnp.maximum(c - 1, 0)
    units = jnp.where(c >= 1, cn_ref[cur], 0)
    nsub = (units + 1) // 2
    start = pl.multiple_of(cs_ref[cur], MOE_UNIT)

    def gather_issue():
        base = pl.multiple_of(j * MOE_SUB, MOE_SUB)
        for half, toks in enumerate((tok_ref, tok2_ref)):
            for r in range(MOE_UNIT):
                pltpu.make_async_copy(f_hbm.at[pl.ds(toks[0, r], 1)], xw.at[pl.ds(base + half * MOE_UNIT + r, 1)],
                                      sem_in.at[j]).start()

    def gather_wait(i):
        _row_copy_wait(f_hbm, xw.at[pl.ds(i * MOE_SUB, MOE_SUB)], sem_in.at[i], MOE_SUB)

    def unpack(i):
        rows = slice(i * MOE_SUB, (i + 1) * MOE_SUB)
        w = xw[rows, :]
        xs[rows, :HALF_D] = lax.bitcast_convert_type(w << 16, F32).astype(BF16)
        xs[rows, HALF_D:] = lax.bitcast_convert_type(w & jnp.uint32(0xFFFF0000), F32).astype(BF16)

    def copy_out(slot, off, n, col):
        return pltpu.make_async_copy(
            os_.at[slot, pl.ds(off, n * MOE_UNIT), :],
            yb_hbm.at[pl.ds(start + off, n * MOE_UNIT), pl.ds(col, MOE_TF)],
            sem_out.at[slot])

    def for_valid_subs(fn):
        fn(0)
        for i in range(1, MOE_NSUB):
            @pl.when(i < nsub)
            def _(i=i):
                fn(i)

    big = MOE_PIECES[0]

    def for_pieces(fn):
        n_big = units // big
        rem = units - big * n_big
        for q in range(MOE_CHUNK // (big * MOE_UNIT)):
            @pl.when(q < n_big)
            def _(q=q):
                fn(q * big * MOE_UNIT, big, q == 0)
        base = n_big * (big * MOE_UNIT)
        for size in MOE_PIECES[1:]:
            above = (rem // (2 * size)) * (2 * size)

            @pl.when((rem // size) % 2 == 1)
            def _(size=size, above=above):
                fn(pl.multiple_of(base + above * MOE_UNIT, MOE_UNIT), size, False)

    @pl.when(jnp.logical_and(nsub > 0, j == 0))
    def _():
        for i in range(MOE_NSUB):
            gather_wait(i)
        for_valid_subs(unpack)

    @pl.when(jnp.logical_and(j < MOE_F_STEPS, units < big))
    def _():
        gather_issue()

    @pl.when(nsub > 0)
    def _():
        @pl.when(j < MOE_F_STEPS)
        def _():
            bg = bg_ref[...]
            bu = bu_ref[...]

            def gate_up(off, n, first_big):
                if first_big:
                    gather_issue()
                rows = pl.ds(off, n * MOE_UNIT)
                x = xs[rows, :]
                g = jnp.dot(x, wg_ref[...].astype(BF16), preferred_element_type=F32) + bg
                u = jnp.dot(x, wu_ref[...].astype(BF16), preferred_element_type=F32) + bu
                g = jnp.minimum(g, SWIGLU_LIMIT)
                u = jnp.clip(u, -SWIGLU_LIMIT, SWIGLU_LIMIT)
                h = (u + 1.0) * (g * jax.nn.sigmoid(SWIGLU_ALPHA * g))
                hs[j, rows, :] = h.astype(BF16)

            for_pieces(gate_up)

        @pl.when(j >= MOE_F_STEPS)
        def _():
            jn = j - MOE_F_STEPS
            slot = jn % 2
            col = pl.multiple_of(jn * MOE_TF, MOE_TF)
            bd = bd_ref[...]

            @pl.when(jn >= 2)
            def _():
                for_pieces(lambda off, n, _: copy_out(slot, off, n, col).wait())

            def down(off, n, _):
                rows = pl.ds(off, n * MOE_UNIT)
                h = jnp.concatenate([hs[f, rows, :] for f in range(MOE_F_STEPS)], axis=1)
                os_[slot, rows, :] = jnp.dot(h, wd_ref[...].astype(BF16), preferred_element_type=F32) + bd
                copy_out(slot, off, n, col).start()

            for_pieces(down)

            @pl.when(jn == MOE_N_STEPS - 1)
            def _():
                for_pieces(lambda off, n, _: copy_out(1 - slot, off, n, col).wait())
                for_pieces(lambda off, n, _: copy_out(slot, off, n, col).wait())

    @pl.when(jnp.logical_and(c == pl.num_programs(0) - 1, j == MOE_STEPS - 1))
    def _():
        for i in range(MOE_NSUB):
            gather_wait(i)


def _moe_experts(layer, chunk_e, chunk_start, chunk_nsub, used_chunks, buf_tok, f_packed, w_gate_up, b_gate_up,
                 w_down, b_down):
    n_chunks = chunk_e.shape[0]
    rows = buf_tok.shape[0]
    n_units = rows // MOE_UNIT

    def gu_idx(half):
        def idx(c, j, ce, cs, cn):
            cur = jnp.maximum(c - 1, 0)
            jj = jnp.where(cn[cur] > 0, jnp.minimum(j, MOE_F_STEPS - 1), MOE_F_STEPS - 1)
            jj = jnp.where(c == 0, 0, jj)
            return (layer, ce[cur], 0, half * MOE_F_STEPS + jj)
        return idx

    def d_idx(c, j, ce, cs, cn):
        cur = jnp.maximum(c - 1, 0)
        jj = jnp.where(cn[cur] > 0, jnp.maximum(j - MOE_F_STEPS, 0), MOE_N_STEPS - 1)
        jj = jnp.where(c == 0, 0, jj)
        return (layer, ce[cur], 0, jj)

    def tok_idx(half):
        def idx(c, j, ce, cs, cn):
            nxt = jnp.minimum(c, n_chunks - 1)
            unit = cs[nxt] // MOE_UNIT + 2 * jnp.minimum(j, MOE_F_STEPS - 1) + half
            return (jnp.minimum(unit, n_units - 1), 0, 0)
        return idx

    grid_spec = pltpu.PrefetchScalarGridSpec(
        num_scalar_prefetch=3,
        grid=(used_chunks + 1, MOE_STEPS),
        in_specs=[
            pl.BlockSpec((None, 1, MOE_UNIT), tok_idx(0), memory_space=pltpu.SMEM),
            pl.BlockSpec((None, 1, MOE_UNIT), tok_idx(1), memory_space=pltpu.SMEM),
            pl.BlockSpec(memory_space=pl.ANY),
            pl.BlockSpec((None, None, D_MODEL, MOE_TF), gu_idx(0)),
            pl.BlockSpec((None, None, D_MODEL, MOE_TF), gu_idx(1)),
            pl.BlockSpec((None, None, 1, MOE_TF), gu_idx(0)),
            pl.BlockSpec((None, None, 1, MOE_TF), gu_idx(1)),
            pl.BlockSpec((None, None, D_FF, MOE_TF), d_idx),
            pl.BlockSpec((None, None, 1, MOE_TF), d_idx),
        ],
        out_specs=pl.BlockSpec(memory_space=pl.ANY),
        scratch_shapes=[
            pltpu.VMEM((MOE_CHUNK, HALF_D), jnp.uint32),
            pltpu.VMEM((MOE_CHUNK, D_MODEL), BF16),
            pltpu.VMEM((MOE_F_STEPS, MOE_CHUNK, MOE_TF), BF16),
            pltpu.VMEM((2, MOE_CHUNK, MOE_TF), F32),
            pltpu.SemaphoreType.DMA((MOE_NSUB,)),
            pltpu.SemaphoreType.DMA((2,)),
        ],
    )
    bgu = b_gate_up.reshape(DEPTH, N_EXPERTS, 1, 2 * D_FF)
    bd = b_down.reshape(DEPTH, N_EXPERTS, 1, D_MODEL)
    return pl.pallas_call(
        _moe_kernel,
        grid_spec=grid_spec,
        out_shape=jax.ShapeDtypeStruct((rows, D_MODEL), F32),
        compiler_params=pltpu.CompilerParams(
            dimension_semantics=("arbitrary", "arbitrary"), vmem_limit_bytes=VMEM_LIMIT),
        name="moe_experts",
    )(chunk_e, chunk_start, chunk_nsub, *([buf_tok.reshape(n_units, 1, MOE_UNIT)] * 2), f_packed,
      w_gate_up, w_gate_up, bgu, bgu, w_down, bd)


CMB_TM = 256


def _combine_kernel(idx_ref, yb_hbm, gates_ref, x_ref, mod_ref, *rest, final):
    if final:
        g_ref, o_ref, buf, sem = rest
    else:
        o_ref, buf, sem = rest

    for k in range(TOP_K):
        def issue(r, carry, k=k):
            pltpu.make_async_copy(yb_hbm.at[pl.ds(idx_ref[0, k * CMB_TM + r], 1)], buf.at[k, pl.ds(r, 1)],
                                  sem.at[k]).start()
            return carry

        lax.fori_loop(0, CMB_TM, issue, 0, unroll=8)

    gates = gates_ref[...]
    y = None
    for k in range(TOP_K):
        _row_copy_wait(yb_hbm, buf.at[k], sem.at[k], CMB_TM)
        t = gates[:, k:k + 1] * buf[k]
        y = t if y is None else y + t
    xn = x_ref[...] + mod_ref[5:6, :] * y
    if final:
        xn = xn * lax.rsqrt(jnp.mean(xn * xn, axis=-1, keepdims=True) + EPS) * g_ref[...]
    o_ref[...] = xn


def _combine(yb, dest, gates, x_mid, mod, final_g, *, row_tile0, ctx_tiles):
    n = x_mid.shape[0]
    n_tiles = n // CMB_TM
    final = final_g is not None
    idx = dest.reshape(n_tiles, CMB_TM, TOP_K).transpose(0, 2, 1).reshape(n_tiles, 1, TOP_K * CMB_TM)
    in_specs = [
        pl.BlockSpec((None, 1, TOP_K * CMB_TM), lambda t: (t, 0, 0), memory_space=pltpu.SMEM),
        pl.BlockSpec(memory_space=pl.ANY),
        pl.BlockSpec((CMB_TM, TOP_K), lambda t: (t, 0)),
        pl.BlockSpec((CMB_TM, D_MODEL), lambda t: (t, 0)),
        pl.BlockSpec((None, 6, D_MODEL), lambda t: (_mod_row(t + row_tile0, CMB_TM, ctx_tiles), 0, 0)),
    ]
    args = [idx, yb, gates, x_mid, mod]
    if final:
        in_specs.append(pl.BlockSpec((1, D_MODEL), lambda t: (0, 0)))
        args.append(final_g.reshape(1, D_MODEL))
    return pl.pallas_call(
        functools.partial(_combine_kernel, final=final),
        grid=(n_tiles,),
        in_specs=in_specs,
        out_specs=pl.BlockSpec((CMB_TM, D_MODEL), lambda t: (t, 0)),
        out_shape=jax.ShapeDtypeStruct((n, D_MODEL), F32),
        scratch_shapes=[pltpu.VMEM((TOP_K, CMB_TM, D_MODEL), F32), pltpu.SemaphoreType.DMA((TOP_K,))],
        compiler_params=pltpu.CompilerParams(dimension_semantics=("arbitrary",), vmem_limit_bytes=VMEM_LIMIT),
        name="moe_combine",
    )(*args)


def _moe_ffn(layer, f_packed, logits, w_gate_up, b_gate_up, w_down, b_down):
    n = f_packed.shape[0]
    slots = n * TOP_K
    buf_rows = _moe_rows(n)
    n_chunks = _moe_chunks(n)

    top_logit, top_e = lax.top_k(logits, TOP_K)
    gates = jax.nn.softmax(top_logit, axis=-1)
    flat_e = top_e.reshape(-1).astype(jnp.int32)
    onehot = (flat_e[:, None] == jnp.arange(N_EXPERTS, dtype=jnp.int32)[None, :]).astype(jnp.int32)
    running = jnp.cumsum(onehot, axis=0)
    counts = running[-1]
    padded = (counts + MOE_UNIT - 1) // MOE_UNIT * MOE_UNIT
    pad_end = jnp.cumsum(padded)
    pad_start = pad_end - padded
    dest_of_slot = jnp.sum(onehot * (running - 1 + pad_start[None, :]), axis=1)
    buf_tok = jnp.zeros((buf_rows,), jnp.int32).at[dest_of_slot].set(jnp.arange(slots, dtype=jnp.int32) // TOP_K)

    e_chunks = (padded + MOE_CHUNK - 1) // MOE_CHUNK
    chunk_end = jnp.cumsum(e_chunks)
    total_chunks = chunk_end[-1]
    cidx = jnp.arange(n_chunks, dtype=jnp.int32)
    ce = jnp.minimum(jnp.searchsorted(chunk_end, cidx, side='right'), N_EXPERTS - 1).astype(jnp.int32)
    local = cidx - (chunk_end[ce] - e_chunks[ce])
    valid = cidx < total_chunks
    c_start = jnp.where(valid, pad_start[ce] + local * MOE_CHUNK, 0).astype(jnp.int32)
    c_nsub = jnp.where(valid, jnp.minimum(MOE_CHUNK, padded[ce] - local * MOE_CHUNK) // MOE_UNIT, 0).astype(jnp.int32)
    last_e = ce[jnp.maximum(total_chunks - 1, 0)]
    ce = jnp.where(valid, ce, last_e).astype(jnp.int32)

    yb = _moe_experts(layer, ce, c_start, c_nsub, total_chunks.astype(jnp.int32), buf_tok, f_packed,
                      w_gate_up, b_gate_up, w_down, b_down)
    return yb, dest_of_slot.reshape(n, TOP_K), gates


MIX_TB = 256
CONV_HALO = 16
CONV_PIECE = 64
NEG_INF = float("-inf")


def _conv_pitch(rowlen):
    return rowlen + 2 * CONV_HALO


def _dwconv_block(u, pad_ref, w_ref, rowlen):
    taps = w_ref.shape[0]
    half = taps // 2
    pitch = _conv_pitch(rowlen)
    ch = u.shape[1]
    zeros = jnp.zeros((CONV_HALO, ch), F32)
    for r in range(MIX_TB // rowlen):
        base = r * pitch
        pad_ref[base:base + CONV_HALO, :] = zeros
        pad_ref[base + CONV_HALO:base + CONV_HALO + rowlen, :] = u[r * rowlen:(r + 1) * rowlen, :]
        pad_ref[base + CONV_HALO + rowlen:base + pitch, :] = zeros
    outs = []
    for r in range(MIX_TB // rowlen):
        for piece in range(rowlen // CONV_PIECE):
            acc = None
            for j in range(taps):
                off = r * pitch + CONV_HALO + piece * CONV_PIECE + j - half
                term = pad_ref[off:off + CONV_PIECE, :] * w_ref[j:j + 1, :]
                acc = term if acc is None else acc + term
            outs.append(acc)
    return jnp.concatenate(outs, axis=0)


def _ln_rows(x, g, b):
    xc = x - jnp.mean(x, axis=-1, keepdims=True)
    return xc * lax.rsqrt(jnp.mean(xc * xc, axis=-1, keepdims=True) + EPS) * g + b


def _silu(x):
    return x * jax.nn.sigmoid(x)


SSD_PAIRS = SSD_HEADS // 2
SSD_PAIR_W = 2 * SSD_HEAD_DIM


def _ssd_kernel(xbc_ref, dt_ref, init_ref, cw_ref, cb_ref, dtb_ref, a_ref, skip_ref, y_ref, fin_ref,
                pad_ref, st_ref, *, rowlen, rev, dcol):
    s = pl.program_id(1)

    @pl.when(s == 0)
    def _():
        st_ref[...] = init_ref[...]

    xa = _silu(_dwconv_block(xbc_ref[...], pad_ref, cw_ref, rowlen) + cb_ref[...])
    dt_all = dt_ref[...] + dtb_ref[...]
    dt_all = jnp.maximum(dt_all, 0.0) + jnp.log1p(jnp.exp(-jnp.abs(dt_all)))
    da_all = dt_all * a_ref[...]

    row_i = lax.broadcasted_iota(jnp.int32, (SSD_CHUNK, SSD_CHUNK), 0)
    col_i = lax.broadcasted_iota(jnp.int32, (SSD_CHUNK, SSD_CHUNK), 1)
    tri = (col_i >= row_i) if rev else (col_i <= row_i)
    tri_f = tri.astype(F32)
    first_half = lax.broadcasted_iota(jnp.int32, (SSD_CHUNK, SSD_PAIR_W), 1) < SSD_HEAD_DIM
    first_half_row = first_half[0:1, :]

    chunks = range(MIX_TB // SSD_CHUNK)
    for ci in (reversed(chunks) if rev else chunks):
        rows = slice(ci * SSD_CHUNK, (ci + 1) * SSD_CHUNK)
        acc = jnp.dot(tri_f, da_all[rows, :], preferred_element_type=F32, precision=lax.Precision.HIGHEST)
        acc_t = acc.T
        tot = acc[0:1, :] if rev else acc[SSD_CHUNK - 1:SSD_CHUNK, :]
        to_end = jnp.exp(tot - acc)
        from_start = jnp.exp(acc)
        chunk_decay = jnp.exp(tot)
        dt_c = dt_all[rows, :]
        for g in range(SSD_GROUPS):
            bg = xa[rows, GROUP_W + g * SSD_STATE:GROUP_W + (g + 1) * SSD_STATE]
            cg = xa[rows, GROUP_W + (SSD_GROUPS + g) * SSD_STATE:GROUP_W + (SSD_GROUPS + g + 1) * SSD_STATE]
            scores = lax.dot_general(cg.astype(BF16), bg.astype(BF16), (((1,), (1,)), ((), ())),
                                     preferred_element_type=F32)
            pairs_per_group = SSD_PAIRS // SSD_GROUPS
            for p in range(g * pairs_per_group, (g + 1) * pairs_per_group):
                c0 = dcol + 2 * p
                c1 = c0 + 1
                lhs, bw = [], []
                for col in (c0, c1):
                    seg = acc[:, col:col + 1] - acc_t[col:col + 1, :]
                    lhs.append(scores * jnp.exp(jnp.where(tri, seg, NEG_INF)))
                for col in (c0, c1):
                    lhs.append(cg * from_start[:, col:col + 1])
                    bw.append((bg * to_end[:, col:col + 1]).T)
                xp = xa[rows, p * SSD_PAIR_W:(p + 1) * SSD_PAIR_W]
                xd = xp * jnp.where(first_half, dt_c[:, c0:c0 + 1], dt_c[:, c1:c1 + 1])
                xd_top = jnp.where(first_half, xd, 0.0).astype(BF16)
                xd_bot = jnp.where(first_half, 0.0, xd).astype(BF16)
                st = st_ref[p]
                st_top = jnp.where(first_half, st, 0.0).astype(BF16)
                st_bot = jnp.where(first_half, 0.0, st).astype(BF16)
                y = jnp.dot(jnp.concatenate(lhs, axis=1).astype(BF16),
                            jnp.concatenate([xd_top, xd_bot, st_top, st_bot], axis=0),
                            preferred_element_type=F32)
                y_ref[rows, p * SSD_PAIR_W:(p + 1) * SSD_PAIR_W] = y + skip_ref[:, p * SSD_PAIR_W:(p + 1) * SSD_PAIR_W] * xp
                upd = jnp.dot(jnp.concatenate(bw, axis=1).astype(BF16), jnp.concatenate([xd_top, xd_bot], axis=0),
                              preferred_element_type=F32)
                decay_lane = jnp.where(first_half_row, chunk_decay[:, c0:c0 + 1], chunk_decay[:, c1:c1 + 1])
                st_ref[p] = st * decay_lane + upd

    @pl.when(s == pl.num_programs(1) - 1)
    def _():
        fin_ref[...] = st_ref[...]


def _ssd_sweep(xbc_src, xbc_col_block, dt_src, row_block0, n_blocks, init, lw_ssd, *, rowlen, rev, direction):
    cw, cb, dtb_row, a_row, skip_row = lw_ssd[direction]

    def blk(b, s):
        return row_block0 + b * n_blocks + ((n_blocks - 1 - s) if rev else s)

    def out_blk(b, s):
        return b * n_blocks + ((n_blocks - 1 - s) if rev else s)

    state_shape = (BATCH, SSD_PAIRS, SSD_STATE, SSD_PAIR_W)
    return pl.pallas_call(
        functools.partial(_ssd_kernel, rowlen=rowlen, rev=rev, dcol=direction * SSD_HEADS),
        grid=(BATCH, n_blocks),
        in_specs=[
            pl.BlockSpec((MIX_TB, SSD_XBC), lambda b, s: (blk(b, s), xbc_col_block)),
            pl.BlockSpec((MIX_TB, LANE), lambda b, s: (blk(b, s), 0)),
            pl.BlockSpec((None, SSD_PAIRS, SSD_STATE, SSD_PAIR_W), lambda b, s: (b, 0, 0, 0)),
            pl.BlockSpec((3, SSD_XBC), lambda b, s: (0, 0)),
            pl.BlockSpec((1, SSD_XBC), lambda b, s: (0, 0)),
            pl.BlockSpec((1, LANE), lambda b, s: (0, 0)),
            pl.BlockSpec((1, LANE), lambda b, s: (0, 0)),
            pl.BlockSpec((1, GROUP_W), lambda b, s: (0, 0)),
        ],
        out_specs=[
            pl.BlockSpec((MIX_TB, GROUP_W), lambda b, s: (out_blk(b, s), 0)),
            pl.BlockSpec((None, SSD_PAIRS, SSD_STATE, SSD_PAIR_W), lambda b, s: (b, 0, 0, 0)),
        ],
        out_shape=[
            jax.ShapeDtypeStruct((BATCH * n_blocks * MIX_TB, GROUP_W), F32),
            jax.ShapeDtypeStruct(state_shape, F32),
        ],
        scratch_shapes=[
            pltpu.VMEM(((MIX_TB // rowlen) * _conv_pitch(rowlen), SSD_XBC), F32),
            pltpu.VMEM((SSD_PAIRS, SSD_STATE, SSD_PAIR_W), F32),
        ],
        compiler_params=pltpu.CompilerParams(
            dimension_semantics=("arbitrary", "arbitrary"), vmem_limit_bytes=VMEM_LIMIT),
        name="ssd_sweep",
    )(xbc_src, dt_src, init, cw, cb, dtb_row, a_row, skip_row)


def _ssd_params(conv_w, conv_b, dt_bias, a_log, d_skip):
    out = []
    pad = LANE - 2 * SSD_HEADS
    dtb_row = jnp.pad(dt_bias.reshape(-1), (0, pad)).reshape(1, LANE)
    a_row = jnp.pad(-jnp.exp(a_log.reshape(-1)), (0, pad)).reshape(1, LANE)
    for d in range(2):
        skip_row = jnp.repeat(d_skip[d], SSD_HEAD_DIM).reshape(1, GROUP_W)
        out.append((conv_w, conv_b.reshape(1, SSD_XBC), dtb_row, a_row, skip_row))
    return out


def _local_kernel(p_ref, z_ref, y0_ref, y1_ref, scw_ref, cfw_ref, cfb_ref, cfg_ref, cfbeta_ref, sgg_ref,
                  sgbeta_ref, sgw_ref, sgb_ref, ng_ref, m_ref, pad_ref, *, rowlen):
    gate_b = p_ref[:, 0:GROUP_W]
    u = p_ref[:, GROUP_W:2 * GROUP_W] * p_ref[:, 2 * GROUP_W:3 * GROUP_W]
    m_ref[:, 0:GROUP_W] = gate_b * _dwconv_block(u, pad_ref, scw_ref, rowlen)
    u = p_ref[:, OFF_CF:OFF_CF + GROUP_W] * jax.nn.sigmoid(p_ref[:, OFF_CF + GROUP_W:OFF_SG])
    u = _dwconv_block(u, pad_ref, cfw_ref, rowlen) + cfb_ref[...]
    m_ref[:, GROUP_W:2 * GROUP_W] = _silu(_ln_rows(u, cfg_ref[...], cfbeta_ref[...]))
    q = p_ref[:, OFF_SG:OFF_SSD]
    q = 0.5 * q * (1.0 + lax.erf(q * (2.0 ** -0.5)))
    v = _ln_rows(q[:, GROUP_W:], sgg_ref[...], sgbeta_ref[...]).astype(BF16)
    n_chunks = MIX_TB // SG_CHUNK
    for h in range(SG_HEADS):
        cols = slice(h * SG_HEAD_DIM, (h + 1) * SG_HEAD_DIM)
        rhs = jnp.concatenate([v[c * SG_CHUNK:(c + 1) * SG_CHUNK, cols] for c in range(n_chunks)], axis=1)
        sres = jnp.dot(sgw_ref[h].astype(BF16), rhs, preferred_element_type=F32) + sgb_ref[:, h:h + 1]
        for c in range(n_chunks):
            rows = slice(c * SG_CHUNK, (c + 1) * SG_CHUNK)
            m_ref[rows, 2 * GROUP_W + h * SG_HEAD_DIM:2 * GROUP_W + (h + 1) * SG_HEAD_DIM] = (
                q[rows, cols] * sres[:, c * SG_HEAD_DIM:(c + 1) * SG_HEAD_DIM])
    yv = (y0_ref[...] + y1_ref[...]) * _silu(z_ref[...])
    gw = GROUP_W // SSD_GROUPS
    for g in range(SSD_GROUPS):
        vg = yv[:, g * gw:(g + 1) * gw]
        vg = vg * lax.rsqrt(jnp.mean(vg * vg, axis=-1, keepdims=True) + EPS)
        m_ref[:, 3 * GROUP_W + g * gw:3 * GROUP_W + (g + 1) * gw] = vg * ng_ref[:, g * gw:(g + 1) * gw]


def _local_mixers_call(p_src, z_col_block, row_block0, n_blocks, y0, y1, lw, *, rowlen):
    vec = lambda a: a.reshape(1, GROUP_W)
    args = [lw['sc_conv_w'], lw['cf_conv_w'], vec(lw['cf_conv_b']), vec(lw['cf_ln_g']), vec(lw['cf_ln_b']),
            vec(lw['sg_ln_g']), vec(lw['sg_ln_b']), lw['sg_w'], lw['sg_b'].T, vec(lw['ssd_norm_g'])]
    full = lambda a: pl.BlockSpec(a.shape, lambda t, nd=a.ndim: (0,) * nd)
    return pl.pallas_call(
        functools.partial(_local_kernel, rowlen=rowlen),
        grid=(n_blocks,),
        in_specs=[
            pl.BlockSpec((MIX_TB, OFF_SSD), lambda t: (t + row_block0, 0)),
            pl.BlockSpec((MIX_TB, GROUP_W), lambda t: (t + row_block0, z_col_block)),
            pl.BlockSpec((MIX_TB, GROUP_W), lambda t: (t, 0)),
            pl.BlockSpec((MIX_TB, GROUP_W), lambda t: (t, 0)),
        ] + [full(a) for a in args],
        out_specs=pl.BlockSpec((MIX_TB, D_MODEL), lambda t: (t, 0)),
        out_shape=jax.ShapeDtypeStruct((n_blocks * MIX_TB, D_MODEL), F32),
        scratch_shapes=[pltpu.VMEM(((MIX_TB // rowlen) * _conv_pitch(rowlen), GROUP_W), F32)],
        compiler_params=pltpu.CompilerParams(dimension_semantics=("arbitrary",), vmem_limit_bytes=VMEM_LIMIT),
        name="local_mixers",
    )(p_src, p_src, y0, y1, *args)


def _token_mixers(p_ctx_src, ctx_cols, p_lat_src, lat_cols, dt_ctx, dt_lat, lat_row_block0, lw, ctx_out):
    prm = _ssd_params(lw['ssd_conv_w'], lw['ssd_conv_b'], lw['ssd_dt_bias'], lw['ssd_a_log'], lw['ssd_d'])
    zero_state = jnp.zeros((BATCH, SSD_PAIRS, SSD_STATE, SSD_PAIR_W), F32)
    ctx_blocks = CTX_LEN // MIX_TB
    lat_blocks = SEQ // MIX_TB
    y_ctx, y_lat = [], []
    for d in range(2):
        rev = d == 1
        yc, state = _ssd_sweep(p_ctx_src, ctx_cols[1], dt_ctx, 0, ctx_blocks, zero_state, prm,
                               rowlen=CTX_LEN, rev=rev, direction=d)
        yl, _ = _ssd_sweep(p_lat_src, lat_cols[1], dt_lat, lat_row_block0, lat_blocks, state, prm,
                           rowlen=GRID_W, rev=rev, direction=d)
        y_ctx.append(yc)
        y_lat.append(yl)
    m_lat = _local_mixers_call(p_lat_src, lat_cols[0], lat_row_block0, BATCH * lat_blocks, y_lat[0], y_lat[1], lw,
                               rowlen=GRID_W)
    m_ctx = None
    if ctx_out:
        m_ctx = _local_mixers_call(p_ctx_src, ctx_cols[0], 0, BATCH * ctx_blocks, y_ctx[0], y_ctx[1], lw,
                                   rowlen=CTX_LEN)
    return m_lat, m_ctx


def kernel(x, c, ctx, c_ctx, w_mod, b_mod, norm1_g, norm2_g, w_in, b_in, sc_conv_w, cf_conv_w, cf_conv_b,
           cf_ln_g, cf_ln_b, sg_ln_g, sg_ln_b, sg_w, sg_b, ssd_conv_w, ssd_conv_b, ssd_dt_bias, ssd_a_log,
           ssd_d, ssd_norm_g, w_out, b_out, w_router, b_router, w_gate_up, b_gate_up, w_down, b_down,
           final_norm_g):
    cc = jnp.concatenate([c_ctx[None, :], c, jnp.zeros((MOD_ROWS - 1 - BATCH, D_MODEL), F32)], axis=0)
    mod_all = _modulation(cc, w_mod, b_mod).reshape(DEPTH, MOD_ROWS, 6, D_MODEL)

    x_all = jnp.concatenate([ctx.reshape(N_CTX, D_MODEL), x.reshape(N_LAT, D_MODEL)], axis=0)

    for i in range(DEPTH):
        last = i == DEPTH - 1
        lw = dict(sc_conv_w=sc_conv_w[i], cf_conv_w=cf_conv_w[i], cf_conv_b=cf_conv_b[i], cf_ln_g=cf_ln_g[i],
                  cf_ln_b=cf_ln_b[i], sg_ln_g=sg_ln_g[i], sg_ln_b=sg_ln_b[i], sg_w=sg_w[i], sg_b=sg_b[i],
                  ssd_conv_w=ssd_conv_w[i], ssd_conv_b=ssd_conv_b[i], ssd_dt_bias=ssd_dt_bias[i],
                  ssd_a_log=ssd_a_log[i], ssd_d=ssd_d[i], ssd_norm_g=ssd_norm_g[i])
        mod = mod_all[i]
        w_in_bf = w_in[i, :, :MAIN_COLS].astype(BF16)
        b_in_main = b_in[i, :MAIN_COLS].reshape(1, MAIN_COLS)
        wdt_bf = jnp.pad(w_in[i, :, MAIN_COLS:], ((0, 0), (0, LANE - DT_COLS))).astype(BF16)
        bdt = jnp.pad(b_in[i, MAIN_COLS:], (0, LANE - DT_COLS)).reshape(1, LANE)
        w_out_bf = w_out[i].astype(BF16)
        wr_bf = jnp.pad(w_router[i], ((0, 0), (0, LANE - N_EXPERTS))).astype(BF16)
        br = jnp.pad(b_router[i], (0, LANE - N_EXPERTS)).reshape(1, LANE)
        in_ctx_tiles = N_CTX // IN_TM
        proj = functools.partial(_in_projection, x_all, norm1_g[i], mod, w_in_bf, b_in_main, wdt_bf, bdt,
                                 ctx_tiles=in_ctx_tiles)

        z_xbc_cols = (OFF_SSD // GROUP_W, (OFF_SSD + GROUP_W) // SSD_XBC)
        if not last:
            p_all, dt_all = proj(row_tile0=0, n_row_tiles=(N_CTX + N_LAT) // IN_TM,
                                 col_tile0=0, n_col_tiles=MAIN_COLS // IN_TN)
            m_lat, m_ctx = _token_mixers(p_all, z_xbc_cols, p_all, z_xbc_cols, dt_all, dt_all, N_CTX // MIX_TB, lw, True)
        else:
            p_lat, dt_lat = proj(row_tile0=in_ctx_tiles, n_row_tiles=N_LAT // IN_TM,
                                 col_tile0=0, n_col_tiles=MAIN_COLS // IN_TN)
            ctx_col0 = MAIN_COLS - 2 * SSD_XBC
            p_ctx, dt_ctx = proj(row_tile0=0, n_row_tiles=in_ctx_tiles,
                                 col_tile0=ctx_col0 // IN_TN, n_col_tiles=(MAIN_COLS - ctx_col0) // IN_TN)
            ctx_cols = ((OFF_SSD - ctx_col0) // GROUP_W, (OFF_SSD + GROUP_W - ctx_col0) // SSD_XBC)
            m_lat, m_ctx = _token_mixers(p_ctx, ctx_cols, p_lat, z_xbc_cols, dt_ctx, dt_lat, 0, lw, False)
        out_ctx_tiles = N_CTX // OUT_TM
        if not last:
            m_all = jnp.concatenate([m_ctx, m_lat], axis=0)
            row_tile0 = 0
        else:
            m_all = m_lat
            row_tile0 = out_ctx_tiles
        x_mid, f_packed, logits = _out_projection(m_all, x_all, mod, norm2_g[i], w_out_bf,
                                                  b_out[i].reshape(1, D_MODEL), wr_bf, br,
                                                  row_tile0=row_tile0, ctx_tiles=out_ctx_tiles)
        yb, dest, gates = _moe_ffn(i, f_packed, logits[:, :N_EXPERTS], w_gate_up, b_gate_up, w_down, b_down)
        x_all = _combine(yb, dest, gates, x_mid, mod, final_norm_g if last else None,
                         row_tile0=row_tile0 * OUT_TM // CMB_TM, ctx_tiles=N_CTX // CMB_TM)

    return x_all.reshape(BATCH, SEQ, D_MODEL)
```

```python
import functools

import jax
import jax.numpy as jnp
from jax import lax
from jax.experimental import pallas as pl
from jax.experimental.pallas import tpu as pltpu

F32 = jnp.float32
BF16 = jnp.bfloat16

D_MODEL = 2048
BATCH = 4
SEQ = 2048
DEPTH = 2
GRID_W = 64
CTX_LEN = 256
EPS = 1e-6
GROUP_W = 512
SG_HEADS = 4
SG_CHUNK = 128
SG_HEAD_DIM = 128
SSD_HEAD_DIM = 64
SSD_HEADS = 8
SSD_GROUPS = 2
SSD_STATE = 128
SSD_CHUNK = 128
SSD_XBC = 1024
N_EXPERTS = 32
TOP_K = 4
D_FF = 2048
SWIGLU_LIMIT = 7.0
SWIGLU_ALPHA = 1.702
OFF_CF = 1536
OFF_SG = 2560
OFF_SSD = 3584
MAIN_COLS = 5120
DT_COLS = 2 * SSD_HEADS
LANE = 128
HALF_D = D_MODEL // 2

N_CTX = BATCH * CTX_LEN
N_LAT = BATCH * SEQ

VMEM_LIMIT = 56 * 1024 * 1024

MOD_ROWS = 8
MOD_TN = 1024


def _mod_kernel(c_ref, w_ref, b_ref, o_ref):
    c = c_ref[...]
    s = c * jax.nn.sigmoid(c)
    o_ref[...] = jnp.dot(s.astype(BF16), w_ref[...].astype(BF16), preferred_element_type=F32) + b_ref[...]


def _modulation(cc, w_mod, b_mod):
    n_out = 6 * D_MODEL
    return pl.pallas_call(
        _mod_kernel,
        grid=(DEPTH, n_out // MOD_TN),
        in_specs=[
            pl.BlockSpec((MOD_ROWS, D_MODEL), lambda l, n: (0, 0)),
            pl.BlockSpec((None, D_MODEL, MOD_TN), lambda l, n: (l, 0, n)),
            pl.BlockSpec((None, 1, MOD_TN), lambda l, n: (l, 0, n)),
        ],
        out_specs=pl.BlockSpec((None, MOD_ROWS, MOD_TN), lambda l, n: (l, 0, n)),
        out_shape=jax.ShapeDtypeStruct((DEPTH, MOD_ROWS, n_out), F32),
        compiler_params=pltpu.CompilerParams(
            dimension_semantics=("arbitrary", "arbitrary"), vmem_limit_bytes=VMEM_LIMIT),
        name="adaln_mod",
    )(cc, w_mod, b_mod.reshape(DEPTH, 1, n_out))


def _mod_row(tile, tile_rows, ctx_tiles):
    tiles_per_batch = SEQ // tile_rows
    return jnp.where(tile < ctx_tiles, 0, 1 + (tile - ctx_tiles) // tiles_per_batch)


IN_TM = 1024
IN_TN = 1024
IN_PRO_ROWS = 256


def _inproj_kernel(x_ref, g_ref, mod_ref, w_ref, b_ref, wdt_ref, bdt_ref, o_ref, odt_ref, h_ref):
    @pl.when(pl.program_id(1) == 0)
    def _():
        g = g_ref[...]
        scale = 1.0 + mod_ref[1:2, :]
        shift = mod_ref[0:1, :]
        for r in range(IN_TM // IN_PRO_ROWS):
            rows = slice(r * IN_PRO_ROWS, (r + 1) * IN_PRO_ROWS)
            x = x_ref[rows, :]
            y = x * lax.rsqrt(jnp.mean(x * x, axis=-1, keepdims=True) + EPS)
            h_ref[rows, :] = ((y * g) * scale + shift).astype(BF16)
        odt_ref[...] = jnp.dot(h_ref[...], wdt_ref[...], preferred_element_type=F32) + bdt_ref[...]

    o_ref[...] = jnp.dot(h_ref[...], w_ref[...], preferred_element_type=F32) + b_ref[...]


def _in_projection(x_all, norm_g, mod, w_bf, b, wdt_bf, bdt, *, row_tile0, n_row_tiles, col_tile0, n_col_tiles,
                   ctx_tiles):
    rows = n_row_tiles * IN_TM
    return pl.pallas_call(
        _inproj_kernel,
        grid=(n_row_tiles, n_col_tiles),
        in_specs=[
            pl.BlockSpec((IN_TM, D_MODEL), lambda m, n: (m + row_tile0, 0)),
            pl.BlockSpec((1, D_MODEL), lambda m, n: (0, 0)),
            pl.BlockSpec((None, 6, D_MODEL), lambda m, n: (_mod_row(m + row_tile0, IN_TM, ctx_tiles), 0, 0)),
            pl.BlockSpec((D_MODEL, IN_TN), lambda m, n: (0, n + col_tile0)),
            pl.BlockSpec((1, IN_TN), lambda m, n: (0, n + col_tile0)),
            pl.BlockSpec((D_MODEL, LANE), lambda m, n: (0, 0)),
            pl.BlockSpec((1, LANE), lambda m, n: (0, 0)),
        ],
        out_specs=[
            pl.BlockSpec((IN_TM, IN_TN), lambda m, n: (m, n)),
            pl.BlockSpec((IN_TM, LANE), lambda m, n: (m, 0)),
        ],
        out_shape=[
            jax.ShapeDtypeStruct((rows, n_col_tiles * IN_TN), F32),
            jax.ShapeDtypeStruct((rows, LANE), F32),
        ],
        scratch_shapes=[pltpu.VMEM((IN_TM, D_MODEL), BF16)],
        compiler_params=pltpu.CompilerParams(
            dimension_semantics=("arbitrary", "arbitrary"), vmem_limit_bytes=VMEM_LIMIT),
        name="in_proj",
    )(x_all, norm_g.reshape(1, D_MODEL), mod, w_bf, b, wdt_bf, bdt)


OUT_TM = 256


def _outproj_kernel(m_ref, x_ref, mod_ref, g_ref, w_ref, b_ref, wr_ref, br_ref, xo_ref, f_ref, lg_ref):
    y = jnp.dot(m_ref[...].astype(BF16), w_ref[...], preferred_element_type=F32) + b_ref[...]
    xn = x_ref[...] + mod_ref[2:3, :] * y
    xo_ref[...] = xn
    r = lax.rsqrt(jnp.mean(xn * xn, axis=-1, keepdims=True) + EPS)
    f = ((xn * r) * g_ref[...]) * (1.0 + mod_ref[4:5, :]) + mod_ref[3:4, :]
    fb = f.astype(BF16)
    bits = lax.bitcast_convert_type(fb.astype(F32), jnp.uint32)
    f_ref[...] = (bits[:, HALF_D:] & jnp.uint32(0xFFFF0000)) | (bits[:, :HALF_D] >> 16)
    lg_ref[...] = jnp.dot(fb, wr_ref[...], preferred_element_type=F32) + br_ref[...]


def _out_projection(m, x_all, mod, norm_g, w_bf, b, wr_bf, br, *, row_tile0, ctx_tiles):
    rows = m.shape[0]
    n_tiles = rows // OUT_TM
    return pl.pallas_call(
        _outproj_kernel,
        grid=(n_tiles,),
        in_specs=[
            pl.BlockSpec((OUT_TM, D_MODEL), lambda t: (t, 0)),
            pl.BlockSpec((OUT_TM, D_MODEL), lambda t: (t + row_tile0, 0)),
            pl.BlockSpec((None, 6, D_MODEL), lambda t: (_mod_row(t + row_tile0, OUT_TM, ctx_tiles), 0, 0)),
            pl.BlockSpec((1, D_MODEL), lambda t: (0, 0)),
            pl.BlockSpec((D_MODEL, D_MODEL), lambda t: (0, 0)),
            pl.BlockSpec((1, D_MODEL), lambda t: (0, 0)),
            pl.BlockSpec((D_MODEL, LANE), lambda t: (0, 0)),
            pl.BlockSpec((1, LANE), lambda t: (0, 0)),
        ],
        out_specs=[
            pl.BlockSpec((OUT_TM, D_MODEL), lambda t: (t, 0)),
            pl.BlockSpec((OUT_TM, HALF_D), lambda t: (t, 0)),
            pl.BlockSpec((OUT_TM, LANE), lambda t: (t, 0)),
        ],
        out_shape=[
            jax.ShapeDtypeStruct((rows, D_MODEL), F32),
            jax.ShapeDtypeStruct((rows, HALF_D), jnp.uint32),
            jax.ShapeDtypeStruct((rows, LANE), F32),
        ],
        compiler_params=pltpu.CompilerParams(
            dimension_semantics=("arbitrary",), vmem_limit_bytes=VMEM_LIMIT),
        name="out_proj",
    )(m, x_all, mod, norm_g.reshape(1, D_MODEL), w_bf, b, wr_bf, br)


MOE_UNIT = 128
MOE_SUB = 2 * MOE_UNIT
MOE_CHUNK = 2048
MOE_NSUB = MOE_CHUNK // MOE_SUB
MOE_PIECES = (8, 4, 2, 1)
MOE_TF = 256
MOE_F_STEPS = D_FF // MOE_TF
MOE_N_STEPS = D_MODEL // MOE_TF
MOE_STEPS = MOE_F_STEPS + MOE_N_STEPS


def _moe_rows(n_tokens):
    slots = n_tokens * TOP_K
    padded = slots + N_EXPERTS * (MOE_UNIT - 1)
    padded = -(-padded // MOE_UNIT) * MOE_UNIT
    return padded


def _moe_chunks(n_tokens):
    return _moe_rows(n_tokens) // MOE_CHUNK + N_EXPERTS


def _row_copy_wait(src_hbm, dst, sem, rows):
    pltpu.make_async_copy(src_hbm.at[pl.ds(0, rows)], dst, sem).wait()


def _moe_kernel(ce_ref, cs_ref, cn_ref, *refs):
    (tok_ref, tok2_ref, f_hbm, wg_ref, wu_ref, bg_ref, bu_ref, wd_ref, bd_ref, yb_hbm, xw, xs, hs, os_, sem_in,
     sem_out) = refs
    c = pl.program_id(0)
    j = pl.program_id(1)
    cur = jnp.maximum(c - 1, 0)
    units = jnp.where(c >= 1, cn_ref[cur], 0)
    nsub = (units + 1) // 2
    start = pl.multiple_of(cs_ref[cur], MOE_UNIT)

    nxt = jnp.minimum(c, cn_ref.shape[0] - 1)
    units_next = jnp.where(c < pl.num_programs(0) - 1, cn_ref[nxt], 0)
    fetch = jnp.logical_and(j < MOE_F_STEPS, 2 * j < units_next)

    def gather_issue():
        base = pl.multiple_of(j * MOE_SUB, MOE_SUB)
        for half, toks in enumerate((tok_ref, tok2_ref)):
            for r in range(MOE_UNIT):
                pltpu.make_async_copy(f_hbm.at[pl.ds(toks[0, r], 1)], xw.at[pl.ds(base + half * MOE_UNIT + r, 1)],
                                      sem_in.at[j]).start()

    def gather_wait(i):
        _row_copy_wait(f_hbm, xw.at[pl.ds(i * MOE_SUB, MOE_SUB)], sem_in.at[i], MOE_SUB)

    def unpack(i):
        rows = slice(i * MOE_SUB, (i + 1) * MOE_SUB)
        w = xw[rows, :]
        xs[rows, :HALF_D] = lax.bitcast_convert_type(w << 16, F32).astype(BF16)
        xs[rows, HALF_D:] = lax.bitcast_convert_type(w & jnp.uint32(0xFFFF0000), F32).astype(BF16)

    def copy_out(slot, off, n, col):
        return pltpu.make_async_copy(
            os_.at[slot, pl.ds(off, n * MOE_UNIT), :],
            yb_hbm.at[pl.ds(start + off, n * MOE_UNIT), pl.ds(col, MOE_TF)],
            sem_out.at[slot])

    def for_valid_subs(fn):
        fn(0)
        for i in range(1, MOE_NSUB):
            @pl.when(i < nsub)
            def _(i=i):
                fn(i)

    big = MOE_PIECES[0]

    def for_pieces(fn):
        n_big = units // big
        rem = units - big * n_big
        for q in range(MOE_CHUNK // (big * MOE_UNIT)):
            @pl.when(q < n_big)
            def _(q=q):
                fn(q * big * MOE_UNIT, big, q == 0)
        base = n_big * (big * MOE_UNIT)
        for size in MOE_PIECES[1:]:
            above = (rem // (2 * size)) * (2 * size)

            @pl.when((rem // size) % 2 == 1)
            def _(size=size, above=above):
                fn(pl.multiple_of(base + above * MOE_UNIT, MOE_UNIT), size, False)

    @pl.when(jnp.logical_and(nsub > 0, j == 0))
    def _():
        def land(i):
            gather_wait(i)
            unpack(i)

        for_valid_subs(land)

    @pl.when(jnp.logical_and(fetch, units < big))
    def _():
        gather_issue()

    @pl.when(nsub > 0)
    def _():
        @pl.when(j < MOE_F_STEPS)
        def _():
            bg = bg_ref[...]
            bu = bu_ref[...]

            def gate_up_piece(off, n):
                rows = pl.ds(off, n * MOE_UNIT)
                x = xs[rows, :]
                g = jnp.dot(x, wg_ref[...].astype(BF16), preferred_element_type=F32) + bg
                u = jnp.dot(x, wu_ref[...].astype(BF16), preferred_element_type=F32) + bu
                g = jnp.minimum(g, SWIGLU_LIMIT)
                u = jnp.clip(u, -SWIGLU_LIMIT, SWIGLU_LIMIT)
                h = (u + 1.0) * (g * jax.nn.sigmoid(SWIGLU_ALPHA * g))
                hs[j, rows, :] = h.astype(BF16)

            def gate_up(off, n, first_big):
                if not first_big:
                    gate_up_piece(off, n)
                    return

                @pl.when(fetch)
                def _():
                    gather_issue()
                    gate_up_piece(off, n)

                @pl.when(jnp.logical_not(fetch))
                def _():
                    gate_up_piece(off, n)

            for_pieces(gate_up)

        @pl.when(j >= MOE_F_STEPS)
        def _():
            jn = j - MOE_F_STEPS
            slot = jn % 2
            col = pl.multiple_of(jn * MOE_TF, MOE_TF)
            bd = bd_ref[...]

            @pl.when(jn >= 2)
            def _():
                for_pieces(lambda off, n, _: copy_out(slot, off, n, col).wait())

            def down(off, n, _):
                rows = pl.ds(off, n * MOE_UNIT)
                h = jnp.concatenate([hs[f, rows, :] for f in range(MOE_F_STEPS)], axis=1)
                os_[slot, rows, :] = jnp.dot(h, wd_ref[...].astype(BF16), preferred_element_type=F32) + bd
                copy_out(slot, off, n, col).start()

            for_pieces(down)

            @pl.when(jn == MOE_N_STEPS - 1)
            def _():
                for_pieces(lambda off, n, _: copy_out(1 - slot, off, n, col).wait())
                for_pieces(lambda off, n, _: copy_out(slot, off, n, col).wait())


def _moe_experts(layer, chunk_e, chunk_start, chunk_nsub, used_chunks, buf_tok, f_packed, w_gate_up, b_gate_up,
                 w_down, b_down):
    n_chunks = chunk_e.shape[0]
    rows = buf_tok.shape[0]
    n_units = rows // MOE_UNIT

    def gu_idx(half):
        def idx(c, j, ce, cs, cn):
            cur = jnp.maximum(c - 1, 0)
            jj = jnp.where(cn[cur] > 0, jnp.minimum(j, MOE_F_STEPS - 1), MOE_F_STEPS - 1)
            jj = jnp.where(c == 0, 0, jj)
            return (layer, ce[cur], 0, half * MOE_F_STEPS + jj)
        return idx

    def d_idx(c, j, ce, cs, cn):
        cur = jnp.maximum(c - 1, 0)
        jj = jnp.where(cn[cur] > 0, jnp.maximum(j - MOE_F_STEPS, 0), MOE_N_STEPS - 1)
        jj = jnp.where(c == 0, 0, jj)
        return (layer, ce[cur], 0, jj)

    def tok_idx(half):
        def idx(c, j, ce, cs, cn):
            nxt = jnp.minimum(c, n_chunks - 1)
            unit = cs[nxt] // MOE_UNIT + 2 * jnp.minimum(j, MOE_F_STEPS - 1) + half
            return (jnp.minimum(unit, n_units - 1), 0, 0)
        return idx

    grid_spec = pltpu.PrefetchScalarGridSpec(
        num_scalar_prefetch=3,
        grid=(used_chunks + 1, MOE_STEPS),
        in_specs=[
            pl.BlockSpec((None, 1, MOE_UNIT), tok_idx(0), memory_space=pltpu.SMEM),
            pl.BlockSpec((None, 1, MOE_UNIT), tok_idx(1), memory_space=pltpu.SMEM),
            pl.BlockSpec(memory_space=pl.ANY),
            pl.BlockSpec((None, None, D_MODEL, MOE_TF), gu_idx(0)),
            pl.BlockSpec((None, None, D_MODEL, MOE_TF), gu_idx(1)),
            pl.BlockSpec((None, None, 1, MOE_TF), gu_idx(0)),
            pl.BlockSpec((None, None, 1, MOE_TF), gu_idx(1)),
            pl.BlockSpec((None, None, D_FF, MOE_TF), d_idx),
            pl.BlockSpec((None, None, 1, MOE_TF), d_idx),
        ],
        out_specs=pl.BlockSpec(memory_space=pl.ANY),
        scratch_shapes=[
            pltpu.VMEM((MOE_CHUNK, HALF_D), jnp.uint32),
            pltpu.VMEM((MOE_CHUNK, D_MODEL), BF16),
            pltpu.VMEM((MOE_F_STEPS, MOE_CHUNK, MOE_TF), BF16),
            pltpu.VMEM((2, MOE_CHUNK, MOE_TF), F32),
            pltpu.SemaphoreType.DMA((MOE_NSUB,)),
            pltpu.SemaphoreType.DMA((2,)),
        ],
    )
    bgu = b_gate_up.reshape(DEPTH, N_EXPERTS, 1, 2 * D_FF)
    bd = b_down.reshape(DEPTH, N_EXPERTS, 1, D_MODEL)
    return pl.pallas_call(
        _moe_kernel,
        grid_spec=grid_spec,
        out_shape=jax.ShapeDtypeStruct((rows, D_MODEL), F32),
        compiler_params=pltpu.CompilerParams(
            dimension_semantics=("arbitrary", "arbitrary"), vmem_limit_bytes=VMEM_LIMIT),
        name="moe_experts",
    )(chunk_e, chunk_start, chunk_nsub, *([buf_tok.reshape(n_units, 1, MOE_UNIT)] * 2), f_packed,
      w_gate_up, w_gate_up, bgu, bgu, w_down, bd)


CMB_TM = 256


def _combine_kernel(idx_ref, yb_hbm, gates_ref, x_ref, mod_ref, *rest, final):
    if final:
        g_ref, o_ref, buf, sem = rest
    else:
        o_ref, buf, sem = rest

    for k in range(TOP_K):
        def issue(r, carry, k=k):
            pltpu.make_async_copy(yb_hbm.at[pl.ds(idx_ref[0, k * CMB_TM + r], 1)], buf.at[k, pl.ds(r, 1)],
                                  sem.at[k]).start()
            return carry

        lax.fori_loop(0, CMB_TM, issue, 0, unroll=8)

    gates = gates_ref[...]
    y = None
    for k in range(TOP_K):
        _row_copy_wait(yb_hbm, buf.at[k], sem.at[k], CMB_TM)
        t = gates[:, k:k + 1] * buf[k]
        y = t if y is None else y + t
    xn = x_ref[...] + mod_ref[5:6, :] * y
    if final:
        xn = xn * lax.rsqrt(jnp.mean(xn * xn, axis=-1, keepdims=True) + EPS) * g_ref[...]
    o_ref[...] = xn


def _combine(yb, dest, gates, x_mid, mod, final_g, *, row_tile0, ctx_tiles):
    n = x_mid.shape[0]
    n_tiles = n // CMB_TM
    final = final_g is not None
    idx = dest.reshape(n_tiles, CMB_TM, TOP_K).transpose(0, 2, 1).reshape(n_tiles, 1, TOP_K * CMB_TM)
    in_specs = [
        pl.BlockSpec((None, 1, TOP_K * CMB_TM), lambda t: (t, 0, 0), memory_space=pltpu.SMEM),
        pl.BlockSpec(memory_space=pl.ANY),
        pl.BlockSpec((CMB_TM, TOP_K), lambda t: (t, 0)),
        pl.BlockSpec((CMB_TM, D_MODEL), lambda t: (t, 0)),
        pl.BlockSpec((None, 6, D_MODEL), lambda t: (_mod_row(t + row_tile0, CMB_TM, ctx_tiles), 0, 0)),
    ]
    args = [idx, yb, gates, x_mid, mod]
    if final:
        in_specs.append(pl.BlockSpec((1, D_MODEL), lambda t: (0, 0)))
        args.append(final_g.reshape(1, D_MODEL))
    return pl.pallas_call(
        functools.partial(_combine_kernel, final=final),
        grid=(n_tiles,),
        in_specs=in_specs,
        out_specs=pl.BlockSpec((CMB_TM, D_MODEL), lambda t: (t, 0)),
        out_shape=jax.ShapeDtypeStruct((n, D_MODEL), F32),
        scratch_shapes=[pltpu.VMEM((TOP_K, CMB_TM, D_MODEL), F32), pltpu.SemaphoreType.DMA((TOP_K,))],
        compiler_params=pltpu.CompilerParams(dimension_semantics=("arbitrary",), vmem_limit_bytes=VMEM_LIMIT),
        name="moe_combine",
    )(*args)


def _moe_ffn(layer, f_packed, logits, w_gate_up, b_gate_up, w_down, b_down):
    n = f_packed.shape[0]
    slots = n * TOP_K
    buf_rows = _moe_rows(n)
    n_chunks = _moe_chunks(n)

    top_logit, top_e = lax.top_k(logits, TOP_K)
    gates = jax.nn.softmax(top_logit, axis=-1)
    flat_e = top_e.reshape(-1).astype(jnp.int32)
    onehot = (flat_e[:, None] == jnp.arange(N_EXPERTS, dtype=jnp.int32)[None, :]).astype(jnp.int32)
    running = jnp.cumsum(onehot, axis=0)
    counts = running[-1]
    padded = (counts + MOE_UNIT - 1) // MOE_UNIT * MOE_UNIT
    pad_end = jnp.cumsum(padded)
    pad_start = pad_end - padded
    dest_of_slot = jnp.sum(onehot * (running - 1 + pad_start[None, :]), axis=1)
    buf_tok = jnp.zeros((buf_rows,), jnp.int32).at[dest_of_slot].set(jnp.arange(slots, dtype=jnp.int32) // TOP_K)

    e_chunks = (padded + MOE_CHUNK - 1) // MOE_CHUNK
    chunk_end = jnp.cumsum(e_chunks)
    total_chunks = chunk_end[-1]
    cidx = jnp.arange(n_chunks, dtype=jnp.int32)
    ce = jnp.minimum(jnp.searchsorted(chunk_end, cidx, side='right'), N_EXPERTS - 1).astype(jnp.int32)
    local = cidx - (chunk_end[ce] - e_chunks[ce])
    valid = cidx < total_chunks
    c_start = jnp.where(valid, pad_start[ce] + local * MOE_CHUNK, 0).astype(jnp.int32)
    c_nsub = jnp.where(valid, jnp.minimum(MOE_CHUNK, padded[ce] - local * MOE_CHUNK) // MOE_UNIT, 0).astype(jnp.int32)
    last_e = ce[jnp.maximum(total_chunks - 1, 0)]
    ce = jnp.where(valid, ce, last_e).astype(jnp.int32)

    yb = _moe_experts(layer, ce, c_start, c_nsub, total_chunks.astype(jnp.int32), buf_tok, f_packed,
                      w_gate_up, b_gate_up, w_down, b_down)
    return yb, dest_of_slot.reshape(n, TOP_K), gates


MIX_TB = 256
CONV_HALO = 16
CONV_PIECE = 64
NEG_INF = float("-inf")


def _conv_pitch(rowlen):
    return rowlen + 2 * CONV_HALO


def _dwconv_block(u, pad_ref, w_ref, rowlen):
    taps = w_ref.shape[0]
    half = taps // 2
    pitch = _conv_pitch(rowlen)
    ch = u.shape[1]
    zeros = jnp.zeros((CONV_HALO, ch), F32)
    for r in range(MIX_TB // rowlen):
        base = r * pitch
        pad_ref[base:base + CONV_HALO, :] = zeros
        pad_ref[base + CONV_HALO:base + CONV_HALO + rowlen, :] = u[r * rowlen:(r + 1) * rowlen, :]
        pad_ref[base + CONV_HALO + rowlen:base + pitch, :] = zeros
    outs = []
    for r in range(MIX_TB // rowlen):
        for piece in range(rowlen // CONV_PIECE):
            acc = None
            for j in range(taps):
                off = r * pitch + CONV_HALO + piece * CONV_PIECE + j - half
                term = pad_ref[off:off + CONV_PIECE, :] * w_ref[j:j + 1, :]
                acc = term if acc is None else acc + term
            outs.append(acc)
    return jnp.concatenate(outs, axis=0)


def _ln_rows(x, g, b):
    xc = x - jnp.mean(x, axis=-1, keepdims=True)
    return xc * lax.rsqrt(jnp.mean(xc * xc, axis=-1, keepdims=True) + EPS) * g + b


def _silu(x):
    return x * jax.nn.sigmoid(x)


SSD_PAIRS = SSD_HEADS // 2
SSD_PAIR_W = 2 * SSD_HEAD_DIM


def _ssd_kernel(xbc_f, dt_f, xbc_b, dt_b, init_ref, cw_ref, cb_ref, dtb_ref, a_ref, skip_ref, y_f, y_b, fin_ref,
                pad_f, pad_b, st_ref, *, rowlen):
    s = pl.program_id(1)

    @pl.when(s == 0)
    def _():
        st_ref[...] = init_ref[...]

    for d, (xbc_ref, dt_ref, y_ref, pad_ref) in enumerate(((xbc_f, dt_f, y_f, pad_f), (xbc_b, dt_b, y_b, pad_b))):
        _ssd_direction(xbc_ref, dt_ref, cw_ref, cb_ref, dtb_ref, a_ref, skip_ref[d:d + 1, :], y_ref, pad_ref,
                       st_ref.at[d], rowlen=rowlen, rev=d == 1, dcol=d * SSD_HEADS)

    @pl.when(s == pl.num_programs(1) - 1)
    def _():
        fin_ref[...] = st_ref[...]


def _ssd_direction(xbc_ref, dt_ref, cw_ref, cb_ref, dtb_ref, a_ref, skip_row, y_ref, pad_ref, st_ref, *, rowlen, rev,
                   dcol):
    xa = _silu(_dwconv_block(xbc_ref[...], pad_ref, cw_ref, rowlen) + cb_ref[...])
    dt_all = dt_ref[...] + dtb_ref[...]
    dt_all = jnp.maximum(dt_all, 0.0) + jnp.log1p(jnp.exp(-jnp.abs(dt_all)))
    da_all = dt_all * a_ref[...]

    row_i = lax.broadcasted_iota(jnp.int32, (SSD_CHUNK, SSD_CHUNK), 0)
    col_i = lax.broadcasted_iota(jnp.int32, (SSD_CHUNK, SSD_CHUNK), 1)
    tri = (col_i >= row_i) if rev else (col_i <= row_i)
    tri_f = tri.astype(F32)
    first_half = lax.broadcasted_iota(jnp.int32, (SSD_CHUNK, SSD_PAIR_W), 1) < SSD_HEAD_DIM
    first_half_row = first_half[0:1, :]

    chunks = range(MIX_TB // SSD_CHUNK)
    for ci in (reversed(chunks) if rev else chunks):
        rows = slice(ci * SSD_CHUNK, (ci + 1) * SSD_CHUNK)
        acc = jnp.dot(tri_f, da_all[rows, :], preferred_element_type=F32, precision=lax.Precision.HIGHEST)
        acc_t = acc.T
        tot = acc[0:1, :] if rev else acc[SSD_CHUNK - 1:SSD_CHUNK, :]
        to_end = jnp.exp(tot - acc)
        from_start = jnp.exp(acc)
        chunk_decay = jnp.exp(tot)
        dt_c = dt_all[rows, :]
        for g in range(SSD_GROUPS):
            bg = xa[rows, GROUP_W + g * SSD_STATE:GROUP_W + (g + 1) * SSD_STATE]
            cg = xa[rows, GROUP_W + (SSD_GROUPS + g) * SSD_STATE:GROUP_W + (SSD_GROUPS + g + 1) * SSD_STATE]
            scores = lax.dot_general(cg.astype(BF16), bg.astype(BF16), (((1,), (1,)), ((), ())),
                                     preferred_element_type=F32)
            pairs_per_group = SSD_PAIRS // SSD_GROUPS
            for p in range(g * pairs_per_group, (g + 1) * pairs_per_group):
                c0 = dcol + 2 * p
                c1 = c0 + 1
                lhs, bw = [], []
                for col in (c0, c1):
                    seg = acc[:, col:col + 1] - acc_t[col:col + 1, :]
                    lhs.append(scores * jnp.exp(jnp.where(tri, seg, NEG_INF)))
                for col in (c0, c1):
                    lhs.append(cg * from_start[:, col:col + 1])
                    bw.append((bg * to_end[:, col:col + 1]).T)
                xp = xa[rows, p * SSD_PAIR_W:(p + 1) * SSD_PAIR_W]
                xd = xp * jnp.where(first_half, dt_c[:, c0:c0 + 1], dt_c[:, c1:c1 + 1])
                xd_top = jnp.where(first_half, xd, 0.0).astype(BF16)
                xd_bot = jnp.where(first_half, 0.0, xd).astype(BF16)
                st = st_ref[p]
                st_top = jnp.where(first_half, st, 0.0).astype(BF16)
                st_bot = jnp.where(first_half, 0.0, st).astype(BF16)
                y = jnp.dot(jnp.concatenate(lhs, axis=1).astype(BF16),
                            jnp.concatenate([xd_top, xd_bot, st_top, st_bot], axis=0),
                            preferred_element_type=F32)
                y_ref[rows, p * SSD_PAIR_W:(p + 1) * SSD_PAIR_W] = y + skip_row[:, p * SSD_PAIR_W:(p + 1) * SSD_PAIR_W] * xp
                upd = jnp.dot(jnp.concatenate(bw, axis=1).astype(BF16), jnp.concatenate([xd_top, xd_bot], axis=0),
                              preferred_element_type=F32)
                decay_lane = jnp.where(first_half_row, chunk_decay[:, c0:c0 + 1], chunk_decay[:, c1:c1 + 1])
                st_ref[p] = st * decay_lane + upd


def _ssd_sweep(xbc_src, xbc_col_block, dt_src, row_block0, n_blocks, init, prm, *, rowlen):
    cw, cb, dtb_row, a_row, skip_rows = prm

    def fwd(b, s):
        return b * n_blocks + s

    def bwd(b, s):
        return b * n_blocks + (n_blocks - 1 - s)

    state_block = (None, 2, SSD_PAIRS, SSD_STATE, SSD_PAIR_W)
    y_shape = jax.ShapeDtypeStruct((BATCH * n_blocks * MIX_TB, GROUP_W), F32)
    pad_shape = pltpu.VMEM(((MIX_TB // rowlen) * _conv_pitch(rowlen), SSD_XBC), F32)
    return pl.pallas_call(
        functools.partial(_ssd_kernel, rowlen=rowlen),
        grid=(BATCH, n_blocks),
        in_specs=[
            pl.BlockSpec((MIX_TB, SSD_XBC), lambda b, s: (row_block0 + fwd(b, s), xbc_col_block)),
            pl.BlockSpec((MIX_TB, LANE), lambda b, s: (row_block0 + fwd(b, s), 0)),
            pl.BlockSpec((MIX_TB, SSD_XBC), lambda b, s: (row_block0 + bwd(b, s), xbc_col_block)),
            pl.BlockSpec((MIX_TB, LANE), lambda b, s: (row_block0 + bwd(b, s), 0)),
            pl.BlockSpec(state_block, lambda b, s: (b, 0, 0, 0, 0)),
            pl.BlockSpec((3, SSD_XBC), lambda b, s: (0, 0)),
            pl.BlockSpec((1, SSD_XBC), lambda b, s: (0, 0)),
            pl.BlockSpec((1, LANE), lambda b, s: (0, 0)),
            pl.BlockSpec((1, LANE), lambda b, s: (0, 0)),
            pl.BlockSpec((2, GROUP_W), lambda b, s: (0, 0)),
        ],
        out_specs=[
            pl.BlockSpec((MIX_TB, GROUP_W), lambda b, s: (fwd(b, s), 0)),
            pl.BlockSpec((MIX_TB, GROUP_W), lambda b, s: (bwd(b, s), 0)),
            pl.BlockSpec(state_block, lambda b, s: (b, 0, 0, 0, 0)),
        ],
        out_shape=[y_shape, y_shape, jax.ShapeDtypeStruct((BATCH, 2, SSD_PAIRS, SSD_STATE, SSD_PAIR_W), F32)],
        scratch_shapes=[pad_shape, pad_shape, pltpu.VMEM((2, SSD_PAIRS, SSD_STATE, SSD_PAIR_W), F32)],
        compiler_params=pltpu.CompilerParams(
            dimension_semantics=("arbitrary", "arbitrary"), vmem_limit_bytes=VMEM_LIMIT),
        name="ssd_sweep",
    )(xbc_src, dt_src, xbc_src, dt_src, init, cw, cb, dtb_row, a_row, skip_rows)


def _ssd_params(conv_w, conv_b, dt_bias, a_log, d_skip):
    pad = LANE - 2 * SSD_HEADS
    dtb_row = jnp.pad(dt_bias.reshape(-1), (0, pad)).reshape(1, LANE)
    a_row = jnp.pad(-jnp.exp(a_log.reshape(-1)), (0, pad)).reshape(1, LANE)
    skip_rows = jnp.repeat(d_skip, SSD_HEAD_DIM, axis=1)
    return conv_w, conv_b.reshape(1, SSD_XBC), dtb_row, a_row, skip_rows


def _local_kernel(p_ref, z_ref, y0_ref, y1_ref, scw_ref, cfw_ref, cfb_ref, cfg_ref, cfbeta_ref, sgg_ref,
                  sgbeta_ref, sgw_ref, sgb_ref, ng_ref, m_ref, pad_ref, *, rowlen):
    gate_b = p_ref[:, 0:GROUP_W]
    u = p_ref[:, GROUP_W:2 * GROUP_W] * p_ref[:, 2 * GROUP_W:3 * GROUP_W]
    m_ref[:, 0:GROUP_W] = gate_b * _dwconv_block(u, pad_ref, scw_ref, rowlen)
    u = p_ref[:, OFF_CF:OFF_CF + GROUP_W] * jax.nn.sigmoid(p_ref[:, OFF_CF + GROUP_W:OFF_SG])
    u = _dwconv_block(u, pad_ref, cfw_ref, rowlen) + cfb_ref[...]
    m_ref[:, GROUP_W:2 * GROUP_W] = _silu(_ln_rows(u, cfg_ref[...], cfbeta_ref[...]))
    q = p_ref[:, OFF_SG:OFF_SSD]
    q = 0.5 * q * (1.0 + lax.erf(q * (2.0 ** -0.5)))
    v = _ln_rows(q[:, GROUP_W:], sgg_ref[...], sgbeta_ref[...]).astype(BF16)
    n_chunks = MIX_TB // SG_CHUNK
    for h in range(SG_HEADS):
        cols = slice(h * SG_HEAD_DIM, (h + 1) * SG_HEAD_DIM)
        rhs = jnp.concatenate([v[c * SG_CHUNK:(c + 1) * SG_CHUNK, cols] for c in range(n_chunks)], axis=1)
        sres = jnp.dot(sgw_ref[h].astype(BF16), rhs, preferred_element_type=F32) + sgb_ref[:, h:h + 1]
        for c in range(n_chunks):
            rows = slice(c * SG_CHUNK, (c + 1) * SG_CHUNK)
            m_ref[rows, 2 * GROUP_W + h * SG_HEAD_DIM:2 * GROUP_W + (h + 1) * SG_HEAD_DIM] = (
                q[rows, cols] * sres[:, c * SG_HEAD_DIM:(c + 1) * SG_HEAD_DIM])
    yv = (y0_ref[...] + y1_ref[...]) * _silu(z_ref[...])
    gw = GROUP_W // SSD_GROUPS
    for g in range(SSD_GROUPS):
        vg = yv[:, g * gw:(g + 1) * gw]
        vg = vg * lax.rsqrt(jnp.mean(vg * vg, axis=-1, keepdims=True) + EPS)
        m_ref[:, 3 * GROUP_W + g * gw:3 * GROUP_W + (g + 1) * gw] = vg * ng_ref[:, g * gw:(g + 1) * gw]


def _local_mixers_call(p_src, z_col_block, row_block0, n_blocks, y0, y1, lw, *, rowlen):
    vec = lambda a: a.reshape(1, GROUP_W)
    args = [lw['sc_conv_w'], lw['cf_conv_w'], vec(lw['cf_conv_b']), vec(lw['cf_ln_g']), vec(lw['cf_ln_b']),
            vec(lw['sg_ln_g']), vec(lw['sg_ln_b']), lw['sg_w'], lw['sg_b'].T, vec(lw['ssd_norm_g'])]
    full = lambda a: pl.BlockSpec(a.shape, lambda t, nd=a.ndim: (0,) * nd)
    return pl.pallas_call(
        functools.partial(_local_kernel, rowlen=rowlen),
        grid=(n_blocks,),
        in_specs=[
            pl.BlockSpec((MIX_TB, OFF_SSD), lambda t: (t + row_block0, 0)),
            pl.BlockSpec((MIX_TB, GROUP_W), lambda t: (t + row_block0, z_col_block)),
            pl.BlockSpec((MIX_TB, GROUP_W), lambda t: (t, 0)),
            pl.BlockSpec((MIX_TB, GROUP_W), lambda t: (t, 0)),
        ] + [full(a) for a in args],
        out_specs=pl.BlockSpec((MIX_TB, D_MODEL), lambda t: (t, 0)),
        out_shape=jax.ShapeDtypeStruct((n_blocks * MIX_TB, D_MODEL), F32),
        scratch_shapes=[pltpu.VMEM(((MIX_TB // rowlen) * _conv_pitch(rowlen), GROUP_W), F32)],
        compiler_params=pltpu.CompilerParams(dimension_semantics=("arbitrary",), vmem_limit_bytes=VMEM_LIMIT),
        name="local_mixers",
    )(p_src, p_src, y0, y1, *args)


def _token_mixers(p_ctx_src, ctx_cols, p_lat_src, lat_cols, dt_ctx, dt_lat, lat_row_block0, lw, ctx_out):
    prm = _ssd_params(lw['ssd_conv_w'], lw['ssd_conv_b'], lw['ssd_dt_bias'], lw['ssd_a_log'], lw['ssd_d'])
    zero_state = jnp.zeros((BATCH, 2, SSD_PAIRS, SSD_STATE, SSD_PAIR_W), F32)
    ctx_blocks = CTX_LEN // MIX_TB
    lat_blocks = SEQ // MIX_TB
    *y_ctx, state = _ssd_sweep(p_ctx_src, ctx_cols[1], dt_ctx, 0, ctx_blocks, zero_state, prm, rowlen=CTX_LEN)
    *y_lat, _ = _ssd_sweep(p_lat_src, lat_cols[1], dt_lat, lat_row_block0, lat_blocks, state, prm, rowlen=GRID_W)
    m_lat = _local_mixers_call(p_lat_src, lat_cols[0], lat_row_block0, BATCH * lat_blocks, y_lat[0], y_lat[1], lw,
                               rowlen=GRID_W)
    m_ctx = None
    if ctx_out:
        m_ctx = _local_mixers_call(p_ctx_src, ctx_cols[0], 0, BATCH * ctx_blocks, y_ctx[0], y_ctx[1], lw,
                                   rowlen=CTX_LEN)
    return m_lat, m_ctx


def kernel(x, c, ctx, c_ctx, w_mod, b_mod, norm1_g, norm2_g, w_in, b_in, sc_conv_w, cf_conv_w, cf_conv_b,
           cf_ln_g, cf_ln_b, sg_ln_g, sg_ln_b, sg_w, sg_b, ssd_conv_w, ssd_conv_b, ssd_dt_bias, ssd_a_log,
           ssd_d, ssd_norm_g, w_out, b_out, w_router, b_router, w_gate_up, b_gate_up, w_down, b_down,
           final_norm_g):
    cc = jnp.concatenate([c_ctx[None, :], c, jnp.zeros((MOD_ROWS - 1 - BATCH, D_MODEL), F32)], axis=0)
    mod_all = _modulation(cc, w_mod, b_mod).reshape(DEPTH, MOD_ROWS, 6, D_MODEL)

    x_all = jnp.concatenate([ctx.reshape(N_CTX, D_MODEL), x.reshape(N_LAT, D_MODEL)], axis=0)

    for i in range(DEPTH):
        last = i == DEPTH - 1
        lw = dict(sc_conv_w=sc_conv_w[i], cf_conv_w=cf_conv_w[i], cf_conv_b=cf_conv_b[i], cf_ln_g=cf_ln_g[i],
                  cf_ln_b=cf_ln_b[i], sg_ln_g=sg_ln_g[i], sg_ln_b=sg_ln_b[i], sg_w=sg_w[i], sg_b=sg_b[i],
                  ssd_conv_w=ssd_conv_w[i], ssd_conv_b=ssd_conv_b[i], ssd_dt_bias=ssd_dt_bias[i],
                  ssd_a_log=ssd_a_log[i], ssd_d=ssd_d[i], ssd_norm_g=ssd_norm_g[i])
        mod = mod_all[i]
        w_in_bf = w_in[i, :, :MAIN_COLS].astype(BF16)
        b_in_main = b_in[i, :MAIN_COLS].reshape(1, MAIN_COLS)
        wdt_bf = jnp.pad(w_in[i, :, MAIN_COLS:], ((0, 0), (0, LANE - DT_COLS))).astype(BF16)
        bdt = jnp.pad(b_in[i, MAIN_COLS:], (0, LANE - DT_COLS)).reshape(1, LANE)
        w_out_bf = w_out[i].astype(BF16)
        wr_bf = jnp.pad(w_router[i], ((0, 0), (0, LANE - N_EXPERTS))).astype(BF16)
        br = jnp.pad(b_router[i], (0, LANE - N_EXPERTS)).reshape(1, LANE)
        in_ctx_tiles = N_CTX // IN_TM
        proj = functools.partial(_in_projection, x_all, norm1_g[i], mod, w_in_bf, b_in_main, wdt_bf, bdt,
                                 ctx_tiles=in_ctx_tiles)

        z_xbc_cols = (OFF_SSD // GROUP_W, (OFF_SSD + GROUP_W) // SSD_XBC)
        if not last:
            p_all, dt_all = proj(row_tile0=0, n_row_tiles=(N_CTX + N_LAT) // IN_TM,
                                 col_tile0=0, n_col_tiles=MAIN_COLS // IN_TN)
            m_lat, m_ctx = _token_mixers(p_all, z_xbc_cols, p_all, z_xbc_cols, dt_all, dt_all, N_CTX // MIX_TB, lw, True)
        else:
            p_lat, dt_lat = proj(row_tile0=in_ctx_tiles, n_row_tiles=N_LAT // IN_TM,
                                 col_tile0=0, n_col_tiles=MAIN_COLS // IN_TN)
            ctx_col0 = MAIN_COLS - 2 * SSD_XBC
            p_ctx, dt_ctx = proj(row_tile0=0, n_row_tiles=in_ctx_tiles,
                                 col_tile0=ctx_col0 // IN_TN, n_col_tiles=(MAIN_COLS - ctx_col0) // IN_TN)
            ctx_cols = ((OFF_SSD - ctx_col0) // GROUP_W, (OFF_SSD + GROUP_W - ctx_col0) // SSD_XBC)
            m_lat, m_ctx = _token_mixers(p_ctx, ctx_cols, p_lat, z_xbc_cols, dt_ctx, dt_lat, 0, lw, False)
        out_ctx_tiles = N_CTX // OUT_TM
        if not last:
            m_all = jnp.concatenate([m_ctx, m_lat], axis=0)
            row_tile0 = 0
        else:
            m_all = m_lat
            row_tile0 = out_ctx_tiles
        x_mid, f_packed, logits = _out_projection(m_all, x_all, mod, norm2_g[i], w_out_bf,
                                                  b_out[i].reshape(1, D_MODEL), wr_bf, br,
                                                  row_tile0=row_tile0, ctx_tiles=out_ctx_tiles)
        yb, dest, gates = _moe_ffn(i, f_packed, logits[:, :N_EXPERTS], w_gate_up, b_gate_up, w_down, b_down)
        x_all = _combine(yb, dest, gates, x_mid, mod, final_norm_g if last else None,
                         row_tile0=row_tile0 * OUT_TM // CMB_TM, ctx_tiles=N_CTX // CMB_TM)

    return x_all.reshape(BATCH, SEQ, D_MODEL)
```

```python
import functools

import jax
import jax.numpy as jnp
from jax import lax
from jax.experimental import pallas as pl
from jax.experimental.pallas import tpu as pltpu

F32 = jnp.float32
BF16 = jnp.bfloat16

D_MODEL = 2048
BATCH = 4
SEQ = 2048
DEPTH = 2
GRID_W = 64
CTX_LEN = 256
EPS = 1e-6
GROUP_W = 512
SG_HEADS = 4
SG_CHUNK = 128
SG_HEAD_DIM = 128
SSD_HEAD_DIM = 64
SSD_HEADS = 8
SSD_GROUPS = 2
SSD_STATE = 128
SSD_CHUNK = 128
SSD_XBC = 1024
N_EXPERTS = 32
TOP_K = 4
D_FF = 2048
SWIGLU_LIMIT = 7.0
SWIGLU_ALPHA = 1.702
OFF_CF = 1536
OFF_SG = 2560
OFF_SSD = 3584
MAIN_COLS = 5120
DT_COLS = 2 * SSD_HEADS
LANE = 128
HALF_D = D_MODEL // 2

N_CTX = BATCH * CTX_LEN
N_LAT = BATCH * SEQ

VMEM_LIMIT = 56 * 1024 * 1024

MOD_ROWS = 8
MOD_TN = 1024


def _mod_kernel(c_ref, w_ref, b_ref, o_ref):
    c = c_ref[...]
    s = c * jax.nn.sigmoid(c)
    o_ref[...] = jnp.dot(s.astype(BF16), w_ref[...].astype(BF16), preferred_element_type=F32) + b_ref[...]


def _modulation(cc, w_mod, b_mod):
    n_out = 6 * D_MODEL
    return pl.pallas_call(
        _mod_kernel,
        grid=(DEPTH, n_out // MOD_TN),
        in_specs=[
            pl.BlockSpec((MOD_ROWS, D_MODEL), lambda l, n: (0, 0)),
            pl.BlockSpec((None, D_MODEL, MOD_TN), lambda l, n: (l, 0, n)),
            pl.BlockSpec((None, 1, MOD_TN), lambda l, n: (l, 0, n)),
        ],
        out_specs=pl.BlockSpec((None, MOD_ROWS, MOD_TN), lambda l, n: (l, 0, n)),
        out_shape=jax.ShapeDtypeStruct((DEPTH, MOD_ROWS, n_out), F32),
        compiler_params=pltpu.CompilerParams(
            dimension_semantics=("arbitrary", "arbitrary"), vmem_limit_bytes=VMEM_LIMIT),
        name="adaln_mod",
    )(cc, w_mod, b_mod.reshape(DEPTH, 1, n_out))


def _mod_row(tile, tile_rows, ctx_tiles):
    tiles_per_batch = SEQ // tile_rows
    return jnp.where(tile < ctx_tiles, 0, 1 + (tile - ctx_tiles) // tiles_per_batch)


IN_TM = 1024
IN_TN = 1024
IN_PRO_ROWS = 256


def _inproj_kernel(x_ref, g_ref, mod_ref, w_ref, b_ref, wdt_ref, bdt_ref, o_ref, odt_ref, h_ref):
    @pl.when(pl.program_id(1) == 0)
    def _():
        g = g_ref[...]
        scale = 1.0 + mod_ref[1:2, :]
        shift = mod_ref[0:1, :]
        for r in range(IN_TM // IN_PRO_ROWS):
            rows = slice(r * IN_PRO_ROWS, (r + 1) * IN_PRO_ROWS)
            x = x_ref[rows, :]
            y = x * lax.rsqrt(jnp.mean(x * x, axis=-1, keepdims=True) + EPS)
            h_ref[rows, :] = ((y * g) * scale + shift).astype(BF16)
        odt_ref[...] = jnp.dot(h_ref[...], wdt_ref[...], preferred_element_type=F32) + bdt_ref[...]

    o_ref[...] = jnp.dot(h_ref[...], w_ref[...], preferred_element_type=F32) + b_ref[...]


def _in_projection(x_all, norm_g, mod, w_bf, b, wdt_bf, bdt, *, row_tile0, n_row_tiles, col_tile0, n_col_tiles,
                   ctx_tiles):
    rows = n_row_tiles * IN_TM
    return pl.pallas_call(
        _inproj_kernel,
        grid=(n_row_tiles, n_col_tiles),
        in_specs=[
            pl.BlockSpec((IN_TM, D_MODEL), lambda m, n: (m + row_tile0, 0)),
            pl.BlockSpec((1, D_MODEL), lambda m, n: (0, 0)),
            pl.BlockSpec((None, 6, D_MODEL), lambda m, n: (_mod_row(m + row_tile0, IN_TM, ctx_tiles), 0, 0)),
            pl.BlockSpec((D_MODEL, IN_TN), lambda m, n: (0, n + col_tile0)),
            pl.BlockSpec((1, IN_TN), lambda m, n: (0, n + col_tile0)),
            pl.BlockSpec((D_MODEL, LANE), lambda m, n: (0, 0)),
            pl.BlockSpec((1, LANE), lambda m, n: (0, 0)),
        ],
        out_specs=[
            pl.BlockSpec((IN_TM, IN_TN), lambda m, n: (m, n)),
            pl.BlockSpec((IN_TM, LANE), lambda m, n: (m, 0)),
        ],
        out_shape=[
            jax.ShapeDtypeStruct((rows, n_col_tiles * IN_TN), F32),
            jax.ShapeDtypeStruct((rows, LANE), F32),
        ],
        scratch_shapes=[pltpu.VMEM((IN_TM, D_MODEL), BF16)],
        compiler_params=pltpu.CompilerParams(
            dimension_semantics=("arbitrary", "arbitrary"), vmem_limit_bytes=VMEM_LIMIT),
        name="in_proj",
    )(x_all, norm_g.reshape(1, D_MODEL), mod, w_bf, b, wdt_bf, bdt)


OUT_TM = 256


def _outproj_kernel(m_ref, x_ref, mod_ref, g_ref, w_ref, b_ref, wr_ref, br_ref, xo_ref, f_ref, lg_ref):
    y = jnp.dot(m_ref[...].astype(BF16), w_ref[...], preferred_element_type=F32) + b_ref[...]
    xn = x_ref[...] + mod_ref[2:3, :] * y
    xo_ref[...] = xn
    r = lax.rsqrt(jnp.mean(xn * xn, axis=-1, keepdims=True) + EPS)
    f = ((xn * r) * g_ref[...]) * (1.0 + mod_ref[4:5, :]) + mod_ref[3:4, :]
    fb = f.astype(BF16)
    bits = lax.bitcast_convert_type(fb.astype(F32), jnp.uint32)
    f_ref[...] = (bits[:, HALF_D:] & jnp.uint32(0xFFFF0000)) | (bits[:, :HALF_D] >> 16)
    lg_ref[...] = jnp.dot(fb, wr_ref[...], preferred_element_type=F32) + br_ref[...]


def _out_projection(m, x_all, mod, norm_g, w_bf, b, wr_bf, br, *, row_tile0, ctx_tiles):
    rows = m.shape[0]
    n_tiles = rows // OUT_TM
    return pl.pallas_call(
        _outproj_kernel,
        grid=(n_tiles,),
        in_specs=[
            pl.BlockSpec((OUT_TM, D_MODEL), lambda t: (t, 0)),
            pl.BlockSpec((OUT_TM, D_MODEL), lambda t: (t + row_tile0, 0)),
            pl.BlockSpec((None, 6, D_MODEL), lambda t: (_mod_row(t + row_tile0, OUT_TM, ctx_tiles), 0, 0)),
            pl.BlockSpec((1, D_MODEL), lambda t: (0, 0)),
            pl.BlockSpec((D_MODEL, D_MODEL), lambda t: (0, 0)),
            pl.BlockSpec((1, D_MODEL), lambda t: (0, 0)),
            pl.BlockSpec((D_MODEL, LANE), lambda t: (0, 0)),
            pl.BlockSpec((1, LANE), lambda t: (0, 0)),
        ],
        out_specs=[
            pl.BlockSpec((OUT_TM, D_MODEL), lambda t: (t, 0)),
            pl.BlockSpec((OUT_TM, HALF_D), lambda t: (t, 0)),
            pl.BlockSpec((OUT_TM, LANE), lambda t: (t, 0)),
        ],
        out_shape=[
            jax.ShapeDtypeStruct((rows, D_MODEL), F32),
            jax.ShapeDtypeStruct((rows, HALF_D), jnp.uint32),
            jax.ShapeDtypeStruct((rows, LANE), F32),
        ],
        compiler_params=pltpu.CompilerParams(
            dimension_semantics=("arbitrary",), vmem_limit_bytes=VMEM_LIMIT),
        name="out_proj",
    )(m, x_all, mod, norm_g.reshape(1, D_MODEL), w_bf, b, wr_bf, br)


MOE_UNIT = 128
MOE_SUB = 2 * MOE_UNIT
MOE_CHUNK = 2048
MOE_NSUB = MOE_CHUNK // MOE_SUB
MOE_PIECES = (8, 4, 2, 1)
MOE_TF = 256
MOE_TN = 512
MOE_F_STEPS = D_FF // MOE_TF
MOE_N_STEPS = D_MODEL // MOE_TN
MOE_STEPS = MOE_F_STEPS + MOE_N_STEPS


def _moe_rows(n_tokens):
    slots = n_tokens * TOP_K
    padded = slots + N_EXPERTS * (MOE_UNIT - 1)
    padded = -(-padded // MOE_UNIT) * MOE_UNIT
    return padded


def _moe_chunks(n_tokens):
    return _moe_rows(n_tokens) // MOE_CHUNK + N_EXPERTS


def _row_copy_wait(src_hbm, dst, sem, rows):
    pltpu.make_async_copy(src_hbm.at[pl.ds(0, rows)], dst, sem).wait()


def _moe_kernel(ce_ref, cs_ref, cn_ref, *refs):
    (tok_ref, tok2_ref, f_hbm, wg_ref, wu_ref, bg_ref, bu_ref, wd_ref, bd_ref, yb_hbm, xw, xs, hs, os_, sem_in,
     sem_out) = refs
    c = pl.program_id(0)
    j = pl.program_id(1)
    cur = jnp.maximum(c - 1, 0)
    units = jnp.where(c >= 1, cn_ref[cur], 0)
    nsub = (units + 1) // 2
    start = pl.multiple_of(cs_ref[cur], MOE_UNIT)

    nxt = jnp.minimum(c, cn_ref.shape[0] - 1)
    units_next = jnp.where(c < pl.num_programs(0) - 1, cn_ref[nxt], 0)
    fetch = jnp.logical_and(j < MOE_F_STEPS, 2 * j < units_next)

    def gather_issue():
        base = pl.multiple_of(j * MOE_SUB, MOE_SUB)
        for half, toks in enumerate((tok_ref, tok2_ref)):
            for r in range(MOE_UNIT):
                pltpu.make_async_copy(f_hbm.at[pl.ds(toks[0, r], 1)], xw.at[pl.ds(base + half * MOE_UNIT + r, 1)],
                                      sem_in.at[j]).start()

    def gather_wait(i):
        _row_copy_wait(f_hbm, xw.at[pl.ds(i * MOE_SUB, MOE_SUB)], sem_in.at[i], MOE_SUB)

    def unpack(i):
        rows = slice(i * MOE_SUB, (i + 1) * MOE_SUB)
        w = xw[rows, :]
        xs[rows, :HALF_D] = lax.bitcast_convert_type(w << 16, F32).astype(BF16)
        xs[rows, HALF_D:] = lax.bitcast_convert_type(w & jnp.uint32(0xFFFF0000), F32).astype(BF16)

    def copy_out(slot, off, n, col):
        return pltpu.make_async_copy(
            os_.at[slot, pl.ds(off, n * MOE_UNIT), :],
            yb_hbm.at[pl.ds(start + off, n * MOE_UNIT), pl.ds(col, MOE_TN)],
            sem_out.at[slot])

    def for_valid_subs(fn):
        fn(0)
        for i in range(1, MOE_NSUB):
            @pl.when(i < nsub)
            def _(i=i):
                fn(i)

    big = MOE_PIECES[0]

    def for_pieces(fn):
        n_big = units // big
        rem = units - big * n_big
        for q in range(MOE_CHUNK // (big * MOE_UNIT)):
            @pl.when(q < n_big)
            def _(q=q):
                fn(q * big * MOE_UNIT, big, q == 0)
        base = n_big * (big * MOE_UNIT)
        for size in MOE_PIECES[1:]:
            above = (rem // (2 * size)) * (2 * size)

            @pl.when((rem // size) % 2 == 1)
            def _(size=size, above=above):
                fn(pl.multiple_of(base + above * MOE_UNIT, MOE_UNIT), size, False)

    @pl.when(jnp.logical_and(nsub > 0, j == 0))
    def _():
        def land(i):
            gather_wait(i)
            unpack(i)

        for_valid_subs(land)

    @pl.when(jnp.logical_and(fetch, units < big))
    def _():
        gather_issue()

    @pl.when(nsub > 0)
    def _():
        @pl.when(j < MOE_F_STEPS)
        def _():
            bg = bg_ref[...]
            bu = bu_ref[...]

            def gate_up_piece(off, n):
                rows = pl.ds(off, n * MOE_UNIT)
                x = xs[rows, :]
                g = jnp.dot(x, wg_ref[...].astype(BF16), preferred_element_type=F32) + bg
                u = jnp.dot(x, wu_ref[...].astype(BF16), preferred_element_type=F32) + bu
                g = jnp.minimum(g, SWIGLU_LIMIT)
                u = jnp.clip(u, -SWIGLU_LIMIT, SWIGLU_LIMIT)
                h = (u + 1.0) * (g * jax.nn.sigmoid(SWIGLU_ALPHA * g))
                hs[j, rows, :] = h.astype(BF16)

            def gate_up(off, n, first_big):
                if not first_big:
                    gate_up_piece(off, n)
                    return

                @pl.when(fetch)
                def _():
                    gather_issue()
                    gate_up_piece(off, n)

                @pl.when(jnp.logical_not(fetch))
                def _():
                    gate_up_piece(off, n)

            for_pieces(gate_up)

        @pl.when(j >= MOE_F_STEPS)
        def _():
            jn = j - MOE_F_STEPS
            slot = jn % 2
            col = pl.multiple_of(jn * MOE_TN, MOE_TN)
            bd = bd_ref[...]

            @pl.when(jn >= 2)
            def _():
                for_pieces(lambda off, n, _: copy_out(slot, off, n, col).wait())

            def down(off, n, _):
                rows = pl.ds(off, n * MOE_UNIT)
                h = jnp.concatenate([hs[f, rows, :] for f in range(MOE_F_STEPS)], axis=1)
                os_[slot, rows, :] = jnp.dot(h, wd_ref[...].astype(BF16), preferred_element_type=F32) + bd
                copy_out(slot, off, n, col).start()

            for_pieces(down)

            @pl.when(jn == MOE_N_STEPS - 1)
            def _():
                for_pieces(lambda off, n, _: copy_out(1 - slot, off, n, col).wait())
                for_pieces(lambda off, n, _: copy_out(slot, off, n, col).wait())


def _moe_experts(layer, chunk_e, chunk_start, chunk_nsub, used_chunks, buf_tok, f_packed, w_gate_up, b_gate_up,
                 w_down, b_down):
    n_chunks = chunk_e.shape[0]
    rows = buf_tok.shape[0]
    n_units = rows // MOE_UNIT

    def gu_idx(half):
        def idx(c, j, ce, cs, cn):
            cur = jnp.maximum(c - 1, 0)
            jj = jnp.where(cn[cur] > 0, jnp.minimum(j, MOE_F_STEPS - 1), MOE_F_STEPS - 1)
            jj = jnp.where(c == 0, 0, jj)
            return (layer, ce[cur], 0, half * MOE_F_STEPS + jj)
        return idx

    def d_idx(c, j, ce, cs, cn):
        cur = jnp.maximum(c - 1, 0)
        jj = jnp.where(cn[cur] > 0, jnp.maximum(j - MOE_F_STEPS, 0), MOE_N_STEPS - 1)
        jj = jnp.where(c == 0, 0, jj)
        return (layer, ce[cur], 0, jj)

    def tok_idx(half):
        def idx(c, j, ce, cs, cn):
            nxt = jnp.minimum(c, n_chunks - 1)
            unit = cs[nxt] // MOE_UNIT + 2 * jnp.minimum(j, MOE_F_STEPS - 1) + half
            return (jnp.minimum(unit, n_units - 1), 0, 0)
        return idx

    grid_spec = pltpu.PrefetchScalarGridSpec(
        num_scalar_prefetch=3,
        grid=(used_chunks + 1, MOE_STEPS),
        in_specs=[
            pl.BlockSpec((None, 1, MOE_UNIT), tok_idx(0), memory_space=pltpu.SMEM),
            pl.BlockSpec((None, 1, MOE_UNIT), tok_idx(1), memory_space=pltpu.SMEM),
            pl.BlockSpec(memory_space=pl.ANY),
            pl.BlockSpec((None, None, D_MODEL, MOE_TF), gu_idx(0)),
            pl.BlockSpec((None, None, D_MODEL, MOE_TF), gu_idx(1)),
            pl.BlockSpec((None, None, 1, MOE_TF), gu_idx(0)),
            pl.BlockSpec((None, None, 1, MOE_TF), gu_idx(1)),
            pl.BlockSpec((None, None, D_FF, MOE_TN), d_idx),
            pl.BlockSpec((None, None, 1, MOE_TN), d_idx),
        ],
        out_specs=pl.BlockSpec(memory_space=pl.ANY),
        scratch_shapes=[
            pltpu.VMEM((MOE_CHUNK, HALF_D), jnp.uint32),
            pltpu.VMEM((MOE_CHUNK, D_MODEL), BF16),
            pltpu.VMEM((MOE_F_STEPS, MOE_CHUNK, MOE_TF), BF16),
            pltpu.VMEM((2, MOE_CHUNK, MOE_TN), F32),
            pltpu.SemaphoreType.DMA((MOE_NSUB,)),
            pltpu.SemaphoreType.DMA((2,)),
        ],
    )
    bgu = b_gate_up.reshape(DEPTH, N_EXPERTS, 1, 2 * D_FF)
    bd = b_down.reshape(DEPTH, N_EXPERTS, 1, D_MODEL)
    return pl.pallas_call(
        _moe_kernel,
        grid_spec=grid_spec,
        out_shape=jax.ShapeDtypeStruct((rows, D_MODEL), F32),
        compiler_params=pltpu.CompilerParams(
            dimension_semantics=("arbitrary", "arbitrary"), vmem_limit_bytes=VMEM_LIMIT),
        name="moe_experts",
    )(chunk_e, chunk_start, chunk_nsub, *([buf_tok.reshape(n_units, 1, MOE_UNIT)] * 2), f_packed,
      w_gate_up, w_gate_up, bgu, bgu, w_down, bd)


CMB_TM = 256


def _combine_kernel(idx_ref, yb_hbm, gates_ref, x_ref, mod_ref, *rest, final):
    if final:
        g_ref, o_ref, buf, sem = rest
    else:
        o_ref, buf, sem = rest

    for k in range(TOP_K):
        def issue(r, carry, k=k):
            pltpu.make_async_copy(yb_hbm.at[pl.ds(idx_ref[0, k * CMB_TM + r], 1)], buf.at[k, pl.ds(r, 1)],
                                  sem.at[k]).start()
            return carry

        lax.fori_loop(0, CMB_TM, issue, 0, unroll=8)

    gates = gates_ref[...]
    y = None
    for k in range(TOP_K):
        _row_copy_wait(yb_hbm, buf.at[k], sem.at[k], CMB_TM)
        t = gates[:, k:k + 1] * buf[k]
        y = t if y is None else y + t
    xn = x_ref[...] + mod_ref[5:6, :] * y
    if final:
        xn = xn * lax.rsqrt(jnp.mean(xn * xn, axis=-1, keepdims=True) + EPS) * g_ref[...]
    o_ref[...] = xn


def _combine(yb, dest, gates, x_mid, mod, final_g, *, row_tile0, ctx_tiles):
    n = x_mid.shape[0]
    n_tiles = n // CMB_TM
    final = final_g is not None
    idx = dest.reshape(n_tiles, CMB_TM, TOP_K).transpose(0, 2, 1).reshape(n_tiles, 1, TOP_K * CMB_TM)
    in_specs = [
        pl.BlockSpec((None, 1, TOP_K * CMB_TM), lambda t: (t, 0, 0), memory_space=pltpu.SMEM),
        pl.BlockSpec(memory_space=pl.ANY),
        pl.BlockSpec((CMB_TM, TOP_K), lambda t: (t, 0)),
        pl.BlockSpec((CMB_TM, D_MODEL), lambda t: (t, 0)),
        pl.BlockSpec((None, 6, D_MODEL), lambda t: (_mod_row(t + row_tile0, CMB_TM, ctx_tiles), 0, 0)),
    ]
    args = [idx, yb, gates, x_mid, mod]
    if final:
        in_specs.append(pl.BlockSpec((1, D_MODEL), lambda t: (0, 0)))
        args.append(final_g.reshape(1, D_MODEL))
    return pl.pallas_call(
        functools.partial(_combine_kernel, final=final),
        grid=(n_tiles,),
        in_specs=in_specs,
        out_specs=pl.BlockSpec((CMB_TM, D_MODEL), lambda t: (t, 0)),
        out_shape=jax.ShapeDtypeStruct((n, D_MODEL), F32),
        scratch_shapes=[pltpu.VMEM((TOP_K, CMB_TM, D_MODEL), F32), pltpu.SemaphoreType.DMA((TOP_K,))],
        compiler_params=pltpu.CompilerParams(dimension_semantics=("arbitrary",), vmem_limit_bytes=VMEM_LIMIT),
        name="moe_combine",
    )(*args)


def _moe_ffn(layer, f_packed, logits, w_gate_up, b_gate_up, w_down, b_down):
    n = f_packed.shape[0]
    slots = n * TOP_K
    buf_rows = _moe_rows(n)
    n_chunks = _moe_chunks(n)

    top_logit, top_e = lax.top_k(logits, TOP_K)
    gates = jax.nn.softmax(top_logit, axis=-1)
    flat_e = top_e.reshape(-1).astype(jnp.int32)
    onehot = (flat_e[:, None] == jnp.arange(N_EXPERTS, dtype=jnp.int32)[None, :]).astype(jnp.int32)
    running = jnp.cumsum(onehot, axis=0)
    counts = running[-1]
    padded = (counts + MOE_UNIT - 1) // MOE_UNIT * MOE_UNIT
    pad_end = jnp.cumsum(padded)
    pad_start = pad_end - padded
    dest_of_slot = jnp.sum(onehot * (running - 1 + pad_start[None, :]), axis=1)
    buf_tok = jnp.zeros((buf_rows,), jnp.int32).at[dest_of_slot].set(
        jnp.arange(slots, dtype=jnp.int32) // TOP_K, unique_indices=True, mode='promise_in_bounds')

    e_chunks = (padded + MOE_CHUNK - 1) // MOE_CHUNK
    chunk_end = jnp.cumsum(e_chunks)
    total_chunks = chunk_end[-1]
    cidx = jnp.arange(n_chunks, dtype=jnp.int32)
    ce = jnp.minimum(jnp.searchsorted(chunk_end, cidx, side='right'), N_EXPERTS - 1).astype(jnp.int32)
    local = cidx - (chunk_end[ce] - e_chunks[ce])
    valid = cidx < total_chunks
    c_start = jnp.where(valid, pad_start[ce] + local * MOE_CHUNK, 0).astype(jnp.int32)
    c_nsub = jnp.where(valid, jnp.minimum(MOE_CHUNK, padded[ce] - local * MOE_CHUNK) // MOE_UNIT, 0).astype(jnp.int32)
    last_e = ce[jnp.maximum(total_chunks - 1, 0)]
    ce = jnp.where(valid, ce, last_e).astype(jnp.int32)

    yb = _moe_experts(layer, ce, c_start, c_nsub, total_chunks.astype(jnp.int32), buf_tok, f_packed,
                      w_gate_up, b_gate_up, w_down, b_down)
    return yb, dest_of_slot.reshape(n, TOP_K), gates


MIX_TB = 256
CONV_HALO = 16
CONV_PIECE = 64
NEG_INF = float("-inf")


def _conv_pitch(rowlen):
    return rowlen + 2 * CONV_HALO


def _dwconv_block(u, pad_ref, w_ref, rowlen):
    taps = w_ref.shape[0]
    half = taps // 2
    pitch = _conv_pitch(rowlen)
    ch = u.shape[1]
    zeros = jnp.zeros((CONV_HALO, ch), F32)
    for r in range(MIX_TB // rowlen):
        base = r * pitch
        pad_ref[base:base + CONV_HALO, :] = zeros
        pad_ref[base + CONV_HALO:base + CONV_HALO + rowlen, :] = u[r * rowlen:(r + 1) * rowlen, :]
        pad_ref[base + CONV_HALO + rowlen:base + pitch, :] = zeros
    outs = []
    for r in range(MIX_TB // rowlen):
        for piece in range(rowlen // CONV_PIECE):
            acc = None
            for j in range(taps):
                off = r * pitch + CONV_HALO + piece * CONV_PIECE + j - half
                term = pad_ref[off:off + CONV_PIECE, :] * w_ref[j:j + 1, :]
                acc = term if acc is None else acc + term
            outs.append(acc)
    return jnp.concatenate(outs, axis=0)


def _ln_rows(x, g, b):
    xc = x - jnp.mean(x, axis=-1, keepdims=True)
    return xc * lax.rsqrt(jnp.mean(xc * xc, axis=-1, keepdims=True) + EPS) * g + b


def _silu(x):
    return x * jax.nn.sigmoid(x)


SSD_PAIRS = SSD_HEADS // 2
SSD_PAIR_W = 2 * SSD_HEAD_DIM


def _ssd_kernel(xbc_f, dt_f, xbc_b, dt_b, init_ref, cw_ref, cb_ref, dtb_ref, a_ref, skip_ref, y_f, y_b, fin_ref,
                pad_f, pad_b, st_ref, *, rowlen):
    s = pl.program_id(1)

    @pl.when(s == 0)
    def _():
        st_ref[...] = init_ref[...]

    for d, (xbc_ref, dt_ref, y_ref, pad_ref) in enumerate(((xbc_f, dt_f, y_f, pad_f), (xbc_b, dt_b, y_b, pad_b))):
        _ssd_direction(xbc_ref, dt_ref, cw_ref, cb_ref, dtb_ref, a_ref, skip_ref[d:d + 1, :], y_ref, pad_ref,
                       st_ref.at[d], rowlen=rowlen, rev=d == 1, dcol=d * SSD_HEADS)

    @pl.when(s == pl.num_programs(1) - 1)
    def _():
        fin_ref[...] = st_ref[...]


def _ssd_direction(xbc_ref, dt_ref, cw_ref, cb_ref, dtb_ref, a_ref, skip_row, y_ref, pad_ref, st_ref, *, rowlen, rev,
                   dcol):
    xa = _silu(_dwconv_block(xbc_ref[...], pad_ref, cw_ref, rowlen) + cb_ref[...])
    dt_all = dt_ref[...] + dtb_ref[...]
    dt_all = jnp.maximum(dt_all, 0.0) + jnp.log1p(jnp.exp(-jnp.abs(dt_all)))
    da_all = dt_all * a_ref[...]

    row_i = lax.broadcasted_iota(jnp.int32, (SSD_CHUNK, SSD_CHUNK), 0)
    col_i = lax.broadcasted_iota(jnp.int32, (SSD_CHUNK, SSD_CHUNK), 1)
    tri = (col_i >= row_i) if rev else (col_i <= row_i)
    tri_f = tri.astype(F32)
    first_half = lax.broadcasted_iota(jnp.int32, (SSD_CHUNK, SSD_PAIR_W), 1) < SSD_HEAD_DIM
    first_half_row = first_half[0:1, :]

    chunks = range(MIX_TB // SSD_CHUNK)
    for ci in (reversed(chunks) if rev else chunks):
        rows = slice(ci * SSD_CHUNK, (ci + 1) * SSD_CHUNK)
        acc = jnp.dot(tri_f, da_all[rows, :], preferred_element_type=F32, precision=lax.Precision.HIGHEST)
        acc_t = acc.T
        tot = acc[0:1, :] if rev else acc[SSD_CHUNK - 1:SSD_CHUNK, :]
        to_end = jnp.exp(tot - acc)
        from_start = jnp.exp(acc)
        chunk_decay = jnp.exp(tot)
        dt_c = dt_all[rows, :]
        for g in range(SSD_GROUPS):
            bg = xa[rows, GROUP_W + g * SSD_STATE:GROUP_W + (g + 1) * SSD_STATE]
            cg = xa[rows, GROUP_W + (SSD_GROUPS + g) * SSD_STATE:GROUP_W + (SSD_GROUPS + g + 1) * SSD_STATE]
            scores = lax.dot_general(cg.astype(BF16), bg.astype(BF16), (((1,), (1,)), ((), ())),
                                     preferred_element_type=F32)
            pairs_per_group = SSD_PAIRS // SSD_GROUPS
            for p in range(g * pairs_per_group, (g + 1) * pairs_per_group):
                c0 = dcol + 2 * p
                c1 = c0 + 1
                lhs, bw = [], []
                for col in (c0, c1):
                    seg = acc[:, col:col + 1] - acc_t[col:col + 1, :]
                    lhs.append(scores * jnp.exp(jnp.where(tri, seg, NEG_INF)))
                for col in (c0, c1):
                    lhs.append(cg * from_start[:, col:col + 1])
                    bw.append((bg * to_end[:, col:col + 1]).T)
                xp = xa[rows, p * SSD_PAIR_W:(p + 1) * SSD_PAIR_W]
                xd = xp * jnp.where(first_half, dt_c[:, c0:c0 + 1], dt_c[:, c1:c1 + 1])
                xd_top = jnp.where(first_half, xd, 0.0).astype(BF16)
                xd_bot = jnp.where(first_half, 0.0, xd).astype(BF16)
                st = st_ref[p]
                st_top = jnp.where(first_half, st, 0.0).astype(BF16)
                st_bot = jnp.where(first_half, 0.0, st).astype(BF16)
                y = jnp.dot(jnp.concatenate(lhs, axis=1).astype(BF16),
                            jnp.concatenate([xd_top, xd_bot, st_top, st_bot], axis=0),
                            preferred_element_type=F32)
                y_ref[rows, p * SSD_PAIR_W:(p + 1) * SSD_PAIR_W] = y + skip_row[:, p * SSD_PAIR_W:(p + 1) * SSD_PAIR_W] * xp
                upd = jnp.dot(jnp.concatenate(bw, axis=1).astype(BF16), jnp.concatenate([xd_top, xd_bot], axis=0),
                              preferred_element_type=F32)
                decay_lane = jnp.where(first_half_row, chunk_decay[:, c0:c0 + 1], chunk_decay[:, c1:c1 + 1])
                st_ref[p] = st * decay_lane + upd


def _ssd_sweep(xbc_src, xbc_col_block, dt_src, row_block0, n_blocks, init, prm, *, rowlen):
    cw, cb, dtb_row, a_row, skip_rows = prm

    def fwd(b, s):
        return b * n_blocks + s

    def bwd(b, s):
        return b * n_blocks + (n_blocks - 1 - s)

    state_block = (None, 2, SSD_PAIRS, SSD_STATE, SSD_PAIR_W)
    y_shape = jax.ShapeDtypeStruct((BATCH * n_blocks * MIX_TB, GROUP_W), F32)
    pad_shape = pltpu.VMEM(((MIX_TB // rowlen) * _conv_pitch(rowlen), SSD_XBC), F32)
    return pl.pallas_call(
        functools.partial(_ssd_kernel, rowlen=rowlen),
        grid=(BATCH, n_blocks),
        in_specs=[
            pl.BlockSpec((MIX_TB, SSD_XBC), lambda b, s: (row_block0 + fwd(b, s), xbc_col_block)),
            pl.BlockSpec((MIX_TB, LANE), lambda b, s: (row_block0 + fwd(b, s), 0)),
            pl.BlockSpec((MIX_TB, SSD_XBC), lambda b, s: (row_block0 + bwd(b, s), xbc_col_block)),
            pl.BlockSpec((MIX_TB, LANE), lambda b, s: (row_block0 + bwd(b, s), 0)),
            pl.BlockSpec(state_block, lambda b, s: (b, 0, 0, 0, 0)),
            pl.BlockSpec((3, SSD_XBC), lambda b, s: (0, 0)),
            pl.BlockSpec((1, SSD_XBC), lambda b, s: (0, 0)),
            pl.BlockSpec((1, LANE), lambda b, s: (0, 0)),
            pl.BlockSpec((1, LANE), lambda b, s: (0, 0)),
            pl.BlockSpec((2, GROUP_W), lambda b, s: (0, 0)),
        ],
        out_specs=[
            pl.BlockSpec((MIX_TB, GROUP_W), lambda b, s: (fwd(b, s), 0)),
            pl.BlockSpec((MIX_TB, GROUP_W), lambda b, s: (bwd(b, s), 0)),
            pl.BlockSpec(state_block, lambda b, s: (b, 0, 0, 0, 0)),
        ],
        out_shape=[y_shape, y_shape, jax.ShapeDtypeStruct((BATCH, 2, SSD_PAIRS, SSD_STATE, SSD_PAIR_W), F32)],
        scratch_shapes=[pad_shape, pad_shape, pltpu.VMEM((2, SSD_PAIRS, SSD_STATE, SSD_PAIR_W), F32)],
        compiler_params=pltpu.CompilerParams(
            dimension_semantics=("arbitrary", "arbitrary"), vmem_limit_bytes=VMEM_LIMIT),
        name="ssd_sweep",
    )(xbc_src, dt_src, xbc_src, dt_src, init, cw, cb, dtb_row, a_row, skip_rows)


def _ssd_params(conv_w, conv_b, dt_bias, a_log, d_skip):
    pad = LANE - 2 * SSD_HEADS
    dtb_row = jnp.pad(dt_bias.reshape(-1), (0, pad)).reshape(1, LANE)
    a_row = jnp.pad(-jnp.exp(a_log.reshape(-1)), (0, pad)).reshape(1, LANE)
    skip_rows = jnp.repeat(d_skip, SSD_HEAD_DIM, axis=1)
    return conv_w, conv_b.reshape(1, SSD_XBC), dtb_row, a_row, skip_rows


def _local_kernel(p_ref, z_ref, y0_ref, y1_ref, scw_ref, cfw_ref, cfb_ref, cfg_ref, cfbeta_ref, sgg_ref,
                  sgbeta_ref, sgw_ref, sgb_ref, ng_ref, m_ref, pad_ref, *, rowlen):
    gate_b = p_ref[:, 0:GROUP_W]
    u = p_ref[:, GROUP_W:2 * GROUP_W] * p_ref[:, 2 * GROUP_W:3 * GROUP_W]
    m_ref[:, 0:GROUP_W] = gate_b * _dwconv_block(u, pad_ref, scw_ref, rowlen)
    u = p_ref[:, OFF_CF:OFF_CF + GROUP_W] * jax.nn.sigmoid(p_ref[:, OFF_CF + GROUP_W:OFF_SG])
    u = _dwconv_block(u, pad_ref, cfw_ref, rowlen) + cfb_ref[...]
    m_ref[:, GROUP_W:2 * GROUP_W] = _silu(_ln_rows(u, cfg_ref[...], cfbeta_ref[...]))
    q = p_ref[:, OFF_SG:OFF_SSD]
    q = 0.5 * q * (1.0 + lax.erf(q * (2.0 ** -0.5)))
    v = _ln_rows(q[:, GROUP_W:], sgg_ref[...], sgbeta_ref[...]).astype(BF16)
    n_chunks = MIX_TB // SG_CHUNK
    for h in range(SG_HEADS):
        cols = slice(h * SG_HEAD_DIM, (h + 1) * SG_HEAD_DIM)
        rhs = jnp.concatenate([v[c * SG_CHUNK:(c + 1) * SG_CHUNK, cols] for c in range(n_chunks)], axis=1)
        sres = jnp.dot(sgw_ref[h].astype(BF16), rhs, preferred_element_type=F32) + sgb_ref[:, h:h + 1]
        for c in range(n_chunks):
            rows = slice(c * SG_CHUNK, (c + 1) * SG_CHUNK)
            m_ref[rows, 2 * GROUP_W + h * SG_HEAD_DIM:2 * GROUP_W + (h + 1) * SG_HEAD_DIM] = (
                q[rows, cols] * sres[:, c * SG_HEAD_DIM:(c + 1) * SG_HEAD_DIM])
    yv = (y0_ref[...] + y1_ref[...]) * _silu(z_ref[...])
    gw = GROUP_W // SSD_GROUPS
    for g in range(SSD_GROUPS):
        vg = yv[:, g * gw:(g + 1) * gw]
        vg = vg * lax.rsqrt(jnp.mean(vg * vg, axis=-1, keepdims=True) + EPS)
        m_ref[:, 3 * GROUP_W + g * gw:3 * GROUP_W + (g + 1) * gw] = vg * ng_ref[:, g * gw:(g + 1) * gw]


def _local_mixers_call(p_src, z_col_block, row_block0, n_blocks, y0, y1, lw, *, rowlen):
    vec = lambda a: a.reshape(1, GROUP_W)
    args = [lw['sc_conv_w'], lw['cf_conv_w'], vec(lw['cf_conv_b']), vec(lw['cf_ln_g']), vec(lw['cf_ln_b']),
            vec(lw['sg_ln_g']), vec(lw['sg_ln_b']), lw['sg_w'], lw['sg_b'].T, vec(lw['ssd_norm_g'])]
    full = lambda a: pl.BlockSpec(a.shape, lambda t, nd=a.ndim: (0,) * nd)
    return pl.pallas_call(
        functools.partial(_local_kernel, rowlen=rowlen),
        grid=(n_blocks,),
        in_specs=[
            pl.BlockSpec((MIX_TB, OFF_SSD), lambda t: (t + row_block0, 0)),
            pl.BlockSpec((MIX_TB, GROUP_W), lambda t: (t + row_block0, z_col_block)),
            pl.BlockSpec((MIX_TB, GROUP_W), lambda t: (t, 0)),
            pl.BlockSpec((MIX_TB, GROUP_W), lambda t: (t, 0)),
        ] + [full(a) for a in args],
        out_specs=pl.BlockSpec((MIX_TB, D_MODEL), lambda t: (t, 0)),
        out_shape=jax.ShapeDtypeStruct((n_blocks * MIX_TB, D_MODEL), F32),
        scratch_shapes=[pltpu.VMEM(((MIX_TB // rowlen) * _conv_pitch(rowlen), GROUP_W), F32)],
        compiler_params=pltpu.CompilerParams(dimension_semantics=("arbitrary",), vmem_limit_bytes=VMEM_LIMIT),
        name="local_mixers",
    )(p_src, p_src, y0, y1, *args)


def _token_mixers(p_ctx_src, ctx_cols, p_lat_src, lat_cols, dt_ctx, dt_lat, lat_row_block0, lw, ctx_out):
    prm = _ssd_params(lw['ssd_conv_w'], lw['ssd_conv_b'], lw['ssd_dt_bias'], lw['ssd_a_log'], lw['ssd_d'])
    zero_state = jnp.zeros((BATCH, 2, SSD_PAIRS, SSD_STATE, SSD_PAIR_W), F32)
    ctx_blocks = CTX_LEN // MIX_TB
    lat_blocks = SEQ // MIX_TB
    *y_ctx, state = _ssd_sweep(p_ctx_src, ctx_cols[1], dt_ctx, 0, ctx_blocks, zero_state, prm, rowlen=CTX_LEN)
    *y_lat, _ = _ssd_sweep(p_lat_src, lat_cols[1], dt_lat, lat_row_block0, lat_blocks, state, prm, rowlen=GRID_W)
    m_lat = _local_mixers_call(p_lat_src, lat_cols[0], lat_row_block0, BATCH * lat_blocks, y_lat[0], y_lat[1], lw,
                               rowlen=GRID_W)
    m_ctx = None
    if ctx_out:
        m_ctx = _local_mixers_call(p_ctx_src, ctx_cols[0], 0, BATCH * ctx_blocks, y_ctx[0], y_ctx[1], lw,
                                   rowlen=CTX_LEN)
    return m_lat, m_ctx


def kernel(x, c, ctx, c_ctx, w_mod, b_mod, norm1_g, norm2_g, w_in, b_in, sc_conv_w, cf_conv_w, cf_conv_b,
           cf_ln_g, cf_ln_b, sg_ln_g, sg_ln_b, sg_w, sg_b, ssd_conv_w, ssd_conv_b, ssd_dt_bias, ssd_a_log,
           ssd_d, ssd_norm_g, w_out, b_out, w_router, b_router, w_gate_up, b_gate_up, w_down, b_down,
           final_norm_g):
    cc = jnp.concatenate([c_ctx[None, :], c, jnp.zeros((MOD_ROWS - 1 - BATCH, D_MODEL), F32)], axis=0)
    mod_all = _modulation(cc, w_mod, b_mod).reshape(DEPTH, MOD_ROWS, 6, D_MODEL)

    x_all = jnp.concatenate([ctx.reshape(N_CTX, D_MODEL), x.reshape(N_LAT, D_MODEL)], axis=0)

    for i in range(DEPTH):
        last = i == DEPTH - 1
        lw = dict(sc_conv_w=sc_conv_w[i], cf_conv_w=cf_conv_w[i], cf_conv_b=cf_conv_b[i], cf_ln_g=cf_ln_g[i],
                  cf_ln_b=cf_ln_b[i], sg_ln_g=sg_ln_g[i], sg_ln_b=sg_ln_b[i], sg_w=sg_w[i], sg_b=sg_b[i],
                  ssd_conv_w=ssd_conv_w[i], ssd_conv_b=ssd_conv_b[i], ssd_dt_bias=ssd_dt_bias[i],
                  ssd_a_log=ssd_a_log[i], ssd_d=ssd_d[i], ssd_norm_g=ssd_norm_g[i])
        mod = mod_all[i]
        w_in_bf = w_in[i, :, :MAIN_COLS].astype(BF16)
        b_in_main = b_in[i, :MAIN_COLS].reshape(1, MAIN_COLS)
        wdt_bf = jnp.pad(w_in[i, :, MAIN_COLS:], ((0, 0), (0, LANE - DT_COLS))).astype(BF16)
        bdt = jnp.pad(b_in[i, MAIN_COLS:], (0, LANE - DT_COLS)).reshape(1, LANE)
        w_out_bf = w_out[i].astype(BF16)
        wr_bf = jnp.pad(w_router[i], ((0, 0), (0, LANE - N_EXPERTS))).astype(BF16)
        br = jnp.pad(b_router[i], (0, LANE - N_EXPERTS)).reshape(1, LANE)
        in_ctx_tiles = N_CTX // IN_TM
        proj = functools.partial(_in_projection, x_all, norm1_g[i], mod, w_in_bf, b_in_main, wdt_bf, bdt,
                                 ctx_tiles=in_ctx_tiles)

        z_xbc_cols = (OFF_SSD // GROUP_W, (OFF_SSD + GROUP_W) // SSD_XBC)
        if not last:
            p_all, dt_all = proj(row_tile0=0, n_row_tiles=(N_CTX + N_LAT) // IN_TM,
                                 col_tile0=0, n_col_tiles=MAIN_COLS // IN_TN)
            m_lat, m_ctx = _token_mixers(p_all, z_xbc_cols, p_all, z_xbc_cols, dt_all, dt_all, N_CTX // MIX_TB, lw, True)
        else:
            p_lat, dt_lat = proj(row_tile0=in_ctx_tiles, n_row_tiles=N_LAT // IN_TM,
                                 col_tile0=0, n_col_tiles=MAIN_COLS // IN_TN)
            ctx_col0 = MAIN_COLS - 2 * SSD_XBC
            p_ctx, dt_ctx = proj(row_tile0=0, n_row_tiles=in_ctx_tiles,
                                 col_tile0=ctx_col0 // IN_TN, n_col_tiles=(MAIN_COLS - ctx_col0) // IN_TN)
            ctx_cols = ((OFF_SSD - ctx_col0) // GROUP_W, (OFF_SSD + GROUP_W - ctx_col0) // SSD_XBC)
            m_lat, m_ctx = _token_mixers(p_ctx, ctx_cols, p_lat, z_xbc_cols, dt_ctx, dt_lat, 0, lw, False)
        out_ctx_tiles = N_CTX // OUT_TM
        if not last:
            m_all = jnp.concatenate([m_ctx, m_lat], axis=0)
            row_tile0 = 0
        else:
            m_all = m_lat
            row_tile0 = out_ctx_tiles
        x_mid, f_packed, logits = _out_projection(m_all, x_all, mod, norm2_g[i], w_out_bf,
                                                  b_out[i].reshape(1, D_MODEL), wr_bf, br,
                                                  row_tile0=row_tile0, ctx_tiles=out_ctx_tiles)
        yb, dest, gates = _moe_ffn(i, f_packed, logits[:, :N_EXPERTS], w_gate_up, b_gate_up, w_down, b_down)
        x_all = _combine(yb, dest, gates, x_mid, mod, final_norm_g if last else None,
                         row_tile0=row_tile0 * OUT_TM // CMB_TM, ctx_tiles=N_CTX // CMB_TM)

    return x_all.reshape(BATCH, SEQ, D_MODEL)
```

```python
import functools

import jax
import jax.numpy as jnp
from jax import lax
from jax.experimental import pallas as pl
from jax.experimental.pallas import tpu as pltpu

F32 = jnp.float32
BF16 = jnp.bfloat16

D_MODEL = 2048
BATCH = 4
SEQ = 2048
DEPTH = 2
GRID_W = 64
CTX_LEN = 256
EPS = 1e-6
GROUP_W = 512
SG_HEADS = 4
SG_CHUNK = 128
SG_HEAD_DIM = 128
SSD_HEAD_DIM = 64
SSD_HEADS = 8
SSD_GROUPS = 2
SSD_STATE = 128
SSD_CHUNK = 128
SSD_XBC = 1024
N_EXPERTS = 32
TOP_K = 4
D_FF = 2048
SWIGLU_LIMIT = 7.0
SWIGLU_ALPHA = 1.702
OFF_CF = 1536
OFF_SG = 2560
OFF_SSD = 3584
MAIN_COLS = 5120
DT_COLS = 2 * SSD_HEADS
LANE = 128
HALF_D = D_MODEL // 2

N_CTX = BATCH * CTX_LEN
N_LAT = BATCH * SEQ

VMEM_LIMIT = 56 * 1024 * 1024

MOD_ROWS = 8
MOD_TN = 1024


def _mod_kernel(c_ref, w_ref, b_ref, o_ref):
    c = c_ref[...]
    s = c * jax.nn.sigmoid(c)
    o_ref[...] = jnp.dot(s.astype(BF16), w_ref[...].astype(BF16), preferred_element_type=F32) + b_ref[...]


def _modulation(cc, w_mod, b_mod):
    n_out = 6 * D_MODEL
    return pl.pallas_call(
        _mod_kernel,
        grid=(DEPTH, n_out // MOD_TN),
        in_specs=[
            pl.BlockSpec((MOD_ROWS, D_MODEL), lambda l, n: (0, 0)),
            pl.BlockSpec((None, D_MODEL, MOD_TN), lambda l, n: (l, 0, n)),
            pl.BlockSpec((None, 1, MOD_TN), lambda l, n: (l, 0, n)),
        ],
        out_specs=pl.BlockSpec((None, MOD_ROWS, MOD_TN), lambda l, n: (l, 0, n)),
        out_shape=jax.ShapeDtypeStruct((DEPTH, MOD_ROWS, n_out), F32),
        compiler_params=pltpu.CompilerParams(
            dimension_semantics=("arbitrary", "arbitrary"), vmem_limit_bytes=VMEM_LIMIT),
        name="adaln_mod",
    )(cc, w_mod, b_mod.reshape(DEPTH, 1, n_out))


def _mod_row(tile, tile_rows, ctx_tiles):
    tiles_per_batch = SEQ // tile_rows
    return jnp.where(tile < ctx_tiles, 0, 1 + (tile - ctx_tiles) // tiles_per_batch)


IN_TM = 1024
IN_TN = 1024
IN_PRO_ROWS = 256


def _inproj_kernel(x_ref, g_ref, mod_ref, w_ref, b_ref, wdt_ref, bdt_ref, o_ref, odt_ref, h_ref):
    @pl.when(pl.program_id(1) == 0)
    def _():
        g = g_ref[...]
        scale = 1.0 + mod_ref[1:2, :]
        shift = mod_ref[0:1, :]
        for r in range(IN_TM // IN_PRO_ROWS):
            rows = slice(r * IN_PRO_ROWS, (r + 1) * IN_PRO_ROWS)
            x = x_ref[rows, :]
            y = x * lax.rsqrt(jnp.mean(x * x, axis=-1, keepdims=True) + EPS)
            h_ref[rows, :] = ((y * g) * scale + shift).astype(BF16)
        odt_ref[...] = jnp.dot(h_ref[...], wdt_ref[...], preferred_element_type=F32) + bdt_ref[...]

    o_ref[...] = jnp.dot(h_ref[...], w_ref[...], preferred_element_type=F32) + b_ref[...]


def _in_projection(x_all, norm_g, mod, w_bf, b, wdt_bf, bdt, *, row_tile0, n_row_tiles, col_tile0, n_col_tiles,
                   ctx_tiles):
    rows = n_row_tiles * IN_TM
    return pl.pallas_call(
        _inproj_kernel,
        grid=(n_row_tiles, n_col_tiles),
        in_specs=[
            pl.BlockSpec((IN_TM, D_MODEL), lambda m, n: (m + row_tile0, 0)),
            pl.BlockSpec((1, D_MODEL), lambda m, n: (0, 0)),
            pl.BlockSpec((None, 6, D_MODEL), lambda m, n: (_mod_row(m + row_tile0, IN_TM, ctx_tiles), 0, 0)),
            pl.BlockSpec((D_MODEL, IN_TN), lambda m, n: (0, n + col_tile0)),
            pl.BlockSpec((1, IN_TN), lambda m, n: (0, n + col_tile0)),
            pl.BlockSpec((D_MODEL, LANE), lambda m, n: (0, 0)),
            pl.BlockSpec((1, LANE), lambda m, n: (0, 0)),
        ],
        out_specs=[
            pl.BlockSpec((IN_TM, IN_TN), lambda m, n: (m, n)),
            pl.BlockSpec((IN_TM, LANE), lambda m, n: (m, 0)),
        ],
        out_shape=[
            jax.ShapeDtypeStruct((rows, n_col_tiles * IN_TN), F32),
            jax.ShapeDtypeStruct((rows, LANE), F32),
        ],
        scratch_shapes=[pltpu.VMEM((IN_TM, D_MODEL), BF16)],
        compiler_params=pltpu.CompilerParams(
            dimension_semantics=("arbitrary", "arbitrary"), vmem_limit_bytes=VMEM_LIMIT),
        name="in_proj",
    )(x_all, norm_g.reshape(1, D_MODEL), mod, w_bf, b, wdt_bf, bdt)


OUT_TM = 256


def _outproj_kernel(*refs, ctx_tiles):
    if ctx_tiles:
        mc_ref, m_ref, x_ref, mod_ref, g_ref, w_ref, b_ref, wr_ref, br_ref, xo_ref, f_ref, lg_ref = refs
        m = jnp.where(pl.program_id(0) < ctx_tiles, mc_ref[...], m_ref[...])
    else:
        m_ref, x_ref, mod_ref, g_ref, w_ref, b_ref, wr_ref, br_ref, xo_ref, f_ref, lg_ref = refs
        m = m_ref[...]
    y = jnp.dot(m.astype(BF16), w_ref[...], preferred_element_type=F32) + b_ref[...]
    xn = x_ref[...] + mod_ref[2:3, :] * y
    xo_ref[...] = xn
    r = lax.rsqrt(jnp.mean(xn * xn, axis=-1, keepdims=True) + EPS)
    f = ((xn * r) * g_ref[...]) * (1.0 + mod_ref[4:5, :]) + mod_ref[3:4, :]
    fb = f.astype(BF16)
    bits = lax.bitcast_convert_type(fb.astype(F32), jnp.uint32)
    f_ref[...] = (bits[:, HALF_D:] & jnp.uint32(0xFFFF0000)) | (bits[:, :HALF_D] >> 16)
    lg_ref[...] = jnp.dot(fb, wr_ref[...], preferred_element_type=F32) + br_ref[...]


def _out_projection(m_ctx, m, x_all, mod, norm_g, w_bf, b, wr_bf, br, *, row_tile0, ctx_tiles):
    m_ctx_tiles = 0 if m_ctx is None else m_ctx.shape[0] // OUT_TM
    rows = m.shape[0] + m_ctx_tiles * OUT_TM
    n_tiles = rows // OUT_TM
    m_specs = [pl.BlockSpec((OUT_TM, D_MODEL), lambda t: (jnp.maximum(t - m_ctx_tiles, 0), 0))]
    m_args = [m]
    if m_ctx is not None:
        m_specs.insert(0, pl.BlockSpec((OUT_TM, D_MODEL), lambda t: (jnp.minimum(t, m_ctx_tiles - 1), 0)))
        m_args.insert(0, m_ctx)
    return pl.pallas_call(
        functools.partial(_outproj_kernel, ctx_tiles=m_ctx_tiles),
        grid=(n_tiles,),
        in_specs=m_specs + [
            pl.BlockSpec((OUT_TM, D_MODEL), lambda t: (t + row_tile0, 0)),
            pl.BlockSpec((None, 6, D_MODEL), lambda t: (_mod_row(t + row_tile0, OUT_TM, ctx_tiles), 0, 0)),
            pl.BlockSpec((1, D_MODEL), lambda t: (0, 0)),
            pl.BlockSpec((D_MODEL, D_MODEL), lambda t: (0, 0)),
            pl.BlockSpec((1, D_MODEL), lambda t: (0, 0)),
            pl.BlockSpec((D_MODEL, LANE), lambda t: (0, 0)),
            pl.BlockSpec((1, LANE), lambda t: (0, 0)),
        ],
        out_specs=[
            pl.BlockSpec((OUT_TM, D_MODEL), lambda t: (t, 0)),
            pl.BlockSpec((OUT_TM, HALF_D), lambda t: (t, 0)),
            pl.BlockSpec((OUT_TM, LANE), lambda t: (t, 0)),
        ],
        out_shape=[
            jax.ShapeDtypeStruct((rows, D_MODEL), F32),
            jax.ShapeDtypeStruct((rows, HALF_D), jnp.uint32),
            jax.ShapeDtypeStruct((rows, LANE), F32),
        ],
        compiler_params=pltpu.CompilerParams(
            dimension_semantics=("arbitrary",), vmem_limit_bytes=VMEM_LIMIT),
        name="out_proj",
    )(*m_args, x_all, mod, norm_g.reshape(1, D_MODEL), w_bf, b, wr_bf, br)


MOE_UNIT = 128
MOE_SUB = 2 * MOE_UNIT
MOE_CHUNK = 2048
MOE_NSUB = MOE_CHUNK // MOE_SUB
MOE_PIECES = (8, 4, 2, 1)
MOE_TF = 256
MOE_TN = 512
MOE_F_STEPS = D_FF // MOE_TF
MOE_N_STEPS = D_MODEL // MOE_TN
MOE_STEPS = MOE_F_STEPS + MOE_N_STEPS


def _moe_rows(n_tokens):
    slots = n_tokens * TOP_K
    padded = slots + N_EXPERTS * (MOE_UNIT - 1)
    padded = -(-padded // MOE_UNIT) * MOE_UNIT
    return padded


def _moe_chunks(n_tokens):
    return _moe_rows(n_tokens) // MOE_CHUNK + N_EXPERTS


def _row_copy_wait(src_hbm, dst, sem, rows):
    pltpu.make_async_copy(src_hbm.at[pl.ds(0, rows)], dst, sem).wait()


def _moe_kernel(ce_ref, cs_ref, cn_ref, tail_ref, *refs):
    (tok_ref, tok2_ref, f_hbm, wg_ref, wu_ref, bg_ref, bu_ref, wd_ref, bd_ref, yb_hbm, xw, xs, hs, os_, sem_in,
     sem_out) = refs
    c = pl.program_id(0)
    j = pl.program_id(1)
    cur = jnp.maximum(c - 1, 0)
    units = jnp.where(c >= 1, cn_ref[cur], 0)
    nsub = (units + 1) // 2
    start = pl.multiple_of(cs_ref[cur], MOE_UNIT)

    nxt = jnp.minimum(c, cn_ref.shape[0] - 1)
    units_next = jnp.where(c < pl.num_programs(0) - 1, cn_ref[nxt], 0)
    fetch = jnp.logical_and(j < MOE_F_STEPS, 2 * j < units_next)

    def gather_issue():
        base = pl.multiple_of(j * MOE_SUB, MOE_SUB)
        for half, toks in enumerate((tok_ref, tok2_ref)):
            for r in range(MOE_UNIT):
                pltpu.make_async_copy(f_hbm.at[pl.ds(toks[0, r], 1)], xw.at[pl.ds(base + half * MOE_UNIT + r, 1)],
                                      sem_in.at[j]).start()

    def gather_wait(i):
        _row_copy_wait(f_hbm, xw.at[pl.ds(i * MOE_SUB, MOE_SUB)], sem_in.at[i], MOE_SUB)

    def unpack(i):
        rows = slice(i * MOE_SUB, (i + 1) * MOE_SUB)
        w = xw[rows, :]
        xs[rows, :HALF_D] = lax.bitcast_convert_type(w << 16, F32).astype(BF16)
        xs[rows, HALF_D:] = lax.bitcast_convert_type(w & jnp.uint32(0xFFFF0000), F32).astype(BF16)

    def copy_out(slot, off, n, col):
        return pltpu.make_async_copy(
            os_.at[slot, pl.ds(off, n * MOE_UNIT), :],
            yb_hbm.at[pl.ds(start + off, n * MOE_UNIT), pl.ds(col, MOE_TN)],
            sem_out.at[slot])

    def for_valid_subs(fn):
        fn(0)
        for i in range(1, MOE_NSUB):
            @pl.when(i < nsub)
            def _(i=i):
                fn(i)

    big = MOE_PIECES[0]

    def for_pieces(fn):
        n_big = units // big
        rem = units - big * n_big
        for q in range(MOE_CHUNK // (big * MOE_UNIT)):
            @pl.when(q < n_big)
            def _(q=q):
                fn(q * big * MOE_UNIT, big, q == 0)
        base = n_big * (big * MOE_UNIT)
        for size in MOE_PIECES[1:]:
            above = (rem // (2 * size)) * (2 * size)

            @pl.when((rem // size) % 2 == 1)
            def _(size=size, above=above):
                fn(pl.multiple_of(base + above * MOE_UNIT, MOE_UNIT), size, False)

    @pl.when(jnp.logical_and(c == 0, j == MOE_F_STEPS))
    def _():
        os_[0, 0:MOE_UNIT, :] = jnp.zeros((MOE_UNIT, MOE_TN), F32)
        tail_start = pl.multiple_of(tail_ref[0], MOE_UNIT)

        def tail_copy(u, t):
            return pltpu.make_async_copy(
                os_.at[0, pl.ds(0, MOE_UNIT), :],
                yb_hbm.at[pl.ds(tail_start + u * MOE_UNIT, MOE_UNIT), pl.ds(t * MOE_TN, MOE_TN)],
                sem_out.at[0])

        for action in ("start", "wait"):
            for u in range(N_EXPERTS):
                @pl.when(u < tail_ref[1])
                def _(u=u, action=action):
                    for t in range(MOE_N_STEPS):
                        getattr(tail_copy(u, t), action)()

    @pl.when(jnp.logical_and(nsub > 0, j == 0))
    def _():
        def land(i):
            gather_wait(i)
            unpack(i)

        for_valid_subs(land)

    @pl.when(jnp.logical_and(fetch, units < big))
    def _():
        gather_issue()

    @pl.when(nsub > 0)
    def _():
        @pl.when(j < MOE_F_STEPS)
        def _():
            bg = bg_ref[...]
            bu = bu_ref[...]

            def gate_up_piece(off, n):
                rows = pl.ds(off, n * MOE_UNIT)
                x = xs[rows, :]
                g = jnp.dot(x, wg_ref[...].astype(BF16), preferred_element_type=F32) + bg
                u = jnp.dot(x, wu_ref[...].astype(BF16), preferred_element_type=F32) + bu
                g = jnp.minimum(g, SWIGLU_LIMIT)
                u = jnp.clip(u, -SWIGLU_LIMIT, SWIGLU_LIMIT)
                h = (u + 1.0) * (g * jax.nn.sigmoid(SWIGLU_ALPHA * g))
                hs[j, rows, :] = h.astype(BF16)

            def gate_up(off, n, first_big):
                if not first_big:
                    gate_up_piece(off, n)
                    return

                @pl.when(fetch)
                def _():
                    gather_issue()
                    gate_up_piece(off, n)

                @pl.when(jnp.logical_not(fetch))
                def _():
                    gate_up_piece(off, n)

            for_pieces(gate_up)

        @pl.when(j >= MOE_F_STEPS)
        def _():
            jn = j - MOE_F_STEPS
            slot = jn % 2
            col = pl.multiple_of(jn * MOE_TN, MOE_TN)
            bd = bd_ref[...]

            @pl.when(jn >= 2)
            def _():
                for_pieces(lambda off, n, _: copy_out(slot, off, n, col).wait())

            def down(off, n, _):
                rows = pl.ds(off, n * MOE_UNIT)
                h = jnp.concatenate([hs[f, rows, :] for f in range(MOE_F_STEPS)], axis=1)
                os_[slot, rows, :] = jnp.dot(h, wd_ref[...].astype(BF16), preferred_element_type=F32) + bd
                copy_out(slot, off, n, col).start()

            for_pieces(down)

            @pl.when(jn == MOE_N_STEPS - 1)
            def _():
                for_pieces(lambda off, n, _: copy_out(1 - slot, off, n, col).wait())
                for_pieces(lambda off, n, _: copy_out(slot, off, n, col).wait())


def _moe_experts(layer, chunk_e, chunk_start, chunk_nsub, tail, used_chunks, buf_tok, f_packed, w_gate_up, b_gate_up,
                 w_down, b_down):
    n_chunks = chunk_e.shape[0]
    rows = buf_tok.shape[0]
    n_units = rows // MOE_UNIT

    def gu_idx(half):
        def idx(c, j, ce, cs, cn, tl):
            cur = jnp.maximum(c - 1, 0)
            jj = jnp.where(cn[cur] > 0, jnp.minimum(j, MOE_F_STEPS - 1), MOE_F_STEPS - 1)
            jj = jnp.where(c == 0, 0, jj)
            return (layer, ce[cur], 0, half * MOE_F_STEPS + jj)
        return idx

    def d_idx(c, j, ce, cs, cn, tl):
        cur = jnp.maximum(c - 1, 0)
        jj = jnp.where(cn[cur] > 0, jnp.maximum(j - MOE_F_STEPS, 0), MOE_N_STEPS - 1)
        jj = jnp.where(c == 0, 0, jj)
        return (layer, ce[cur], 0, jj)

    def tok_idx(half):
        def idx(c, j, ce, cs, cn, tl):
            nxt = jnp.minimum(c, n_chunks - 1)
            unit = cs[nxt] // MOE_UNIT + 2 * jnp.minimum(j, MOE_F_STEPS - 1) + half
            return (jnp.minimum(unit, n_units - 1), 0, 0)
        return idx

    grid_spec = pltpu.PrefetchScalarGridSpec(
        num_scalar_prefetch=4,
        grid=(used_chunks + 1, MOE_STEPS),
        in_specs=[
            pl.BlockSpec((None, 1, MOE_UNIT), tok_idx(0), memory_space=pltpu.SMEM),
            pl.BlockSpec((None, 1, MOE_UNIT), tok_idx(1), memory_space=pltpu.SMEM),
            pl.BlockSpec(memory_space=pl.ANY),
            pl.BlockSpec((None, None, D_MODEL, MOE_TF), gu_idx(0)),
            pl.BlockSpec((None, None, D_MODEL, MOE_TF), gu_idx(1)),
            pl.BlockSpec((None, None, 1, MOE_TF), gu_idx(0)),
            pl.BlockSpec((None, None, 1, MOE_TF), gu_idx(1)),
            pl.BlockSpec((None, None, D_FF, MOE_TN), d_idx),
            pl.BlockSpec((None, None, 1, MOE_TN), d_idx),
        ],
        out_specs=pl.BlockSpec(memory_space=pl.ANY),
        scratch_shapes=[
            pltpu.VMEM((MOE_CHUNK, HALF_D), jnp.uint32),
            pltpu.VMEM((MOE_CHUNK, D_MODEL), BF16),
            pltpu.VMEM((MOE_F_STEPS, MOE_CHUNK, MOE_TF), BF16),
            pltpu.VMEM((2, MOE_CHUNK, MOE_TN), F32),
            pltpu.SemaphoreType.DMA((MOE_NSUB,)),
            pltpu.SemaphoreType.DMA((2,)),
        ],
    )
    bgu = b_gate_up.reshape(DEPTH, N_EXPERTS, 1, 2 * D_FF)
    bd = b_down.reshape(DEPTH, N_EXPERTS, 1, D_MODEL)
    return pl.pallas_call(
        _moe_kernel,
        grid_spec=grid_spec,
        out_shape=jax.ShapeDtypeStruct((rows, D_MODEL), F32),
        compiler_params=pltpu.CompilerParams(
            dimension_semantics=("arbitrary", "arbitrary"), vmem_limit_bytes=VMEM_LIMIT),
        name="moe_experts",
    )(chunk_e, chunk_start, chunk_nsub, tail, *([buf_tok.reshape(n_units, 1, MOE_UNIT)] * 2), f_packed,
      w_gate_up, w_gate_up, bgu, bgu, w_down, bd)


CMB_TM = 256


def _combine_kernel(idx_ref, yb_hbm, gates_ref, x_ref, mod_ref, *rest, final):
    if final:
        g_ref, o_ref, buf, sem = rest
    else:
        o_ref, buf, sem = rest

    for k in range(TOP_K):
        def issue(r, carry, k=k):
            pltpu.make_async_copy(yb_hbm.at[pl.ds(idx_ref[0, k * CMB_TM + r], 1)], buf.at[k, pl.ds(r, 1)],
                                  sem.at[k]).start()
            return carry

        lax.fori_loop(0, CMB_TM, issue, 0, unroll=8)

    gates = gates_ref[...]
    y = None
    for k in range(TOP_K):
        _row_copy_wait(yb_hbm, buf.at[k], sem.at[k], CMB_TM)
        t = gates[:, k:k + 1] * buf[k]
        y = t if y is None else y + t
    xn = x_ref[...] + mod_ref[5:6, :] * y
    if final:
        xn = xn * lax.rsqrt(jnp.mean(xn * xn, axis=-1, keepdims=True) + EPS) * g_ref[...]
    o_ref[...] = xn


def _combine(yb, dest, gates, x_mid, mod, final_g, *, row_tile0, ctx_tiles):
    n = x_mid.shape[0]
    n_tiles = n // CMB_TM
    final = final_g is not None
    idx = dest.reshape(n_tiles, CMB_TM, TOP_K).transpose(0, 2, 1).reshape(n_tiles, 1, TOP_K * CMB_TM)
    in_specs = [
        pl.BlockSpec((None, 1, TOP_K * CMB_TM), lambda t: (t, 0, 0), memory_space=pltpu.SMEM),
        pl.BlockSpec(memory_space=pl.ANY),
        pl.BlockSpec((CMB_TM, TOP_K), lambda t: (t, 0)),
        pl.BlockSpec((CMB_TM, D_MODEL), lambda t: (t, 0)),
        pl.BlockSpec((None, 6, D_MODEL), lambda t: (_mod_row(t + row_tile0, CMB_TM, ctx_tiles), 0, 0)),
    ]
    args = [idx, yb, gates, x_mid, mod]
    if final:
        in_specs.append(pl.BlockSpec((1, D_MODEL), lambda t: (0, 0)))
        args.append(final_g.reshape(1, D_MODEL))
    return pl.pallas_call(
        functools.partial(_combine_kernel, final=final),
        grid=(n_tiles,),
        in_specs=in_specs,
        out_specs=pl.BlockSpec((CMB_TM, D_MODEL), lambda t: (t, 0)),
        out_shape=jax.ShapeDtypeStruct((n, D_MODEL), F32),
        scratch_shapes=[pltpu.VMEM((TOP_K, CMB_TM, D_MODEL), F32), pltpu.SemaphoreType.DMA((TOP_K,))],
        compiler_params=pltpu.CompilerParams(dimension_semantics=("arbitrary",), vmem_limit_bytes=VMEM_LIMIT),
        name="moe_combine",
    )(*args)


def _moe_ffn(layer, f_packed, logits, w_gate_up, b_gate_up, w_down, b_down):
    n = f_packed.shape[0]
    slots = n * TOP_K
    buf_rows = _moe_rows(n)
    n_chunks = _moe_chunks(n)

    top_logit, top_e = lax.top_k(logits, TOP_K)
    gates = jax.nn.softmax(top_logit, axis=-1)
    flat_e = top_e.reshape(-1).astype(jnp.int32)
    onehot = (flat_e[:, None] == jnp.arange(N_EXPERTS, dtype=jnp.int32)[None, :]).astype(jnp.int32)
    running = jnp.cumsum(onehot, axis=0)
    counts = running[-1]
    padded = (counts + MOE_UNIT - 1) // MOE_UNIT * MOE_UNIT
    pad_end = jnp.cumsum(padded)
    pad_start = pad_end - padded
    dest_of_slot = jnp.sum(onehot * (running - 1 + pad_start[None, :]), axis=1)
    buf_tok = jnp.zeros((buf_rows,), jnp.int32).at[dest_of_slot].set(
        jnp.arange(slots, dtype=jnp.int32) // TOP_K, unique_indices=True, mode='promise_in_bounds')

    e_chunks = (padded + MOE_CHUNK - 1) // MOE_CHUNK
    chunk_end = jnp.cumsum(e_chunks)
    total_chunks = chunk_end[-1]
    cidx = jnp.arange(n_chunks, dtype=jnp.int32)
    ce = jnp.minimum(jnp.searchsorted(chunk_end, cidx, side='right'), N_EXPERTS - 1).astype(jnp.int32)
    local = cidx - (chunk_end[ce] - e_chunks[ce])
    valid = cidx < total_chunks
    c_start = jnp.where(valid, pad_start[ce] + local * MOE_CHUNK, 0).astype(jnp.int32)
    c_nsub = jnp.where(valid, jnp.minimum(MOE_CHUNK, padded[ce] - local * MOE_CHUNK) // MOE_UNIT, 0).astype(jnp.int32)
    last_e = ce[jnp.maximum(total_chunks - 1, 0)]
    ce = jnp.where(valid, ce, last_e).astype(jnp.int32)

    tail = jnp.stack([pad_end[-1], (buf_rows - pad_end[-1]) // MOE_UNIT]).astype(jnp.int32)
    yb = _moe_experts(layer, ce, c_start, c_nsub, tail, total_chunks.astype(jnp.int32), buf_tok, f_packed,
                      w_gate_up, b_gate_up, w_down, b_down)
    return yb, dest_of_slot.reshape(n, TOP_K), gates


MIX_TB = 256
CONV_HALO = 16
CONV_PIECE = 64
NEG_INF = float("-inf")


def _conv_pitch(rowlen):
    return rowlen + 2 * CONV_HALO


def _dwconv_block(u, pad_ref, w_ref, rowlen):
    taps = w_ref.shape[0]
    half = taps // 2
    pitch = _conv_pitch(rowlen)
    ch = u.shape[1]
    zeros = jnp.zeros((CONV_HALO, ch), F32)
    for r in range(MIX_TB // rowlen):
        base = r * pitch
        pad_ref[base:base + CONV_HALO, :] = zeros
        pad_ref[base + CONV_HALO:base + CONV_HALO + rowlen, :] = u[r * rowlen:(r + 1) * rowlen, :]
        pad_ref[base + CONV_HALO + rowlen:base + pitch, :] = zeros
    outs = []
    for r in range(MIX_TB // rowlen):
        for piece in range(rowlen // CONV_PIECE):
            acc = None
            for j in range(taps):
                off = r * pitch + CONV_HALO + piece * CONV_PIECE + j - half
                term = pad_ref[off:off + CONV_PIECE, :] * w_ref[j:j + 1, :]
                acc = term if acc is None else acc + term
            outs.append(acc)
    return jnp.concatenate(outs, axis=0)


def _ln_rows(x, g, b):
    xc = x - jnp.mean(x, axis=-1, keepdims=True)
    return xc * lax.rsqrt(jnp.mean(xc * xc, axis=-1, keepdims=True) + EPS) * g + b


def _silu(x):
    return x * jax.nn.sigmoid(x)


SSD_PAIRS = SSD_HEADS // 2
SSD_PAIR_W = 2 * SSD_HEAD_DIM


def _ssd_kernel(xbc_f, dt_f, xbc_b, dt_b, init_ref, cw_ref, cb_ref, dtb_ref, a_ref, skip_ref, y_f, y_b, fin_ref,
                pad_f, pad_b, st_ref, *, rowlen):
    s = pl.program_id(1)

    @pl.when(s == 0)
    def _():
        st_ref[...] = init_ref[...]

    for d, (xbc_ref, dt_ref, y_ref, pad_ref) in enumerate(((xbc_f, dt_f, y_f, pad_f), (xbc_b, dt_b, y_b, pad_b))):
        _ssd_direction(xbc_ref, dt_ref, cw_ref, cb_ref, dtb_ref, a_ref, skip_ref[d:d + 1, :], y_ref, pad_ref,
                       st_ref.at[d], rowlen=rowlen, rev=d == 1, dcol=d * SSD_HEADS)

    @pl.when(s == pl.num_programs(1) - 1)
    def _():
        fin_ref[...] = st_ref[...]


def _ssd_direction(xbc_ref, dt_ref, cw_ref, cb_ref, dtb_ref, a_ref, skip_row, y_ref, pad_ref, st_ref, *, rowlen, rev,
                   dcol):
    xa = _silu(_dwconv_block(xbc_ref[...], pad_ref, cw_ref, rowlen) + cb_ref[...])
    dt_all = dt_ref[...] + dtb_ref[...]
    dt_all = jnp.maximum(dt_all, 0.0) + jnp.log1p(jnp.exp(-jnp.abs(dt_all)))
    da_all = dt_all * a_ref[...]

    row_i = lax.broadcasted_iota(jnp.int32, (SSD_CHUNK, SSD_CHUNK), 0)
    col_i = lax.broadcasted_iota(jnp.int32, (SSD_CHUNK, SSD_CHUNK), 1)
    tri = (col_i >= row_i) if rev else (col_i <= row_i)
    tri_f = tri.astype(F32)
    first_half = lax.broadcasted_iota(jnp.int32, (SSD_CHUNK, SSD_PAIR_W), 1) < SSD_HEAD_DIM
    first_half_row = first_half[0:1, :]

    chunks = range(MIX_TB // SSD_CHUNK)
    for ci in (reversed(chunks) if rev else chunks):
        rows = slice(ci * SSD_CHUNK, (ci + 1) * SSD_CHUNK)
        acc = jnp.dot(tri_f, da_all[rows, :], preferred_element_type=F32, precision=lax.Precision.HIGHEST)
        acc_t = acc.T
        tot = acc[0:1, :] if rev else acc[SSD_CHUNK - 1:SSD_CHUNK, :]
        to_end = jnp.exp(tot - acc)
        from_start = jnp.exp(acc)
        chunk_decay = jnp.exp(tot)
        dt_c = dt_all[rows, :]
        for g in range(SSD_GROUPS):
            bg = xa[rows, GROUP_W + g * SSD_STATE:GROUP_W + (g + 1) * SSD_STATE]
            cg = xa[rows, GROUP_W + (SSD_GROUPS + g) * SSD_STATE:GROUP_W + (SSD_GROUPS + g + 1) * SSD_STATE]
            scores = lax.dot_general(cg.astype(BF16), bg.astype(BF16), (((1,), (1,)), ((), ())),
                                     preferred_element_type=F32)
            pairs_per_group = SSD_PAIRS // SSD_GROUPS
            for p in range(g * pairs_per_group, (g + 1) * pairs_per_group):
                c0 = dcol + 2 * p
                c1 = c0 + 1
                lhs, bw = [], []
                for col in (c0, c1):
                    seg = acc[:, col:col + 1] - acc_t[col:col + 1, :]
                    lhs.append(scores * jnp.exp(jnp.where(tri, seg, NEG_INF)))
                for col in (c0, c1):
                    lhs.append(cg * from_start[:, col:col + 1])
                    bw.append((bg * to_end[:, col:col + 1]).T)
                xp = xa[rows, p * SSD_PAIR_W:(p + 1) * SSD_PAIR_W]
                xd = xp * jnp.where(first_half, dt_c[:, c0:c0 + 1], dt_c[:, c1:c1 + 1])
                xd_top = jnp.where(first_half, xd, 0.0).astype(BF16)
                xd_bot = jnp.where(first_half, 0.0, xd).astype(BF16)
                st = st_ref[p]
                st_top = jnp.where(first_half, st, 0.0).astype(BF16)
                st_bot = jnp.where(first_half, 0.0, st).astype(BF16)
                y = jnp.dot(jnp.concatenate(lhs, axis=1).astype(BF16),
                            jnp.concatenate([xd_top, xd_bot, st_top, st_bot], axis=0),
                            preferred_element_type=F32)
                y_ref[rows, p * SSD_PAIR_W:(p + 1) * SSD_PAIR_W] = y + skip_row[:, p * SSD_PAIR_W:(p + 1) * SSD_PAIR_W] * xp
                upd = jnp.dot(jnp.concatenate(bw, axis=1).astype(BF16), jnp.concatenate([xd_top, xd_bot], axis=0),
                              preferred_element_type=F32)
                decay_lane = jnp.where(first_half_row, chunk_decay[:, c0:c0 + 1], chunk_decay[:, c1:c1 + 1])
                st_ref[p] = st * decay_lane + upd


def _ssd_sweep(xbc_src, xbc_col_block, dt_src, row_block0, n_blocks, init, prm, *, rowlen):
    cw, cb, dtb_row, a_row, skip_rows = prm

    def fwd(b, s):
        return b * n_blocks + s

    def bwd(b, s):
        return b * n_blocks + (n_blocks - 1 - s)

    state_block = (None, 2, SSD_PAIRS, SSD_STATE, SSD_PAIR_W)
    y_shape = jax.ShapeDtypeStruct((BATCH * n_blocks * MIX_TB, GROUP_W), F32)
    pad_shape = pltpu.VMEM(((MIX_TB // rowlen) * _conv_pitch(rowlen), SSD_XBC), F32)
    return pl.pallas_call(
        functools.partial(_ssd_kernel, rowlen=rowlen),
        grid=(BATCH, n_blocks),
        in_specs=[
            pl.BlockSpec((MIX_TB, SSD_XBC), lambda b, s: (row_block0 + fwd(b, s), xbc_col_block)),
            pl.BlockSpec((MIX_TB, LANE), lambda b, s: (row_block0 + fwd(b, s), 0)),
            pl.BlockSpec((MIX_TB, SSD_XBC), lambda b, s: (row_block0 + bwd(b, s), xbc_col_block)),
            pl.BlockSpec((MIX_TB, LANE), lambda b, s: (row_block0 + bwd(b, s), 0)),
            pl.BlockSpec(state_block, lambda b, s: (b, 0, 0, 0, 0)),
            pl.BlockSpec((3, SSD_XBC), lambda b, s: (0, 0)),
            pl.BlockSpec((1, SSD_XBC), lambda b, s: (0, 0)),
            pl.BlockSpec((1, LANE), lambda b, s: (0, 0)),
            pl.BlockSpec((1, LANE), lambda b, s: (0, 0)),
            pl.BlockSpec((2, GROUP_W), lambda b, s: (0, 0)),
        ],
        out_specs=[
            pl.BlockSpec((MIX_TB, GROUP_W), lambda b, s: (fwd(b, s), 0)),
            pl.BlockSpec((MIX_TB, GROUP_W), lambda b, s: (bwd(b, s), 0)),
            pl.BlockSpec(state_block, lambda b, s: (b, 0, 0, 0, 0)),
        ],
        out_shape=[y_shape, y_shape, jax.ShapeDtypeStruct((BATCH, 2, SSD_PAIRS, SSD_STATE, SSD_PAIR_W), F32)],
        scratch_shapes=[pad_shape, pad_shape, pltpu.VMEM((2, SSD_PAIRS, SSD_STATE, SSD_PAIR_W), F32)],
        compiler_params=pltpu.CompilerParams(
            dimension_semantics=("arbitrary", "arbitrary"), vmem_limit_bytes=VMEM_LIMIT),
        name="ssd_sweep",
    )(xbc_src, dt_src, xbc_src, dt_src, init, cw, cb, dtb_row, a_row, skip_rows)


def _ssd_params(conv_w, conv_b, dt_bias, a_log, d_skip):
    pad = LANE - 2 * SSD_HEADS
    dtb_row = jnp.pad(dt_bias.reshape(-1), (0, pad)).reshape(1, LANE)
    a_row = jnp.pad(-jnp.exp(a_log.reshape(-1)), (0, pad)).reshape(1, LANE)
    skip_rows = jnp.repeat(d_skip, SSD_HEAD_DIM, axis=1)
    return conv_w, conv_b.reshape(1, SSD_XBC), dtb_row, a_row, skip_rows


def _local_kernel(p_ref, z_ref, y0_ref, y1_ref, scw_ref, cfw_ref, cfb_ref, cfg_ref, cfbeta_ref, sgg_ref,
                  sgbeta_ref, sgw_ref, sgb_ref, ng_ref, m_ref, pad_ref, *, rowlen):
    gate_b = p_ref[:, 0:GROUP_W]
    u = p_ref[:, GROUP_W:2 * GROUP_W] * p_ref[:, 2 * GROUP_W:3 * GROUP_W]
    m_ref[:, 0:GROUP_W] = gate_b * _dwconv_block(u, pad_ref, scw_ref, rowlen)
    u = p_ref[:, OFF_CF:OFF_CF + GROUP_W] * jax.nn.sigmoid(p_ref[:, OFF_CF + GROUP_W:OFF_SG])
    u = _dwconv_block(u, pad_ref, cfw_ref, rowlen) + cfb_ref[...]
    m_ref[:, GROUP_W:2 * GROUP_W] = _silu(_ln_rows(u, cfg_ref[...], cfbeta_ref[...]))
    q = p_ref[:, OFF_SG:OFF_SSD]
    q = 0.5 * q * (1.0 + lax.erf(q * (2.0 ** -0.5)))
    v = _ln_rows(q[:, GROUP_W:], sgg_ref[...], sgbeta_ref[...]).astype(BF16)
    n_chunks = MIX_TB // SG_CHUNK
    for h in range(SG_HEADS):
        cols = slice(h * SG_HEAD_DIM, (h + 1) * SG_HEAD_DIM)
        rhs = jnp.concatenate([v[c * SG_CHUNK:(c + 1) * SG_CHUNK, cols] for c in range(n_chunks)], axis=1)
        sres = jnp.dot(sgw_ref[h].astype(BF16), rhs, preferred_element_type=F32) + sgb_ref[:, h:h + 1]
        for c in range(n_chunks):
            rows = slice(c * SG_CHUNK, (c + 1) * SG_CHUNK)
            m_ref[rows, 2 * GROUP_W + h * SG_HEAD_DIM:2 * GROUP_W + (h + 1) * SG_HEAD_DIM] = (
                q[rows, cols] * sres[:, c * SG_HEAD_DIM:(c + 1) * SG_HEAD_DIM])
    yv = (y0_ref[...] + y1_ref[...]) * _silu(z_ref[...])
    gw = GROUP_W // SSD_GROUPS
    for g in range(SSD_GROUPS):
        vg = yv[:, g * gw:(g + 1) * gw]
        vg = vg * lax.rsqrt(jnp.mean(vg * vg, axis=-1, keepdims=True) + EPS)
        m_ref[:, 3 * GROUP_W + g * gw:3 * GROUP_W + (g + 1) * gw] = vg * ng_ref[:, g * gw:(g + 1) * gw]


def _local_mixers_call(p_src, z_col_block, row_block0, n_blocks, y0, y1, lw, *, rowlen):
    vec = lambda a: a.reshape(1, GROUP_W)
    args = [lw['sc_conv_w'], lw['cf_conv_w'], vec(lw['cf_conv_b']), vec(lw['cf_ln_g']), vec(lw['cf_ln_b']),
            vec(lw['sg_ln_g']), vec(lw['sg_ln_b']), lw['sg_w'], lw['sg_b'].T, vec(lw['ssd_norm_g'])]
    full = lambda a: pl.BlockSpec(a.shape, lambda t, nd=a.ndim: (0,) * nd)
    return pl.pallas_call(
        functools.partial(_local_kernel, rowlen=rowlen),
        grid=(n_blocks,),
        in_specs=[
            pl.BlockSpec((MIX_TB, OFF_SSD), lambda t: (t + row_block0, 0)),
            pl.BlockSpec((MIX_TB, GROUP_W), lambda t: (t + row_block0, z_col_block)),
            pl.BlockSpec((MIX_TB, GROUP_W), lambda t: (t, 0)),
            pl.BlockSpec((MIX_TB, GROUP_W), lambda t: (t, 0)),
        ] + [full(a) for a in args],
        out_specs=pl.BlockSpec((MIX_TB, D_MODEL), lambda t: (t, 0)),
        out_shape=jax.ShapeDtypeStruct((n_blocks * MIX_TB, D_MODEL), F32),
        scratch_shapes=[pltpu.VMEM(((MIX_TB // rowlen) * _conv_pitch(rowlen), GROUP_W), F32)],
        compiler_params=pltpu.CompilerParams(dimension_semantics=("arbitrary",), vmem_limit_bytes=VMEM_LIMIT),
        name="local_mixers",
    )(p_src, p_src, y0, y1, *args)


def _token_mixers(p_ctx_src, ctx_cols, p_lat_src, lat_cols, dt_ctx, dt_lat, lat_row_block0, lw, ctx_out):
    prm = _ssd_params(lw['ssd_conv_w'], lw['ssd_conv_b'], lw['ssd_dt_bias'], lw['ssd_a_log'], lw['ssd_d'])
    zero_state = jnp.zeros((BATCH, 2, SSD_PAIRS, SSD_STATE, SSD_PAIR_W), F32)
    ctx_blocks = CTX_LEN // MIX_TB
    lat_blocks = SEQ // MIX_TB
    *y_ctx, state = _ssd_sweep(p_ctx_src, ctx_cols[1], dt_ctx, 0, ctx_blocks, zero_state, prm, rowlen=CTX_LEN)
    *y_lat, _ = _ssd_sweep(p_lat_src, lat_cols[1], dt_lat, lat_row_block0, lat_blocks, state, prm, rowlen=GRID_W)
    m_lat = _local_mixers_call(p_lat_src, lat_cols[0], lat_row_block0, BATCH * lat_blocks, y_lat[0], y_lat[1], lw,
                               rowlen=GRID_W)
    m_ctx = None
    if ctx_out:
        m_ctx = _local_mixers_call(p_ctx_src, ctx_cols[0], 0, BATCH * ctx_blocks, y_ctx[0], y_ctx[1], lw,
                                   rowlen=CTX_LEN)
    return m_lat, m_ctx


def kernel(x, c, ctx, c_ctx, w_mod, b_mod, norm1_g, norm2_g, w_in, b_in, sc_conv_w, cf_conv_w, cf_conv_b,
           cf_ln_g, cf_ln_b, sg_ln_g, sg_ln_b, sg_w, sg_b, ssd_conv_w, ssd_conv_b, ssd_dt_bias, ssd_a_log,
           ssd_d, ssd_norm_g, w_out, b_out, w_router, b_router, w_gate_up, b_gate_up, w_down, b_down,
           final_norm_g):
    cc = jnp.concatenate([c_ctx[None, :], c, jnp.zeros((MOD_ROWS - 1 - BATCH, D_MODEL), F32)], axis=0)
    mod_all = _modulation(cc, w_mod, b_mod).reshape(DEPTH, MOD_ROWS, 6, D_MODEL)

    x_all = jnp.concatenate([ctx.reshape(N_CTX, D_MODEL), x.reshape(N_LAT, D_MODEL)], axis=0)

    for i in range(DEPTH):
        last = i == DEPTH - 1
        lw = dict(sc_conv_w=sc_conv_w[i], cf_conv_w=cf_conv_w[i], cf_conv_b=cf_conv_b[i], cf_ln_g=cf_ln_g[i],
                  cf_ln_b=cf_ln_b[i], sg_ln_g=sg_ln_g[i], sg_ln_b=sg_ln_b[i], sg_w=sg_w[i], sg_b=sg_b[i],
                  ssd_conv_w=ssd_conv_w[i], ssd_conv_b=ssd_conv_b[i], ssd_dt_bias=ssd_dt_bias[i],
                  ssd_a_log=ssd_a_log[i], ssd_d=ssd_d[i], ssd_norm_g=ssd_norm_g[i])
        mod = mod_all[i]
        w_in_bf = w_in[i, :, :MAIN_COLS].astype(BF16)
        b_in_main = b_in[i, :MAIN_COLS].reshape(1, MAIN_COLS)
        wdt_bf = jnp.pad(w_in[i, :, MAIN_COLS:], ((0, 0), (0, LANE - DT_COLS))).astype(BF16)
        bdt = jnp.pad(b_in[i, MAIN_COLS:], (0, LANE - DT_COLS)).reshape(1, LANE)
        w_out_bf = w_out[i].astype(BF16)
        wr_bf = jnp.pad(w_router[i], ((0, 0), (0, LANE - N_EXPERTS))).astype(BF16)
        br = jnp.pad(b_router[i], (0, LANE - N_EXPERTS)).reshape(1, LANE)
        in_ctx_tiles = N_CTX // IN_TM
        proj = functools.partial(_in_projection, x_all, norm1_g[i], mod, w_in_bf, b_in_main, wdt_bf, bdt,
                                 ctx_tiles=in_ctx_tiles)

        z_xbc_cols = (OFF_SSD // GROUP_W, (OFF_SSD + GROUP_W) // SSD_XBC)
        if not last:
            p_all, dt_all = proj(row_tile0=0, n_row_tiles=(N_CTX + N_LAT) // IN_TM,
                                 col_tile0=0, n_col_tiles=MAIN_COLS // IN_TN)
            m_lat, m_ctx = _token_mixers(p_all, z_xbc_cols, p_all, z_xbc_cols, dt_all, dt_all, N_CTX // MIX_TB, lw, True)
        else:
            p_lat, dt_lat = proj(row_tile0=in_ctx_tiles, n_row_tiles=N_LAT // IN_TM,
                                 col_tile0=0, n_col_tiles=MAIN_COLS // IN_TN)
            ctx_col0 = MAIN_COLS - 2 * SSD_XBC
            p_ctx, dt_ctx = proj(row_tile0=0, n_row_tiles=in_ctx_tiles,
                                 col_tile0=ctx_col0 // IN_TN, n_col_tiles=(MAIN_COLS - ctx_col0) // IN_TN)
            ctx_cols = ((OFF_SSD - ctx_col0) // GROUP_W, (OFF_SSD + GROUP_W - ctx_col0) // SSD_XBC)
            m_lat, m_ctx = _token_mixers(p_ctx, ctx_cols, p_lat, z_xbc_cols, dt_ctx, dt_lat, 0, lw, False)
        out_ctx_tiles = N_CTX // OUT_TM
        row_tile0 = out_ctx_tiles if last else 0
        x_mid, f_packed, logits = _out_projection(m_ctx, m_lat, x_all, mod, norm2_g[i], w_out_bf,
                                                  b_out[i].reshape(1, D_MODEL), wr_bf, br,
                                                  row_tile0=row_tile0, ctx_tiles=out_ctx_tiles)
        yb, dest, gates = _moe_ffn(i, f_packed, logits[:, :N_EXPERTS], w_gate_up, b_gate_up, w_down, b_down)
        x_all = _combine(yb, dest, gates, x_mid, mod, final_norm_g if last else None,
                         row_tile0=row_tile0 * OUT_TM // CMB_TM, ctx_tiles=N_CTX // CMB_TM)

    return x_all.reshape(BATCH, SEQ, D_MODEL)
```

```python
import functools

import jax
import jax.numpy as jnp
from jax import lax
from jax.experimental import pallas as pl
from jax.experimental.pallas import tpu as pltpu

F32 = jnp.float32
BF16 = jnp.bfloat16

D_MODEL = 2048
BATCH = 4
SEQ = 2048
DEPTH = 2
GRID_W = 64
CTX_LEN = 256
EPS = 1e-6
GROUP_W = 512
SG_HEADS = 4
SG_CHUNK = 128
SG_HEAD_DIM = 128
SSD_HEAD_DIM = 64
SSD_HEADS = 8
SSD_GROUPS = 2
SSD_STATE = 128
SSD_CHUNK = 128
SSD_XBC = 1024
N_EXPERTS = 32
TOP_K = 4
D_FF = 2048
SWIGLU_LIMIT = 7.0
SWIGLU_ALPHA = 1.702
OFF_CF = 1536
OFF_SG = 2560
OFF_SSD = 3584
MAIN_COLS = 5120
DT_COLS = 2 * SSD_HEADS
LANE = 128
HALF_D = D_MODEL // 2

N_CTX = BATCH * CTX_LEN
N_LAT = BATCH * SEQ

VMEM_LIMIT = 56 * 1024 * 1024

MOD_ROWS = 8
MOD_TN = 1024


def _mod_kernel(c_ref, w_ref, b_ref, o_ref):
    c = c_ref[...]
    s = c * jax.nn.sigmoid(c)
    o_ref[...] = jnp.dot(s.astype(BF16), w_ref[...].astype(BF16), preferred_element_type=F32) + b_ref[...]


def _modulation(cc, w_mod, b_mod):
    n_out = 6 * D_MODEL
    return pl.pallas_call(
        _mod_kernel,
        grid=(DEPTH, n_out // MOD_TN),
        in_specs=[
            pl.BlockSpec((MOD_ROWS, D_MODEL), lambda l, n: (0, 0)),
            pl.BlockSpec((None, D_MODEL, MOD_TN), lambda l, n: (l, 0, n)),
            pl.BlockSpec((None, 1, MOD_TN), lambda l, n: (l, 0, n)),
        ],
        out_specs=pl.BlockSpec((None, MOD_ROWS, MOD_TN), lambda l, n: (l, 0, n)),
        out_shape=jax.ShapeDtypeStruct((DEPTH, MOD_ROWS, n_out), F32),
        compiler_params=pltpu.CompilerParams(
            dimension_semantics=("arbitrary", "arbitrary"), vmem_limit_bytes=VMEM_LIMIT),
        name="adaln_mod",
    )(cc, w_mod, b_mod.reshape(DEPTH, 1, n_out))


def _mod_row(tile, tile_rows, ctx_tiles):
    tiles_per_batch = SEQ // tile_rows
    return jnp.where(tile < ctx_tiles, 0, 1 + (tile - ctx_tiles) // tiles_per_batch)


IN_TM = 1024
IN_TN = 1024
IN_PRO_ROWS = 256


def _inproj_kernel(x_ref, g_ref, mod_ref, w_ref, b_ref, wdt_ref, bdt_ref, o_ref, odt_ref, h_ref):
    @pl.when(pl.program_id(1) == 0)
    def _():
        g = g_ref[...]
        scale = 1.0 + mod_ref[1:2, :]
        shift = mod_ref[0:1, :]
        for r in range(IN_TM // IN_PRO_ROWS):
            rows = slice(r * IN_PRO_ROWS, (r + 1) * IN_PRO_ROWS)
            x = x_ref[rows, :]
            y = x * lax.rsqrt(jnp.mean(x * x, axis=-1, keepdims=True) + EPS)
            h_ref[rows, :] = ((y * g) * scale + shift).astype(BF16)
        odt_ref[...] = jnp.dot(h_ref[...], wdt_ref[...], preferred_element_type=F32) + bdt_ref[...]

    o_ref[...] = jnp.dot(h_ref[...], w_ref[...], preferred_element_type=F32) + b_ref[...]


def _in_projection(x_all, norm_g, mod, w_bf, b, wdt_bf, bdt, *, row_tile0, n_row_tiles, col_tile0, n_col_tiles,
                   ctx_tiles, mod_tile0=None):
    rows = n_row_tiles * IN_TM
    mod_tile0 = row_tile0 if mod_tile0 is None else mod_tile0
    return pl.pallas_call(
        _inproj_kernel,
        grid=(n_row_tiles, n_col_tiles),
        in_specs=[
            pl.BlockSpec((IN_TM, D_MODEL), lambda m, n: (m + row_tile0, 0)),
            pl.BlockSpec((1, D_MODEL), lambda m, n: (0, 0)),
            pl.BlockSpec((None, 6, D_MODEL), lambda m, n: (_mod_row(m + mod_tile0, IN_TM, ctx_tiles), 0, 0)),
            pl.BlockSpec((D_MODEL, IN_TN), lambda m, n: (0, n + col_tile0)),
            pl.BlockSpec((1, IN_TN), lambda m, n: (0, n + col_tile0)),
            pl.BlockSpec((D_MODEL, LANE), lambda m, n: (0, 0)),
            pl.BlockSpec((1, LANE), lambda m, n: (0, 0)),
        ],
        out_specs=[
            pl.BlockSpec((IN_TM, IN_TN), lambda m, n: (m, n)),
            pl.BlockSpec((IN_TM, LANE), lambda m, n: (m, 0)),
        ],
        out_shape=[
            jax.ShapeDtypeStruct((rows, n_col_tiles * IN_TN), F32),
            jax.ShapeDtypeStruct((rows, LANE), F32),
        ],
        scratch_shapes=[pltpu.VMEM((IN_TM, D_MODEL), BF16)],
        compiler_params=pltpu.CompilerParams(
            dimension_semantics=("arbitrary", "arbitrary"), vmem_limit_bytes=VMEM_LIMIT),
        name="in_proj",
    )(x_all, norm_g.reshape(1, D_MODEL), mod, w_bf, b, wdt_bf, bdt)


OUT_TM = 256


def _outproj_kernel(*refs, ctx_tiles):
    if ctx_tiles:
        mc_ref, m_ref, xc_ref, x_ref, mod_ref, g_ref, w_ref, b_ref, wr_ref, br_ref, xo_ref, f_ref, lg_ref = refs
        is_ctx = pl.program_id(0) < ctx_tiles
        m = jnp.where(is_ctx, mc_ref[...], m_ref[...])
        x = jnp.where(is_ctx, xc_ref[...], x_ref[...])
    else:
        m_ref, x_ref, mod_ref, g_ref, w_ref, b_ref, wr_ref, br_ref, xo_ref, f_ref, lg_ref = refs
        m = m_ref[...]
        x = x_ref[...]
    y = jnp.dot(m.astype(BF16), w_ref[...], preferred_element_type=F32) + b_ref[...]
    xn = x + mod_ref[2:3, :] * y
    xo_ref[...] = xn
    r = lax.rsqrt(jnp.mean(xn * xn, axis=-1, keepdims=True) + EPS)
    f = ((xn * r) * g_ref[...]) * (1.0 + mod_ref[4:5, :]) + mod_ref[3:4, :]
    fb = f.astype(BF16)
    bits = lax.bitcast_convert_type(fb.astype(F32), jnp.uint32)
    f_ref[...] = (bits[:, HALF_D:] & jnp.uint32(0xFFFF0000)) | (bits[:, :HALF_D] >> 16)
    lg_ref[...] = jnp.dot(fb, wr_ref[...], preferred_element_type=F32) + br_ref[...]


def _out_projection(m_ctx, m, x_ctx, x_all, mod, norm_g, w_bf, b, wr_bf, br, *, row_tile0, ctx_tiles):
    m_ctx_tiles = 0 if m_ctx is None else m_ctx.shape[0] // OUT_TM
    rows = m.shape[0] + m_ctx_tiles * OUT_TM
    n_tiles = rows // OUT_TM
    lat_spec = pl.BlockSpec((OUT_TM, D_MODEL), lambda t: (jnp.maximum(t - m_ctx_tiles, 0), 0))
    if m_ctx is None:
        lead_specs = [lat_spec, pl.BlockSpec((OUT_TM, D_MODEL), lambda t: (t + row_tile0, 0))]
        lead_args = [m, x_all]
    else:
        ctx_spec = pl.BlockSpec((OUT_TM, D_MODEL), lambda t: (jnp.minimum(t, m_ctx_tiles - 1), 0))
        lead_specs = [ctx_spec, lat_spec, ctx_spec, lat_spec]
        lead_args = [m_ctx, m, x_ctx, x_all]
    return pl.pallas_call(
        functools.partial(_outproj_kernel, ctx_tiles=m_ctx_tiles),
        grid=(n_tiles,),
        in_specs=lead_specs + [
            pl.BlockSpec((None, 6, D_MODEL), lambda t: (_mod_row(t + row_tile0, OUT_TM, ctx_tiles), 0, 0)),
            pl.BlockSpec((1, D_MODEL), lambda t: (0, 0)),
            pl.BlockSpec((D_MODEL, D_MODEL), lambda t: (0, 0)),
            pl.BlockSpec((1, D_MODEL), lambda t: (0, 0)),
            pl.BlockSpec((D_MODEL, LANE), lambda t: (0, 0)),
            pl.BlockSpec((1, LANE), lambda t: (0, 0)),
        ],
        out_specs=[
            pl.BlockSpec((OUT_TM, D_MODEL), lambda t: (t, 0)),
            pl.BlockSpec((OUT_TM, HALF_D), lambda t: (t, 0)),
            pl.BlockSpec((OUT_TM, LANE), lambda t: (t, 0)),
        ],
        out_shape=[
            jax.ShapeDtypeStruct((rows, D_MODEL), F32),
            jax.ShapeDtypeStruct((rows, HALF_D), jnp.uint32),
            jax.ShapeDtypeStruct((rows, LANE), F32),
        ],
        compiler_params=pltpu.CompilerParams(
            dimension_semantics=("arbitrary",), vmem_limit_bytes=VMEM_LIMIT),
        name="out_proj",
    )(*lead_args, mod, norm_g.reshape(1, D_MODEL), w_bf, b, wr_bf, br)


MOE_UNIT = 128
MOE_SUB = 2 * MOE_UNIT
MOE_CHUNK = 2048
MOE_NSUB = MOE_CHUNK // MOE_SUB
MOE_PIECES = (8, 4, 2, 1)
MOE_TF = 256
MOE_TN = 512
MOE_F_STEPS = D_FF // MOE_TF
MOE_N_STEPS = D_MODEL // MOE_TN
MOE_STEPS = MOE_F_STEPS + MOE_N_STEPS


def _moe_rows(n_tokens):
    slots = n_tokens * TOP_K
    padded = slots + N_EXPERTS * (MOE_UNIT - 1)
    padded = -(-padded // MOE_UNIT) * MOE_UNIT
    return padded


def _moe_chunks(n_tokens):
    return _moe_rows(n_tokens) // MOE_CHUNK + N_EXPERTS


def _row_copy_wait(src_hbm, dst, sem, rows):
    pltpu.make_async_copy(src_hbm.at[pl.ds(0, rows)], dst, sem).wait()


def _moe_kernel(ce_ref, cs_ref, cn_ref, tail_ref, *refs):
    (tok_ref, tok2_ref, f_hbm, wg_ref, wu_ref, bg_ref, bu_ref, wd_ref, bd_ref, yb_hbm, xw, xs, hs, os_, sem_in,
     sem_out) = refs
    c = pl.program_id(0)
    j = pl.program_id(1)
    cur = jnp.maximum(c - 1, 0)
    units = jnp.where(c >= 1, cn_ref[cur], 0)
    nsub = (units + 1) // 2
    start = pl.multiple_of(cs_ref[cur], MOE_UNIT)

    nxt = jnp.minimum(c, cn_ref.shape[0] - 1)
    units_next = jnp.where(c < pl.num_programs(0) - 1, cn_ref[nxt], 0)
    fetch = jnp.logical_and(j < MOE_F_STEPS, 2 * j < units_next)

    def gather_issue():
        base = pl.multiple_of(j * MOE_SUB, MOE_SUB)
        for half, toks in enumerate((tok_ref, tok2_ref)):
            for r in range(MOE_UNIT):
                pltpu.make_async_copy(f_hbm.at[pl.ds(toks[0, r], 1)], xw.at[pl.ds(base + half * MOE_UNIT + r, 1)],
                                      sem_in.at[j]).start()

    def gather_wait(i):
        _row_copy_wait(f_hbm, xw.at[pl.ds(i * MOE_SUB, MOE_SUB)], sem_in.at[i], MOE_SUB)

    def unpack(i):
        rows = slice(i * MOE_SUB, (i + 1) * MOE_SUB)
        w = xw[rows, :]
        xs[rows, :HALF_D] = lax.bitcast_convert_type(w << 16, F32).astype(BF16)
        xs[rows, HALF_D:] = lax.bitcast_convert_type(w & jnp.uint32(0xFFFF0000), F32).astype(BF16)

    def copy_out(slot, off, n, col):
        return pltpu.make_async_copy(
            os_.at[slot, pl.ds(off, n * MOE_UNIT), :],
            yb_hbm.at[pl.ds(start + off, n * MOE_UNIT), pl.ds(col, MOE_TN)],
            sem_out.at[slot])

    def for_valid_subs(fn):
        fn(0)
        for i in range(1, MOE_NSUB):
            @pl.when(i < nsub)
            def _(i=i):
                fn(i)

    big = MOE_PIECES[0]

    def for_pieces(fn):
        n_big = units // big
        rem = units - big * n_big
        for q in range(MOE_CHUNK // (big * MOE_UNIT)):
            @pl.when(q < n_big)
            def _(q=q):
                fn(q * big * MOE_UNIT, big, q == 0)
        base = n_big * (big * MOE_UNIT)
        for size in MOE_PIECES[1:]:
            above = (rem // (2 * size)) * (2 * size)

            @pl.when((rem // size) % 2 == 1)
            def _(size=size, above=above):
                fn(pl.multiple_of(base + above * MOE_UNIT, MOE_UNIT), size, False)

    @pl.when(jnp.logical_and(c == 0, j == MOE_F_STEPS))
    def _():
        os_[0, 0:MOE_UNIT, :] = jnp.zeros((MOE_UNIT, MOE_TN), F32)
        tail_start = pl.multiple_of(tail_ref[0], MOE_UNIT)

        def tail_copy(u, t):
            return pltpu.make_async_copy(
                os_.at[0, pl.ds(0, MOE_UNIT), :],
                yb_hbm.at[pl.ds(tail_start + u * MOE_UNIT, MOE_UNIT), pl.ds(t * MOE_TN, MOE_TN)],
                sem_out.at[0])

        for action in ("start", "wait"):
            for u in range(N_EXPERTS):
                @pl.when(u < tail_ref[1])
                def _(u=u, action=action):
                    for t in range(MOE_N_STEPS):
                        getattr(tail_copy(u, t), action)()

    @pl.when(jnp.logical_and(nsub > 0, j == 0))
    def _():
        def land(i):
            gather_wait(i)
            unpack(i)

        for_valid_subs(land)

    @pl.when(jnp.logical_and(fetch, units < big))
    def _():
        gather_issue()

    @pl.when(nsub > 0)
    def _():
        @pl.when(j < MOE_F_STEPS)
        def _():
            bg = bg_ref[...]
            bu = bu_ref[...]

            def gate_up_piece(off, n):
                rows = pl.ds(off, n * MOE_UNIT)
                x = xs[rows, :]
                g = jnp.dot(x, wg_ref[...].astype(BF16), preferred_element_type=F32) + bg
                u = jnp.dot(x, wu_ref[...].astype(BF16), preferred_element_type=F32) + bu
                g = jnp.minimum(g, SWIGLU_LIMIT)
                u = jnp.clip(u, -SWIGLU_LIMIT, SWIGLU_LIMIT)
                h = (u + 1.0) * (g * jax.nn.sigmoid(SWIGLU_ALPHA * g))
                hs[j, rows, :] = h.astype(BF16)

            def gate_up(off, n, first_big):
                if not first_big:
                    gate_up_piece(off, n)
                    return

                @pl.when(fetch)
                def _():
                    gather_issue()
                    gate_up_piece(off, n)

                @pl.when(jnp.logical_not(fetch))
                def _():
                    gate_up_piece(off, n)

            for_pieces(gate_up)

        @pl.when(j >= MOE_F_STEPS)
        def _():
            jn = j - MOE_F_STEPS
            slot = jn % 2
            col = pl.multiple_of(jn * MOE_TN, MOE_TN)
            bd = bd_ref[...]

            @pl.when(jn >= 2)
            def _():
                for_pieces(lambda off, n, _: copy_out(slot, off, n, col).wait())

            def down(off, n, _):
                rows = pl.ds(off, n * MOE_UNIT)
                h = jnp.concatenate([hs[f, rows, :] for f in range(MOE_F_STEPS)], axis=1)
                os_[slot, rows, :] = jnp.dot(h, wd_ref[...].astype(BF16), preferred_element_type=F32) + bd
                copy_out(slot, off, n, col).start()

            for_pieces(down)

            @pl.when(jn == MOE_N_STEPS - 1)
            def _():
                for_pieces(lambda off, n, _: copy_out(1 - slot, off, n, col).wait())
                for_pieces(lambda off, n, _: copy_out(slot, off, n, col).wait())


def _moe_experts(layer, chunk_e, chunk_start, chunk_nsub, tail, used_chunks, buf_tok, f_packed, w_gate_up, b_gate_up,
                 w_down, b_down):
    n_chunks = chunk_e.shape[0]
    rows = buf_tok.shape[0]
    n_units = rows // MOE_UNIT

    def gu_idx(half):
        def idx(c, j, ce, cs, cn, tl):
            cur = jnp.maximum(c - 1, 0)
            jj = jnp.where(cn[cur] > 0, jnp.minimum(j, MOE_F_STEPS - 1), MOE_F_STEPS - 1)
            jj = jnp.where(c == 0, 0, jj)
            return (layer, ce[cur], 0, half * MOE_F_STEPS + jj)
        return idx

    def d_idx(c, j, ce, cs, cn, tl):
        cur = jnp.maximum(c - 1, 0)
        jj = jnp.where(cn[cur] > 0, jnp.maximum(j - MOE_F_STEPS, 0), MOE_N_STEPS - 1)
        jj = jnp.where(c == 0, 0, jj)
        return (layer, ce[cur], 0, jj)

    def tok_idx(half):
        def idx(c, j, ce, cs, cn, tl):
            nxt = jnp.minimum(c, n_chunks - 1)
            unit = cs[nxt] // MOE_UNIT + 2 * jnp.minimum(j, MOE_F_STEPS - 1) + half
            return (jnp.minimum(unit, n_units - 1), 0, 0)
        return idx

    grid_spec = pltpu.PrefetchScalarGridSpec(
        num_scalar_prefetch=4,
        grid=(used_chunks + 1, MOE_STEPS),
        in_specs=[
            pl.BlockSpec((None, 1, MOE_UNIT), tok_idx(0), memory_space=pltpu.SMEM),
            pl.BlockSpec((None, 1, MOE_UNIT), tok_idx(1), memory_space=pltpu.SMEM),
            pl.BlockSpec(memory_space=pl.ANY),
            pl.BlockSpec((None, None, D_MODEL, MOE_TF), gu_idx(0)),
            pl.BlockSpec((None, None, D_MODEL, MOE_TF), gu_idx(1)),
            pl.BlockSpec((None, None, 1, MOE_TF), gu_idx(0)),
            pl.BlockSpec((None, None, 1, MOE_TF), gu_idx(1)),
            pl.BlockSpec((None, None, D_FF, MOE_TN), d_idx),
            pl.BlockSpec((None, None, 1, MOE_TN), d_idx),
        ],
        out_specs=pl.BlockSpec(memory_space=pl.ANY),
        scratch_shapes=[
            pltpu.VMEM((MOE_CHUNK, HALF_D), jnp.uint32),
            pltpu.VMEM((MOE_CHUNK, D_MODEL), BF16),
            pltpu.VMEM((MOE_F_STEPS, MOE_CHUNK, MOE_TF), BF16),
            pltpu.VMEM((2, MOE_CHUNK, MOE_TN), F32),
            pltpu.SemaphoreType.DMA((MOE_NSUB,)),
            pltpu.SemaphoreType.DMA((2,)),
        ],
    )
    bgu = b_gate_up.reshape(DEPTH, N_EXPERTS, 1, 2 * D_FF)
    bd = b_down.reshape(DEPTH, N_EXPERTS, 1, D_MODEL)
    return pl.pallas_call(
        _moe_kernel,
        grid_spec=grid_spec,
        out_shape=jax.ShapeDtypeStruct((rows, D_MODEL), F32),
        compiler_params=pltpu.CompilerParams(
            dimension_semantics=("arbitrary", "arbitrary"), vmem_limit_bytes=VMEM_LIMIT),
        name="moe_experts",
    )(chunk_e, chunk_start, chunk_nsub, tail, *([buf_tok.reshape(n_units, 1, MOE_UNIT)] * 2), f_packed,
      w_gate_up, w_gate_up, bgu, bgu, w_down, bd)


CMB_TM = 256


def _combine_kernel(idx_ref, yb_hbm, gates_ref, x_ref, mod_ref, *rest, final):
    if final:
        g_ref, o_ref, buf, sem = rest
    else:
        o_ref, buf, sem = rest

    for k in range(TOP_K):
        def issue(r, carry, k=k):
            pltpu.make_async_copy(yb_hbm.at[pl.ds(idx_ref[0, k * CMB_TM + r], 1)], buf.at[k, pl.ds(r, 1)],
                                  sem.at[k]).start()
            return carry

        lax.fori_loop(0, CMB_TM, issue, 0, unroll=8)

    gates = gates_ref[...]
    y = None
    for k in range(TOP_K):
        _row_copy_wait(yb_hbm, buf.at[k], sem.at[k], CMB_TM)
        t = gates[:, k:k + 1] * buf[k]
        y = t if y is None else y + t
    xn = x_ref[...] + mod_ref[5:6, :] * y
    if final:
        xn = xn * lax.rsqrt(jnp.mean(xn * xn, axis=-1, keepdims=True) + EPS) * g_ref[...]
    o_ref[...] = xn


def _combine(yb, dest, gates, x_mid, mod, final_g, *, row_tile0, ctx_tiles):
    n = x_mid.shape[0]
    n_tiles = n // CMB_TM
    final = final_g is not None
    idx = dest.reshape(n_tiles, CMB_TM, TOP_K).transpose(0, 2, 1).reshape(n_tiles, 1, TOP_K * CMB_TM)
    in_specs = [
        pl.BlockSpec((None, 1, TOP_K * CMB_TM), lambda t: (t, 0, 0), memory_space=pltpu.SMEM),
        pl.BlockSpec(memory_space=pl.ANY),
        pl.BlockSpec((CMB_TM, TOP_K), lambda t: (t, 0)),
        pl.BlockSpec((CMB_TM, D_MODEL), lambda t: (t, 0)),
        pl.BlockSpec((None, 6, D_MODEL), lambda t: (_mod_row(t + row_tile0, CMB_TM, ctx_tiles), 0, 0)),
    ]
    args = [idx, yb, gates, x_mid, mod]
    if final:
        in_specs.append(pl.BlockSpec((1, D_MODEL), lambda t: (0, 0)))
        args.append(final_g.reshape(1, D_MODEL))
    return pl.pallas_call(
        functools.partial(_combine_kernel, final=final),
        grid=(n_tiles,),
        in_specs=in_specs,
        out_specs=pl.BlockSpec((CMB_TM, D_MODEL), lambda t: (t, 0)),
        out_shape=jax.ShapeDtypeStruct((n, D_MODEL), F32),
        scratch_shapes=[pltpu.VMEM((TOP_K, CMB_TM, D_MODEL), F32), pltpu.SemaphoreType.DMA((TOP_K,))],
        compiler_params=pltpu.CompilerParams(dimension_semantics=("arbitrary",), vmem_limit_bytes=VMEM_LIMIT),
        name="moe_combine",
    )(*args)


def _moe_ffn(layer, f_packed, logits, w_gate_up, b_gate_up, w_down, b_down):
    n = f_packed.shape[0]
    slots = n * TOP_K
    buf_rows = _moe_rows(n)
    n_chunks = _moe_chunks(n)

    top_logit, top_e = lax.top_k(logits, TOP_K)
    gates = jax.nn.softmax(top_logit, axis=-1)
    flat_e = top_e.reshape(-1).astype(jnp.int32)
    onehot = (flat_e[:, None] == jnp.arange(N_EXPERTS, dtype=jnp.int32)[None, :]).astype(jnp.int32)
    running = jnp.cumsum(onehot, axis=0)
    counts = running[-1]
    padded = (counts + MOE_UNIT - 1) // MOE_UNIT * MOE_UNIT
    pad_end = jnp.cumsum(padded)
    pad_start = pad_end - padded
    dest_of_slot = jnp.sum(onehot * (running - 1 + pad_start[None, :]), axis=1)
    buf_tok = jnp.zeros((buf_rows,), jnp.int32).at[dest_of_slot].set(
        jnp.arange(slots, dtype=jnp.int32) // TOP_K, unique_indices=True, mode='promise_in_bounds')

    e_chunks = (padded + MOE_CHUNK - 1) // MOE_CHUNK
    chunk_end = jnp.cumsum(e_chunks)
    total_chunks = chunk_end[-1]
    cidx = jnp.arange(n_chunks, dtype=jnp.int32)
    ce = jnp.minimum(jnp.searchsorted(chunk_end, cidx, side='right'), N_EXPERTS - 1).astype(jnp.int32)
    local = cidx - (chunk_end[ce] - e_chunks[ce])
    valid = cidx < total_chunks
    c_start = jnp.where(valid, pad_start[ce] + local * MOE_CHUNK, 0).astype(jnp.int32)
    c_nsub = jnp.where(valid, jnp.minimum(MOE_CHUNK, padded[ce] - local * MOE_CHUNK) // MOE_UNIT, 0).astype(jnp.int32)
    last_e = ce[jnp.maximum(total_chunks - 1, 0)]
    ce = jnp.where(valid, ce, last_e).astype(jnp.int32)

    tail = jnp.stack([pad_end[-1], (buf_rows - pad_end[-1]) // MOE_UNIT]).astype(jnp.int32)
    yb = _moe_experts(layer, ce, c_start, c_nsub, tail, total_chunks.astype(jnp.int32), buf_tok, f_packed,
                      w_gate_up, b_gate_up, w_down, b_down)
    return yb, dest_of_slot.reshape(n, TOP_K), gates


MIX_TB = 256
CONV_HALO = 16
CONV_PIECE = 64
NEG_INF = float("-inf")


def _conv_pitch(rowlen):
    return rowlen + 2 * CONV_HALO


def _dwconv_block(u, pad_ref, w_ref, rowlen):
    taps = w_ref.shape[0]
    half = taps // 2
    pitch = _conv_pitch(rowlen)
    ch = u.shape[1]
    zeros = jnp.zeros((CONV_HALO, ch), F32)
    for r in range(MIX_TB // rowlen):
        base = r * pitch
        pad_ref[base:base + CONV_HALO, :] = zeros
        pad_ref[base + CONV_HALO:base + CONV_HALO + rowlen, :] = u[r * rowlen:(r + 1) * rowlen, :]
        pad_ref[base + CONV_HALO + rowlen:base + pitch, :] = zeros
    outs = []
    for r in range(MIX_TB // rowlen):
        for piece in range(rowlen // CONV_PIECE):
            acc = None
            for j in range(taps):
                off = r * pitch + CONV_HALO + piece * CONV_PIECE + j - half
                term = pad_ref[off:off + CONV_PIECE, :] * w_ref[j:j + 1, :]
                acc = term if acc is None else acc + term
            outs.append(acc)
    return jnp.concatenate(outs, axis=0)


def _ln_rows(x, g, b):
    xc = x - jnp.mean(x, axis=-1, keepdims=True)
    return xc * lax.rsqrt(jnp.mean(xc * xc, axis=-1, keepdims=True) + EPS) * g + b


def _silu(x):
    return x * jax.nn.sigmoid(x)


SSD_PAIRS = SSD_HEADS // 2
SSD_PAIR_W = 2 * SSD_HEAD_DIM


def _ssd_kernel(xbc_f, dt_f, xbc_b, dt_b, init_ref, cw_ref, cb_ref, dtb_ref, a_ref, skip_ref, y_f, y_b, fin_ref,
                pad_f, pad_b, st_ref, *, rowlen):
    s = pl.program_id(1)

    @pl.when(s == 0)
    def _():
        st_ref[...] = init_ref[...]

    for d, (xbc_ref, dt_ref, y_ref, pad_ref) in enumerate(((xbc_f, dt_f, y_f, pad_f), (xbc_b, dt_b, y_b, pad_b))):
        _ssd_direction(xbc_ref, dt_ref, cw_ref, cb_ref, dtb_ref, a_ref, skip_ref[d:d + 1, :], y_ref, pad_ref,
                       st_ref.at[d], rowlen=rowlen, rev=d == 1, dcol=d * SSD_HEADS)

    @pl.when(s == pl.num_programs(1) - 1)
    def _():
        fin_ref[...] = st_ref[...]


def _ssd_direction(xbc_ref, dt_ref, cw_ref, cb_ref, dtb_ref, a_ref, skip_row, y_ref, pad_ref, st_ref, *, rowlen, rev,
                   dcol):
    xa = _silu(_dwconv_block(xbc_ref[...], pad_ref, cw_ref, rowlen) + cb_ref[...])
    dt_all = dt_ref[...] + dtb_ref[...]
    dt_all = jnp.maximum(dt_all, 0.0) + jnp.log1p(jnp.exp(-jnp.abs(dt_all)))
    da_all = dt_all * a_ref[...]

    row_i = lax.broadcasted_iota(jnp.int32, (SSD_CHUNK, SSD_CHUNK), 0)
    col_i = lax.broadcasted_iota(jnp.int32, (SSD_CHUNK, SSD_CHUNK), 1)
    tri = (col_i >= row_i) if rev else (col_i <= row_i)
    tri_f = tri.astype(F32)
    first_half = lax.broadcasted_iota(jnp.int32, (SSD_CHUNK, SSD_PAIR_W), 1) < SSD_HEAD_DIM
    first_half_row = first_half[0:1, :]

    chunks = range(MIX_TB // SSD_CHUNK)
    for ci in (reversed(chunks) if rev else chunks):
        rows = slice(ci * SSD_CHUNK, (ci + 1) * SSD_CHUNK)
        acc = jnp.dot(tri_f, da_all[rows, :], preferred_element_type=F32, precision=lax.Precision.HIGHEST)
        acc_t = acc.T
        tot = acc[0:1, :] if rev else acc[SSD_CHUNK - 1:SSD_CHUNK, :]
        to_end = jnp.exp(tot - acc)
        from_start = jnp.exp(acc)
        chunk_decay = jnp.exp(tot)
        dt_c = dt_all[rows, :]
        for g in range(SSD_GROUPS):
            bg = xa[rows, GROUP_W + g * SSD_STATE:GROUP_W + (g + 1) * SSD_STATE]
            cg = xa[rows, GROUP_W + (SSD_GROUPS + g) * SSD_STATE:GROUP_W + (SSD_GROUPS + g + 1) * SSD_STATE]
            scores = lax.dot_general(cg.astype(BF16), bg.astype(BF16), (((1,), (1,)), ((), ())),
                                     preferred_element_type=F32)
            pairs_per_group = SSD_PAIRS // SSD_GROUPS
            for p in range(g * pairs_per_group, (g + 1) * pairs_per_group):
                c0 = dcol + 2 * p
                c1 = c0 + 1
                lhs, bw = [], []
                for col in (c0, c1):
                    seg = acc[:, col:col + 1] - acc_t[col:col + 1, :]
                    lhs.append(scores * jnp.exp(jnp.where(tri, seg, NEG_INF)))
                for col in (c0, c1):
                    lhs.append(cg * from_start[:, col:col + 1])
                    bw.append((bg * to_end[:, col:col + 1]).T)
                xp = xa[rows, p * SSD_PAIR_W:(p + 1) * SSD_PAIR_W]
                xd = xp * jnp.where(first_half, dt_c[:, c0:c0 + 1], dt_c[:, c1:c1 + 1])
                xd_top = jnp.where(first_half, xd, 0.0).astype(BF16)
                xd_bot = jnp.where(first_half, 0.0, xd).astype(BF16)
                st = st_ref[p]
                st_top = jnp.where(first_half, st, 0.0).astype(BF16)
                st_bot = jnp.where(first_half, 0.0, st).astype(BF16)
                y = jnp.dot(jnp.concatenate(lhs, axis=1).astype(BF16),
                            jnp.concatenate([xd_top, xd_bot, st_top, st_bot], axis=0),
                            preferred_element_type=F32)
                y_ref[rows, p * SSD_PAIR_W:(p + 1) * SSD_PAIR_W] = y + skip_row[:, p * SSD_PAIR_W:(p + 1) * SSD_PAIR_W] * xp
                upd = jnp.dot(jnp.concatenate(bw, axis=1).astype(BF16), jnp.concatenate([xd_top, xd_bot], axis=0),
                              preferred_element_type=F32)
                decay_lane = jnp.where(first_half_row, chunk_decay[:, c0:c0 + 1], chunk_decay[:, c1:c1 + 1])
                st_ref[p] = st * decay_lane + upd


def _ssd_sweep(xbc_src, xbc_col_block, dt_src, row_block0, n_blocks, init, prm, *, rowlen):
    cw, cb, dtb_row, a_row, skip_rows = prm

    def fwd(b, s):
        return b * n_blocks + s

    def bwd(b, s):
        return b * n_blocks + (n_blocks - 1 - s)

    state_block = (None, 2, SSD_PAIRS, SSD_STATE, SSD_PAIR_W)
    y_shape = jax.ShapeDtypeStruct((BATCH * n_blocks * MIX_TB, GROUP_W), F32)
    pad_shape = pltpu.VMEM(((MIX_TB // rowlen) * _conv_pitch(rowlen), SSD_XBC), F32)
    return pl.pallas_call(
        functools.partial(_ssd_kernel, rowlen=rowlen),
        grid=(BATCH, n_blocks),
        in_specs=[
            pl.BlockSpec((MIX_TB, SSD_XBC), lambda b, s: (row_block0 + fwd(b, s), xbc_col_block)),
            pl.BlockSpec((MIX_TB, LANE), lambda b, s: (row_block0 + fwd(b, s), 0)),
            pl.BlockSpec((MIX_TB, SSD_XBC), lambda b, s: (row_block0 + bwd(b, s), xbc_col_block)),
            pl.BlockSpec((MIX_TB, LANE), lambda b, s: (row_block0 + bwd(b, s), 0)),
            pl.BlockSpec(state_block, lambda b, s: (b, 0, 0, 0, 0)),
            pl.BlockSpec((3, SSD_XBC), lambda b, s: (0, 0)),
            pl.BlockSpec((1, SSD_XBC), lambda b, s: (0, 0)),
            pl.BlockSpec((1, LANE), lambda b, s: (0, 0)),
            pl.BlockSpec((1, LANE), lambda b, s: (0, 0)),
            pl.BlockSpec((2, GROUP_W), lambda b, s: (0, 0)),
        ],
        out_specs=[
            pl.BlockSpec((MIX_TB, GROUP_W), lambda b, s: (fwd(b, s), 0)),
            pl.BlockSpec((MIX_TB, GROUP_W), lambda b, s: (bwd(b, s), 0)),
            pl.BlockSpec(state_block, lambda b, s: (b, 0, 0, 0, 0)),
        ],
        out_shape=[y_shape, y_shape, jax.ShapeDtypeStruct((BATCH, 2, SSD_PAIRS, SSD_STATE, SSD_PAIR_W), F32)],
        scratch_shapes=[pad_shape, pad_shape, pltpu.VMEM((2, SSD_PAIRS, SSD_STATE, SSD_PAIR_W), F32)],
        compiler_params=pltpu.CompilerParams(
            dimension_semantics=("arbitrary", "arbitrary"), vmem_limit_bytes=VMEM_LIMIT),
        name="ssd_sweep",
    )(xbc_src, dt_src, xbc_src, dt_src, init, cw, cb, dtb_row, a_row, skip_rows)


def _ssd_params(conv_w, conv_b, dt_bias, a_log, d_skip):
    pad = LANE - 2 * SSD_HEADS
    dtb_row = jnp.pad(dt_bias.reshape(-1), (0, pad)).reshape(1, LANE)
    a_row = jnp.pad(-jnp.exp(a_log.reshape(-1)), (0, pad)).reshape(1, LANE)
    skip_rows = jnp.repeat(d_skip, SSD_HEAD_DIM, axis=1)
    return conv_w, conv_b.reshape(1, SSD_XBC), dtb_row, a_row, skip_rows


def _local_kernel(p_ref, z_ref, y0_ref, y1_ref, scw_ref, cfw_ref, cfb_ref, cfg_ref, cfbeta_ref, sgg_ref,
                  sgbeta_ref, sgw_ref, sgb_ref, ng_ref, m_ref, pad_ref, *, rowlen):
    gate_b = p_ref[:, 0:GROUP_W]
    u = p_ref[:, GROUP_W:2 * GROUP_W] * p_ref[:, 2 * GROUP_W:3 * GROUP_W]
    m_ref[:, 0:GROUP_W] = gate_b * _dwconv_block(u, pad_ref, scw_ref, rowlen)
    u = p_ref[:, OFF_CF:OFF_CF + GROUP_W] * jax.nn.sigmoid(p_ref[:, OFF_CF + GROUP_W:OFF_SG])
    u = _dwconv_block(u, pad_ref, cfw_ref, rowlen) + cfb_ref[...]
    m_ref[:, GROUP_W:2 * GROUP_W] = _silu(_ln_rows(u, cfg_ref[...], cfbeta_ref[...]))
    q = p_ref[:, OFF_SG:OFF_SSD]
    q = 0.5 * q * (1.0 + lax.erf(q * (2.0 ** -0.5)))
    v = _ln_rows(q[:, GROUP_W:], sgg_ref[...], sgbeta_ref[...]).astype(BF16)
    n_chunks = MIX_TB // SG_CHUNK
    for h in range(SG_HEADS):
        cols = slice(h * SG_HEAD_DIM, (h + 1) * SG_HEAD_DIM)
        rhs = jnp.concatenate([v[c * SG_CHUNK:(c + 1) * SG_CHUNK, cols] for c in range(n_chunks)], axis=1)
        sres = jnp.dot(sgw_ref[h].astype(BF16), rhs, preferred_element_type=F32) + sgb_ref[:, h:h + 1]
        for c in range(n_chunks):
            rows = slice(c * SG_CHUNK, (c + 1) * SG_CHUNK)
            m_ref[rows, 2 * GROUP_W + h * SG_HEAD_DIM:2 * GROUP_W + (h + 1) * SG_HEAD_DIM] = (
                q[rows, cols] * sres[:, c * SG_HEAD_DIM:(c + 1) * SG_HEAD_DIM])
    yv = (y0_ref[...] + y1_ref[...]) * _silu(z_ref[...])
    gw = GROUP_W // SSD_GROUPS
    for g in range(SSD_GROUPS):
        vg = yv[:, g * gw:(g + 1) * gw]
        vg = vg * lax.rsqrt(jnp.mean(vg * vg, axis=-1, keepdims=True) + EPS)
        m_ref[:, 3 * GROUP_W + g * gw:3 * GROUP_W + (g + 1) * gw] = vg * ng_ref[:, g * gw:(g + 1) * gw]


def _local_mixers_call(p_src, z_col_block, row_block0, n_blocks, y0, y1, lw, *, rowlen):
    vec = lambda a: a.reshape(1, GROUP_W)
    args = [lw['sc_conv_w'], lw['cf_conv_w'], vec(lw['cf_conv_b']), vec(lw['cf_ln_g']), vec(lw['cf_ln_b']),
            vec(lw['sg_ln_g']), vec(lw['sg_ln_b']), lw['sg_w'], lw['sg_b'].T, vec(lw['ssd_norm_g'])]
    full = lambda a: pl.BlockSpec(a.shape, lambda t, nd=a.ndim: (0,) * nd)
    return pl.pallas_call(
        functools.partial(_local_kernel, rowlen=rowlen),
        grid=(n_blocks,),
        in_specs=[
            pl.BlockSpec((MIX_TB, OFF_SSD), lambda t: (t + row_block0, 0)),
            pl.BlockSpec((MIX_TB, GROUP_W), lambda t: (t + row_block0, z_col_block)),
            pl.BlockSpec((MIX_TB, GROUP_W), lambda t: (t, 0)),
            pl.BlockSpec((MIX_TB, GROUP_W), lambda t: (t, 0)),
        ] + [full(a) for a in args],
        out_specs=pl.BlockSpec((MIX_TB, D_MODEL), lambda t: (t, 0)),
        out_shape=jax.ShapeDtypeStruct((n_blocks * MIX_TB, D_MODEL), F32),
        scratch_shapes=[pltpu.VMEM(((MIX_TB // rowlen) * _conv_pitch(rowlen), GROUP_W), F32)],
        compiler_params=pltpu.CompilerParams(dimension_semantics=("arbitrary",), vmem_limit_bytes=VMEM_LIMIT),
        name="local_mixers",
    )(p_src, p_src, y0, y1, *args)


def _token_mixers(p_ctx_src, ctx_cols, p_lat_src, lat_cols, dt_ctx, dt_lat, lat_row_block0, lw, ctx_out):
    prm = _ssd_params(lw['ssd_conv_w'], lw['ssd_conv_b'], lw['ssd_dt_bias'], lw['ssd_a_log'], lw['ssd_d'])
    zero_state = jnp.zeros((BATCH, 2, SSD_PAIRS, SSD_STATE, SSD_PAIR_W), F32)
    ctx_blocks = CTX_LEN // MIX_TB
    lat_blocks = SEQ // MIX_TB
    *y_ctx, state = _ssd_sweep(p_ctx_src, ctx_cols[1], dt_ctx, 0, ctx_blocks, zero_state, prm, rowlen=CTX_LEN)
    *y_lat, _ = _ssd_sweep(p_lat_src, lat_cols[1], dt_lat, lat_row_block0, lat_blocks, state, prm, rowlen=GRID_W)
    m_lat = _local_mixers_call(p_lat_src, lat_cols[0], lat_row_block0, BATCH * lat_blocks, y_lat[0], y_lat[1], lw,
                               rowlen=GRID_W)
    m_ctx = None
    if ctx_out:
        m_ctx = _local_mixers_call(p_ctx_src, ctx_cols[0], 0, BATCH * ctx_blocks, y_ctx[0], y_ctx[1], lw,
                                   rowlen=CTX_LEN)
    return m_lat, m_ctx


def kernel(x, c, ctx, c_ctx, w_mod, b_mod, norm1_g, norm2_g, w_in, b_in, sc_conv_w, cf_conv_w, cf_conv_b,
           cf_ln_g, cf_ln_b, sg_ln_g, sg_ln_b, sg_w, sg_b, ssd_conv_w, ssd_conv_b, ssd_dt_bias, ssd_a_log,
           ssd_d, ssd_norm_g, w_out, b_out, w_router, b_router, w_gate_up, b_gate_up, w_down, b_down,
           final_norm_g):
    cc = jnp.concatenate([c_ctx[None, :], c, jnp.zeros((MOD_ROWS - 1 - BATCH, D_MODEL), F32)], axis=0)
    mod_all = _modulation(cc, w_mod, b_mod).reshape(DEPTH, MOD_ROWS, 6, D_MODEL)

    assert DEPTH == 2
    x_ctx0 = ctx.reshape(N_CTX, D_MODEL)
    x_lat0 = x.reshape(N_LAT, D_MODEL)
    x_all = None

    for i in range(DEPTH):
        last = i == DEPTH - 1
        lw = dict(sc_conv_w=sc_conv_w[i], cf_conv_w=cf_conv_w[i], cf_conv_b=cf_conv_b[i], cf_ln_g=cf_ln_g[i],
                  cf_ln_b=cf_ln_b[i], sg_ln_g=sg_ln_g[i], sg_ln_b=sg_ln_b[i], sg_w=sg_w[i], sg_b=sg_b[i],
                  ssd_conv_w=ssd_conv_w[i], ssd_conv_b=ssd_conv_b[i], ssd_dt_bias=ssd_dt_bias[i],
                  ssd_a_log=ssd_a_log[i], ssd_d=ssd_d[i], ssd_norm_g=ssd_norm_g[i])
        mod = mod_all[i]
        w_in_bf = w_in[i, :, :MAIN_COLS].astype(BF16)
        b_in_main = b_in[i, :MAIN_COLS].reshape(1, MAIN_COLS)
        wdt_bf = jnp.pad(w_in[i, :, MAIN_COLS:], ((0, 0), (0, LANE - DT_COLS))).astype(BF16)
        bdt = jnp.pad(b_in[i, MAIN_COLS:], (0, LANE - DT_COLS)).reshape(1, LANE)
        w_out_bf = w_out[i].astype(BF16)
        wr_bf = jnp.pad(w_router[i], ((0, 0), (0, LANE - N_EXPERTS))).astype(BF16)
        br = jnp.pad(b_router[i], (0, LANE - N_EXPERTS)).reshape(1, LANE)
        in_ctx_tiles = N_CTX // IN_TM
        proj_args = (norm1_g[i], mod, w_in_bf, b_in_main, wdt_bf, bdt)

        z_xbc_cols = (OFF_SSD // GROUP_W, (OFF_SSD + GROUP_W) // SSD_XBC)
        if not last:
            all_cols = dict(col_tile0=0, n_col_tiles=MAIN_COLS // IN_TN, ctx_tiles=in_ctx_tiles)
            p_ctx, dt_ctx = _in_projection(x_ctx0, *proj_args, row_tile0=0, n_row_tiles=in_ctx_tiles, **all_cols)
            p_lat, dt_lat = _in_projection(x_lat0, *proj_args, row_tile0=0, n_row_tiles=N_LAT // IN_TM,
                                           mod_tile0=in_ctx_tiles, **all_cols)
            m_lat, m_ctx = _token_mixers(p_ctx, z_xbc_cols, p_lat, z_xbc_cols, dt_ctx, dt_lat, 0, lw, True)
        else:
            proj = functools.partial(_in_projection, x_all, *proj_args, ctx_tiles=in_ctx_tiles)
            p_lat, dt_lat = proj(row_tile0=in_ctx_tiles, n_row_tiles=N_LAT // IN_TM,
                                 col_tile0=0, n_col_tiles=MAIN_COLS // IN_TN)
            ctx_col0 = MAIN_COLS - 2 * SSD_XBC
            p_ctx, dt_ctx = proj(row_tile0=0, n_row_tiles=in_ctx_tiles,
                                 col_tile0=ctx_col0 // IN_TN, n_col_tiles=(MAIN_COLS - ctx_col0) // IN_TN)
            ctx_cols = ((OFF_SSD - ctx_col0) // GROUP_W, (OFF_SSD + GROUP_W - ctx_col0) // SSD_XBC)
            m_lat, m_ctx = _token_mixers(p_ctx, ctx_cols, p_lat, z_xbc_cols, dt_ctx, dt_lat, 0, lw, False)
        out_ctx_tiles = N_CTX // OUT_TM
        row_tile0 = out_ctx_tiles if last else 0
        x_mid, f_packed, logits = _out_projection(m_ctx, m_lat, None if last else x_ctx0,
                                                  x_all if last else x_lat0, mod, norm2_g[i], w_out_bf,
                                                  b_out[i].reshape(1, D_MODEL), wr_bf, br,
                                                  row_tile0=row_tile0, ctx_tiles=out_ctx_tiles)
        yb, dest, gates = _moe_ffn(i, f_packed, logits[:, :N_EXPERTS], w_gate_up, b_gate_up, w_down, b_down)
        x_all = _combine(yb, dest, gates, x_mid, mod, final_norm_g if last else None,
                         row_tile0=row_tile0 * OUT_TM // CMB_TM, ctx_tiles=N_CTX // CMB_TM)

    return x_all.reshape(BATCH, SEQ, D_MODEL)
```

```python
import functools

import jax
import jax.numpy as jnp
from jax import lax
from jax.experimental import pallas as pl
from jax.experimental.pallas import tpu as pltpu

F32 = jnp.float32
BF16 = jnp.bfloat16

D_MODEL = 2048
BATCH = 4
SEQ = 2048
DEPTH = 2
GRID_W = 64
CTX_LEN = 256
EPS = 1e-6
GROUP_W = 512
SG_HEADS = 4
SG_CHUNK = 128
SG_HEAD_DIM = 128
SSD_HEAD_DIM = 64
SSD_HEADS = 8
SSD_GROUPS = 2
SSD_STATE = 128
SSD_CHUNK = 128
SSD_XBC = 1024
N_EXPERTS = 32
TOP_K = 4
D_FF = 2048
SWIGLU_LIMIT = 7.0
SWIGLU_ALPHA = 1.702
OFF_CF = 1536
OFF_SG = 2560
OFF_SSD = 3584
MAIN_COLS = 5120
DT_COLS = 2 * SSD_HEADS
LANE = 128
HALF_D = D_MODEL // 2

N_CTX = BATCH * CTX_LEN
N_LAT = BATCH * SEQ

VMEM_LIMIT = 56 * 1024 * 1024

MOD_ROWS = 8
MOD_TN = 1024


def _mod_kernel(c_ref, w_ref, b_ref, o_ref):
    c = c_ref[...]
    s = c * jax.nn.sigmoid(c)
    o_ref[...] = jnp.dot(s.astype(BF16), w_ref[...].astype(BF16), preferred_element_type=F32) + b_ref[...]


def _modulation(cc, w_mod, b_mod):
    n_out = 6 * D_MODEL
    return pl.pallas_call(
        _mod_kernel,
        grid=(DEPTH, n_out // MOD_TN),
        in_specs=[
            pl.BlockSpec((MOD_ROWS, D_MODEL), lambda l, n: (0, 0)),
            pl.BlockSpec((None, D_MODEL, MOD_TN), lambda l, n: (l, 0, n)),
            pl.BlockSpec((None, 1, MOD_TN), lambda l, n: (l, 0, n)),
        ],
        out_specs=pl.BlockSpec((None, MOD_ROWS, MOD_TN), lambda l, n: (l, 0, n)),
        out_shape=jax.ShapeDtypeStruct((DEPTH, MOD_ROWS, n_out), F32),
        compiler_params=pltpu.CompilerParams(
            dimension_semantics=("arbitrary", "arbitrary"), vmem_limit_bytes=VMEM_LIMIT),
        name="adaln_mod",
    )(cc, w_mod, b_mod.reshape(DEPTH, 1, n_out))


def _mod_row(tile, tile_rows, ctx_tiles):
    tiles_per_batch = SEQ // tile_rows
    return jnp.where(tile < ctx_tiles, 0, 1 + (tile - ctx_tiles) // tiles_per_batch)


IN_TM = 1024
IN_TN = 1024
IN_PRO_ROWS = 256


def _inproj_kernel(x_ref, g_ref, mod_ref, w_ref, b_ref, wdt_ref, bdt_ref, o_ref, odt_ref, h_ref):
    @pl.when(pl.program_id(1) == 0)
    def _():
        g = g_ref[...]
        scale = 1.0 + mod_ref[1:2, :]
        shift = mod_ref[0:1, :]
        for r in range(IN_TM // IN_PRO_ROWS):
            rows = slice(r * IN_PRO_ROWS, (r + 1) * IN_PRO_ROWS)
            x = x_ref[rows, :]
            y = x * lax.rsqrt(jnp.mean(x * x, axis=-1, keepdims=True) + EPS)
            h_ref[rows, :] = ((y * g) * scale + shift).astype(BF16)
        odt_ref[...] = jnp.dot(h_ref[...], wdt_ref[...], preferred_element_type=F32) + bdt_ref[...]

    o_ref[...] = jnp.dot(h_ref[...], w_ref[...], preferred_element_type=F32) + b_ref[...]


def _in_projection(x_all, norm_g, mod, w_bf, b, wdt_bf, bdt, *, row_tile0, n_row_tiles, col_tile0, n_col_tiles,
                   ctx_tiles, mod_tile0=None):
    rows = n_row_tiles * IN_TM
    mod_tile0 = row_tile0 if mod_tile0 is None else mod_tile0
    return pl.pallas_call(
        _inproj_kernel,
        grid=(n_row_tiles, n_col_tiles),
        in_specs=[
            pl.BlockSpec((IN_TM, D_MODEL), lambda m, n: (m + row_tile0, 0)),
            pl.BlockSpec((1, D_MODEL), lambda m, n: (0, 0)),
            pl.BlockSpec((None, 6, D_MODEL), lambda m, n: (_mod_row(m + mod_tile0, IN_TM, ctx_tiles), 0, 0)),
            pl.BlockSpec((D_MODEL, IN_TN), lambda m, n: (0, n + col_tile0)),
            pl.BlockSpec((1, IN_TN), lambda m, n: (0, n + col_tile0)),
            pl.BlockSpec((D_MODEL, LANE), lambda m, n: (0, 0)),
            pl.BlockSpec((1, LANE), lambda m, n: (0, 0)),
        ],
        out_specs=[
            pl.BlockSpec((IN_TM, IN_TN), lambda m, n: (m, n)),
            pl.BlockSpec((IN_TM, LANE), lambda m, n: (m, 0)),
        ],
        out_shape=[
            jax.ShapeDtypeStruct((rows, n_col_tiles * IN_TN), F32),
            jax.ShapeDtypeStruct((rows, LANE), F32),
        ],
        scratch_shapes=[pltpu.VMEM((IN_TM, D_MODEL), BF16)],
        compiler_params=pltpu.CompilerParams(
            dimension_semantics=("arbitrary", "arbitrary"), vmem_limit_bytes=VMEM_LIMIT),
        name="in_proj",
    )(x_all, norm_g.reshape(1, D_MODEL), mod, w_bf, b, wdt_bf, bdt)


OUT_TM = 256


def _outproj_kernel(*refs, ctx_tiles):
    if ctx_tiles:
        mc_ref, m_ref, xc_ref, x_ref, mod_ref, g_ref, w_ref, b_ref, wr_ref, br_ref, xo_ref, f_ref, lg_ref = refs
        is_ctx = pl.program_id(0) < ctx_tiles
        m = jnp.where(is_ctx, mc_ref[...], m_ref[...])
        x = jnp.where(is_ctx, xc_ref[...], x_ref[...])
    else:
        m_ref, x_ref, mod_ref, g_ref, w_ref, b_ref, wr_ref, br_ref, xo_ref, f_ref, lg_ref = refs
        m = m_ref[...]
        x = x_ref[...]
    y = jnp.dot(m.astype(BF16), w_ref[...], preferred_element_type=F32) + b_ref[...]
    xn = x + mod_ref[2:3, :] * y
    xo_ref[...] = xn
    r = lax.rsqrt(jnp.mean(xn * xn, axis=-1, keepdims=True) + EPS)
    f = ((xn * r) * g_ref[...]) * (1.0 + mod_ref[4:5, :]) + mod_ref[3:4, :]
    fb = f.astype(BF16)
    bits = lax.bitcast_convert_type(fb.astype(F32), jnp.uint32)
    f_ref[...] = (bits[:, HALF_D:] & jnp.uint32(0xFFFF0000)) | (bits[:, :HALF_D] >> 16)
    lg_ref[...] = jnp.dot(fb, wr_ref[...], preferred_element_type=F32) + br_ref[...]


def _out_projection(m_ctx, m, x_ctx, x_all, mod, norm_g, w_bf, b, wr_bf, br, *, row_tile0, ctx_tiles):
    m_ctx_tiles = 0 if m_ctx is None else m_ctx.shape[0] // OUT_TM
    rows = m.shape[0] + m_ctx_tiles * OUT_TM
    n_tiles = rows // OUT_TM
    lat_spec = pl.BlockSpec((OUT_TM, D_MODEL), lambda t: (jnp.maximum(t - m_ctx_tiles, 0), 0))
    if m_ctx is None:
        lead_specs = [lat_spec, pl.BlockSpec((OUT_TM, D_MODEL), lambda t: (t + row_tile0, 0))]
        lead_args = [m, x_all]
    else:
        ctx_spec = pl.BlockSpec((OUT_TM, D_MODEL), lambda t: (jnp.minimum(t, m_ctx_tiles - 1), 0))
        lead_specs = [ctx_spec, lat_spec, ctx_spec, lat_spec]
        lead_args = [m_ctx, m, x_ctx, x_all]
    return pl.pallas_call(
        functools.partial(_outproj_kernel, ctx_tiles=m_ctx_tiles),
        grid=(n_tiles,),
        in_specs=lead_specs + [
            pl.BlockSpec((None, 6, D_MODEL), lambda t: (_mod_row(t + row_tile0, OUT_TM, ctx_tiles), 0, 0)),
            pl.BlockSpec((1, D_MODEL), lambda t: (0, 0)),
            pl.BlockSpec((D_MODEL, D_MODEL), lambda t: (0, 0)),
            pl.BlockSpec((1, D_MODEL), lambda t: (0, 0)),
            pl.BlockSpec((D_MODEL, LANE), lambda t: (0, 0)),
            pl.BlockSpec((1, LANE), lambda t: (0, 0)),
        ],
        out_specs=[
            pl.BlockSpec((OUT_TM, D_MODEL), lambda t: (t, 0)),
            pl.BlockSpec((OUT_TM, HALF_D), lambda t: (t, 0)),
            pl.BlockSpec((OUT_TM, LANE), lambda t: (t, 0)),
        ],
        out_shape=[
            jax.ShapeDtypeStruct((rows, D_MODEL), F32),
            jax.ShapeDtypeStruct((rows, HALF_D), jnp.uint32),
            jax.ShapeDtypeStruct((rows, LANE), F32),
        ],
        compiler_params=pltpu.CompilerParams(
            dimension_semantics=("arbitrary",), vmem_limit_bytes=VMEM_LIMIT),
        name="out_proj",
    )(*lead_args, mod, norm_g.reshape(1, D_MODEL), w_bf, b, wr_bf, br)


MOE_UNIT = 128
MOE_SUB = 2 * MOE_UNIT
MOE_CHUNK = 2048
MOE_NSUB = MOE_CHUNK // MOE_SUB
MOE_PIECES = (8, 4, 2, 1)
MOE_TF = 256
MOE_TN = 512
MOE_F_STEPS = D_FF // MOE_TF
MOE_N_STEPS = D_MODEL // MOE_TN
MOE_STEPS = MOE_F_STEPS + MOE_N_STEPS


def _moe_rows(n_tokens):
    slots = n_tokens * TOP_K
    padded = slots + N_EXPERTS * (MOE_UNIT - 1)
    padded = -(-padded // MOE_UNIT) * MOE_UNIT
    return padded


def _moe_chunks(n_tokens):
    return _moe_rows(n_tokens) // MOE_CHUNK + N_EXPERTS


def _row_copy_wait(src_hbm, dst, sem, rows):
    pltpu.make_async_copy(src_hbm.at[pl.ds(0, rows)], dst, sem).wait()


def _moe_kernel(ce_ref, cs_ref, cn_ref, tail_ref, *refs):
    (tok_ref, tok2_ref, f_hbm, wg_ref, wu_ref, bg_ref, bu_ref, wd_ref, bd_ref, yb_hbm, xw, xs, hs, os_, sem_in,
     sem_out) = refs
    c = pl.program_id(0)
    j = pl.program_id(1)
    cur = jnp.maximum(c - 1, 0)
    units = jnp.where(c >= 1, cn_ref[cur], 0)
    nsub = (units + 1) // 2
    start = pl.multiple_of(cs_ref[cur], MOE_UNIT)

    nxt = jnp.minimum(c, cn_ref.shape[0] - 1)
    units_next = jnp.where(c < pl.num_programs(0) - 1, cn_ref[nxt], 0)
    fetch = jnp.logical_and(j < MOE_F_STEPS, 2 * j < units_next)

    def gather_issue():
        base = pl.multiple_of(j * MOE_SUB, MOE_SUB)
        for half, toks in enumerate((tok_ref, tok2_ref)):
            for r in range(MOE_UNIT):
                pltpu.make_async_copy(f_hbm.at[pl.ds(toks[0, r], 1)], xw.at[pl.ds(base + half * MOE_UNIT + r, 1)],
                                      sem_in.at[j]).start()

    def gather_wait(i):
        _row_copy_wait(f_hbm, xw.at[pl.ds(i * MOE_SUB, MOE_SUB)], sem_in.at[i], MOE_SUB)

    def unpack(i):
        rows = slice(i * MOE_SUB, (i + 1) * MOE_SUB)
        w = xw[rows, :]
        xs[rows, :HALF_D] = lax.bitcast_convert_type(w << 16, F32).astype(BF16)
        xs[rows, HALF_D:] = lax.bitcast_convert_type(w & jnp.uint32(0xFFFF0000), F32).astype(BF16)

    def copy_out(slot, off, n, col):
        return pltpu.make_async_copy(
            os_.at[slot, pl.ds(off, n * MOE_UNIT), :],
            yb_hbm.at[pl.ds(start + off, n * MOE_UNIT), pl.ds(col, MOE_TN)],
            sem_out.at[slot])

    def for_valid_subs(fn):
        fn(0)
        for i in range(1, MOE_NSUB):
            @pl.when(i < nsub)
            def _(i=i):
                fn(i)

    big = MOE_PIECES[0]

    def for_pieces(fn):
        n_big = units // big
        rem = units - big * n_big
        for q in range(MOE_CHUNK // (big * MOE_UNIT)):
            @pl.when(q < n_big)
            def _(q=q):
                fn(q * big * MOE_UNIT, big, q == 0)
        base = n_big * (big * MOE_UNIT)
        for size in MOE_PIECES[1:]:
            above = (rem // (2 * size)) * (2 * size)

            @pl.when((rem // size) % 2 == 1)
            def _(size=size, above=above):
                fn(pl.multiple_of(base + above * MOE_UNIT, MOE_UNIT), size, False)

    @pl.when(jnp.logical_and(c == 0, j == MOE_F_STEPS))
    def _():
        os_[0, 0:MOE_UNIT, :] = jnp.zeros((MOE_UNIT, MOE_TN), F32)
        tail_start = pl.multiple_of(tail_ref[0], MOE_UNIT)

        def tail_copy(u, t):
            return pltpu.make_async_copy(
                os_.at[0, pl.ds(0, MOE_UNIT), :],
                yb_hbm.at[pl.ds(tail_start + u * MOE_UNIT, MOE_UNIT), pl.ds(t * MOE_TN, MOE_TN)],
                sem_out.at[0])

        for action in ("start", "wait"):
            for u in range(N_EXPERTS):
                @pl.when(u < tail_ref[1])
                def _(u=u, action=action):
                    for t in range(MOE_N_STEPS):
                        getattr(tail_copy(u, t), action)()

    @pl.when(jnp.logical_and(nsub > 0, j == 0))
    def _():
        def land(i):
            gather_wait(i)
            unpack(i)

        for_valid_subs(land)

    @pl.when(jnp.logical_and(fetch, units < big))
    def _():
        gather_issue()

    @pl.when(nsub > 0)
    def _():
        @pl.when(j < MOE_F_STEPS)
        def _():
            bg = bg_ref[...]
            bu = bu_ref[...]

            def gate_up_piece(off, n):
                rows = pl.ds(off, n * MOE_UNIT)
                x = xs[rows, :]
                g = jnp.dot(x, wg_ref[...].astype(BF16), preferred_element_type=F32) + bg
                u = jnp.dot(x, wu_ref[...].astype(BF16), preferred_element_type=F32) + bu
                g = jnp.minimum(g, SWIGLU_LIMIT)
                u = jnp.clip(u, -SWIGLU_LIMIT, SWIGLU_LIMIT)
                h = (u + 1.0) * (g * jax.nn.sigmoid(SWIGLU_ALPHA * g))
                hs[j, rows, :] = h.astype(BF16)

            def gate_up(off, n, first_big):
                if not first_big:
                    gate_up_piece(off, n)
                    return

                @pl.when(fetch)
                def _():
                    gather_issue()
                    gate_up_piece(off, n)

                @pl.when(jnp.logical_not(fetch))
                def _():
                    gate_up_piece(off, n)

            for_pieces(gate_up)

        @pl.when(j >= MOE_F_STEPS)
        def _():
            jn = j - MOE_F_STEPS
            slot = jn % 2
            col = pl.multiple_of(jn * MOE_TN, MOE_TN)
            bd = bd_ref[...]

            @pl.when(jn >= 2)
            def _():
                for_pieces(lambda off, n, _: copy_out(slot, off, n, col).wait())

            def down(off, n, _):
                rows = pl.ds(off, n * MOE_UNIT)
                h = jnp.concatenate([hs[f, rows, :] for f in range(MOE_F_STEPS)], axis=1)
                os_[slot, rows, :] = jnp.dot(h, wd_ref[...].astype(BF16), preferred_element_type=F32) + bd
                copy_out(slot, off, n, col).start()

            for_pieces(down)

            @pl.when(jn == MOE_N_STEPS - 1)
            def _():
                for_pieces(lambda off, n, _: copy_out(1 - slot, off, n, col).wait())
                for_pieces(lambda off, n, _: copy_out(slot, off, n, col).wait())


def _moe_experts(layer, chunk_e, chunk_start, chunk_nsub, tail, used_chunks, buf_tok, f_packed, w_gate_up, b_gate_up,
                 w_down, b_down):
    n_chunks = chunk_e.shape[0]
    rows = buf_tok.shape[0]
    n_units = rows // MOE_UNIT

    def gu_idx(half):
        def idx(c, j, ce, cs, cn, tl):
            cur = jnp.maximum(c - 1, 0)
            jj = jnp.where(cn[cur] > 0, jnp.minimum(j, MOE_F_STEPS - 1), MOE_F_STEPS - 1)
            jj = jnp.where(c == 0, 0, jj)
            return (layer, ce[cur], 0, half * MOE_F_STEPS + jj)
        return idx

    def d_idx(c, j, ce, cs, cn, tl):
        cur = jnp.maximum(c - 1, 0)
        jj = jnp.where(cn[cur] > 0, jnp.maximum(j - MOE_F_STEPS, 0), MOE_N_STEPS - 1)
        jj = jnp.where(c == 0, 0, jj)
        return (layer, ce[cur], 0, jj)

    def tok_idx(half):
        def idx(c, j, ce, cs, cn, tl):
            nxt = jnp.minimum(c, n_chunks - 1)
            unit = cs[nxt] // MOE_UNIT + 2 * jnp.minimum(j, MOE_F_STEPS - 1) + half
            return (jnp.minimum(unit, n_units - 1), 0, 0)
        return idx

    grid_spec = pltpu.PrefetchScalarGridSpec(
        num_scalar_prefetch=4,
        grid=(used_chunks + 1, MOE_STEPS),
        in_specs=[
            pl.BlockSpec((None, 1, MOE_UNIT), tok_idx(0), memory_space=pltpu.SMEM),
            pl.BlockSpec((None, 1, MOE_UNIT), tok_idx(1), memory_space=pltpu.SMEM),
            pl.BlockSpec(memory_space=pl.ANY),
            pl.BlockSpec((None, None, D_MODEL, MOE_TF), gu_idx(0)),
            pl.BlockSpec((None, None, D_MODEL, MOE_TF), gu_idx(1)),
            pl.BlockSpec((None, None, 1, MOE_TF), gu_idx(0)),
            pl.BlockSpec((None, None, 1, MOE_TF), gu_idx(1)),
            pl.BlockSpec((None, None, D_FF, MOE_TN), d_idx),
            pl.BlockSpec((None, None, 1, MOE_TN), d_idx),
        ],
        out_specs=pl.BlockSpec(memory_space=pl.ANY),
        scratch_shapes=[
            pltpu.VMEM((MOE_CHUNK, HALF_D), jnp.uint32),
            pltpu.VMEM((MOE_CHUNK, D_MODEL), BF16),
            pltpu.VMEM((MOE_F_STEPS, MOE_CHUNK, MOE_TF), BF16),
            pltpu.VMEM((2, MOE_CHUNK, MOE_TN), F32),
            pltpu.SemaphoreType.DMA((MOE_NSUB,)),
            pltpu.SemaphoreType.DMA((2,)),
        ],
    )
    bgu = b_gate_up.reshape(DEPTH, N_EXPERTS, 1, 2 * D_FF)
    bd = b_down.reshape(DEPTH, N_EXPERTS, 1, D_MODEL)
    return pl.pallas_call(
        _moe_kernel,
        grid_spec=grid_spec,
        out_shape=jax.ShapeDtypeStruct((rows, D_MODEL), F32),
        compiler_params=pltpu.CompilerParams(
            dimension_semantics=("arbitrary", "arbitrary"), vmem_limit_bytes=VMEM_LIMIT),
        name="moe_experts",
    )(chunk_e, chunk_start, chunk_nsub, tail, *([buf_tok.reshape(n_units, 1, MOE_UNIT)] * 2), f_packed,
      w_gate_up, w_gate_up, bgu, bgu, w_down, bd)


CMB_TM = 256


def _combine_kernel(idx_ref, yb_hbm, gates_ref, x_ref, mod_ref, *rest, final):
    if final:
        g_ref, o_ref, buf, sem = rest
    else:
        o_ref, buf, sem = rest

    for k in range(TOP_K):
        def issue(r2, carry, k=k):
            for queue in range(2):
                r = 2 * r2 + queue
                pltpu.make_async_copy(yb_hbm.at[pl.ds(idx_ref[0, k * CMB_TM + r], 1)], buf.at[k, pl.ds(r, 1)],
                                      sem.at[k]).start(priority=queue)
            return carry

        lax.fori_loop(0, CMB_TM // 2, issue, 0, unroll=4)

    gates = gates_ref[...]
    y = None
    for k in range(TOP_K):
        _row_copy_wait(yb_hbm, buf.at[k], sem.at[k], CMB_TM)
        t = gates[:, k:k + 1] * buf[k]
        y = t if y is None else y + t
    xn = x_ref[...] + mod_ref[5:6, :] * y
    if final:
        xn = xn * lax.rsqrt(jnp.mean(xn * xn, axis=-1, keepdims=True) + EPS) * g_ref[...]
    o_ref[...] = xn


def _combine(yb, dest, gates, x_mid, mod, final_g, *, row_tile0, ctx_tiles):
    n = x_mid.shape[0]
    n_tiles = n // CMB_TM
    final = final_g is not None
    idx = dest.reshape(n_tiles, CMB_TM, TOP_K).transpose(0, 2, 1).reshape(n_tiles, 1, TOP_K * CMB_TM)
    in_specs = [
        pl.BlockSpec((None, 1, TOP_K * CMB_TM), lambda t: (t, 0, 0), memory_space=pltpu.SMEM),
        pl.BlockSpec(memory_space=pl.ANY),
        pl.BlockSpec((CMB_TM, TOP_K), lambda t: (t, 0)),
        pl.BlockSpec((CMB_TM, D_MODEL), lambda t: (t, 0)),
        pl.BlockSpec((None, 6, D_MODEL), lambda t: (_mod_row(t + row_tile0, CMB_TM, ctx_tiles), 0, 0)),
    ]
    args = [idx, yb, gates, x_mid, mod]
    if final:
        in_specs.append(pl.BlockSpec((1, D_MODEL), lambda t: (0, 0)))
        args.append(final_g.reshape(1, D_MODEL))
    return pl.pallas_call(
        functools.partial(_combine_kernel, final=final),
        grid=(n_tiles,),
        in_specs=in_specs,
        out_specs=pl.BlockSpec((CMB_TM, D_MODEL), lambda t: (t, 0)),
        out_shape=jax.ShapeDtypeStruct((n, D_MODEL), F32),
        scratch_shapes=[pltpu.VMEM((TOP_K, CMB_TM, D_MODEL), F32), pltpu.SemaphoreType.DMA((TOP_K,))],
        compiler_params=pltpu.CompilerParams(dimension_semantics=("arbitrary",), vmem_limit_bytes=VMEM_LIMIT),
        name="moe_combine",
    )(*args)


def _moe_ffn(layer, f_packed, logits, w_gate_up, b_gate_up, w_down, b_down):
    n = f_packed.shape[0]
    slots = n * TOP_K
    buf_rows = _moe_rows(n)
    n_chunks = _moe_chunks(n)

    top_logit, top_e = lax.top_k(logits, TOP_K)
    gates = jax.nn.softmax(top_logit, axis=-1)
    flat_e = top_e.reshape(-1).astype(jnp.int32)
    onehot = (flat_e[:, None] == jnp.arange(N_EXPERTS, dtype=jnp.int32)[None, :]).astype(jnp.int32)
    running = jnp.cumsum(onehot, axis=0)
    counts = running[-1]
    padded = (counts + MOE_UNIT - 1) // MOE_UNIT * MOE_UNIT
    pad_end = jnp.cumsum(padded)
    pad_start = pad_end - padded
    dest_of_slot = jnp.sum(onehot * (running - 1 + pad_start[None, :]), axis=1)
    buf_tok = jnp.zeros((buf_rows,), jnp.int32).at[dest_of_slot].set(
        jnp.arange(slots, dtype=jnp.int32) // TOP_K, unique_indices=True, mode='promise_in_bounds')

    e_chunks = (padded + MOE_CHUNK - 1) // MOE_CHUNK
    chunk_end = jnp.cumsum(e_chunks)
    total_chunks = chunk_end[-1]
    cidx = jnp.arange(n_chunks, dtype=jnp.int32)
    ce = jnp.minimum(jnp.searchsorted(chunk_end, cidx, side='right'), N_EXPERTS - 1).astype(jnp.int32)
    local = cidx - (chunk_end[ce] - e_chunks[ce])
    valid = cidx < total_chunks
    c_start = jnp.where(valid, pad_start[ce] + local * MOE_CHUNK, 0).astype(jnp.int32)
    c_nsub = jnp.where(valid, jnp.minimum(MOE_CHUNK, padded[ce] - local * MOE_CHUNK) // MOE_UNIT, 0).astype(jnp.int32)
    last_e = ce[jnp.maximum(total_chunks - 1, 0)]
    ce = jnp.where(valid, ce, last_e).astype(jnp.int32)

    tail = jnp.stack([pad_end[-1], (buf_rows - pad_end[-1]) // MOE_UNIT]).astype(jnp.int32)
    yb = _moe_experts(layer, ce, c_start, c_nsub, tail, total_chunks.astype(jnp.int32), buf_tok, f_packed,
                      w_gate_up, b_gate_up, w_down, b_down)
    return yb, dest_of_slot.reshape(n, TOP_K), gates


MIX_TB = 256
CONV_HALO = 16
CONV_PIECE = 64
NEG_INF = float("-inf")


def _conv_pitch(rowlen):
    return rowlen + 2 * CONV_HALO


def _dwconv_block(u, pad_ref, w_ref, rowlen):
    taps = w_ref.shape[0]
    half = taps // 2
    pitch = _conv_pitch(rowlen)
    ch = u.shape[1]
    zeros = jnp.zeros((CONV_HALO, ch), F32)
    for r in range(MIX_TB // rowlen):
        base = r * pitch
        pad_ref[base:base + CONV_HALO, :] = zeros
        pad_ref[base + CONV_HALO:base + CONV_HALO + rowlen, :] = u[r * rowlen:(r + 1) * rowlen, :]
        pad_ref[base + CONV_HALO + rowlen:base + pitch, :] = zeros
    outs = []
    for r in range(MIX_TB // rowlen):
        for piece in range(rowlen // CONV_PIECE):
            acc = None
            for j in range(taps):
                off = r * pitch + CONV_HALO + piece * CONV_PIECE + j - half
                term = pad_ref[off:off + CONV_PIECE, :] * w_ref[j:j + 1, :]
                acc = term if acc is None else acc + term
            outs.append(acc)
    return jnp.concatenate(outs, axis=0)


def _ln_rows(x, g, b):
    xc = x - jnp.mean(x, axis=-1, keepdims=True)
    return xc * lax.rsqrt(jnp.mean(xc * xc, axis=-1, keepdims=True) + EPS) * g + b


def _silu(x):
    return x * jax.nn.sigmoid(x)


SSD_PAIRS = SSD_HEADS // 2
SSD_PAIR_W = 2 * SSD_HEAD_DIM


def _ssd_kernel(xbc_f, dt_f, xbc_b, dt_b, init_ref, cw_ref, cb_ref, dtb_ref, a_ref, skip_ref, y_f, y_b, fin_ref,
                pad_f, pad_b, st_ref, *, rowlen):
    s = pl.program_id(1)

    @pl.when(s == 0)
    def _():
        st_ref[...] = init_ref[...]

    for d, (xbc_ref, dt_ref, y_ref, pad_ref) in enumerate(((xbc_f, dt_f, y_f, pad_f), (xbc_b, dt_b, y_b, pad_b))):
        _ssd_direction(xbc_ref, dt_ref, cw_ref, cb_ref, dtb_ref, a_ref, skip_ref[d:d + 1, :], y_ref, pad_ref,
                       st_ref.at[d], rowlen=rowlen, rev=d == 1, dcol=d * SSD_HEADS)

    @pl.when(s == pl.num_programs(1) - 1)
    def _():
        fin_ref[...] = st_ref[...]


def _ssd_direction(xbc_ref, dt_ref, cw_ref, cb_ref, dtb_ref, a_ref, skip_row, y_ref, pad_ref, st_ref, *, rowlen, rev,
                   dcol):
    xa = _silu(_dwconv_block(xbc_ref[...], pad_ref, cw_ref, rowlen) + cb_ref[...])
    dt_all = dt_ref[...] + dtb_ref[...]
    dt_all = jnp.maximum(dt_all, 0.0) + jnp.log1p(jnp.exp(-jnp.abs(dt_all)))
    da_all = dt_all * a_ref[...]

    row_i = lax.broadcasted_iota(jnp.int32, (SSD_CHUNK, SSD_CHUNK), 0)
    col_i = lax.broadcasted_iota(jnp.int32, (SSD_CHUNK, SSD_CHUNK), 1)
    tri = (col_i >= row_i) if rev else (col_i <= row_i)
    tri_f = tri.astype(F32)
    first_half = lax.broadcasted_iota(jnp.int32, (SSD_CHUNK, SSD_PAIR_W), 1) < SSD_HEAD_DIM
    first_half_row = first_half[0:1, :]

    chunks = range(MIX_TB // SSD_CHUNK)
    for ci in (reversed(chunks) if rev else chunks):
        rows = slice(ci * SSD_CHUNK, (ci + 1) * SSD_CHUNK)
        acc = jnp.dot(tri_f, da_all[rows, :], preferred_element_type=F32, precision=lax.Precision.HIGHEST)
        acc_t = acc.T
        tot = acc[0:1, :] if rev else acc[SSD_CHUNK - 1:SSD_CHUNK, :]
        to_end = jnp.exp(tot - acc)
        from_start = jnp.exp(acc)
        chunk_decay = jnp.exp(tot)
        dt_c = dt_all[rows, :]
        for g in range(SSD_GROUPS):
            bg = xa[rows, GROUP_W + g * SSD_STATE:GROUP_W + (g + 1) * SSD_STATE]
            cg = xa[rows, GROUP_W + (SSD_GROUPS + g) * SSD_STATE:GROUP_W + (SSD_GROUPS + g + 1) * SSD_STATE]
            scores = lax.dot_general(cg.astype(BF16), bg.astype(BF16), (((1,), (1,)), ((), ())),
                                     preferred_element_type=F32)
            pairs_per_group = SSD_PAIRS // SSD_GROUPS
            for p in range(g * pairs_per_group, (g + 1) * pairs_per_group):
                c0 = dcol + 2 * p
                c1 = c0 + 1
                lhs, bw = [], []
                for col in (c0, c1):
                    seg = acc[:, col:col + 1] - acc_t[col:col + 1, :]
                    lhs.append(scores * jnp.exp(jnp.where(tri, seg, NEG_INF)))
                for col in (c0, c1):
                    lhs.append(cg * from_start[:, col:col + 1])
                    bw.append((bg * to_end[:, col:col + 1]).T)
                xp = xa[rows, p * SSD_PAIR_W:(p + 1) * SSD_PAIR_W]
                xd = xp * jnp.where(first_half, dt_c[:, c0:c0 + 1], dt_c[:, c1:c1 + 1])
                xd_top = jnp.where(first_half, xd, 0.0).astype(BF16)
                xd_bot = jnp.where(first_half, 0.0, xd).astype(BF16)
                st = st_ref[p]
                st_top = jnp.where(first_half, st, 0.0).astype(BF16)
                st_bot = jnp.where(first_half, 0.0, st).astype(BF16)
                y = jnp.dot(jnp.concatenate(lhs, axis=1).astype(BF16),
                            jnp.concatenate([xd_top, xd_bot, st_top, st_bot], axis=0),
                            preferred_element_type=F32)
                y_ref[rows, p * SSD_PAIR_W:(p + 1) * SSD_PAIR_W] = y + skip_row[:, p * SSD_PAIR_W:(p + 1) * SSD_PAIR_W] * xp
                upd = jnp.dot(jnp.concatenate(bw, axis=1).astype(BF16), jnp.concatenate([xd_top, xd_bot], axis=0),
                              preferred_element_type=F32)
                decay_lane = jnp.where(first_half_row, chunk_decay[:, c0:c0 + 1], chunk_decay[:, c1:c1 + 1])
                st_ref[p] = st * decay_lane + upd


def _ssd_sweep(xbc_src, xbc_col_block, dt_src, row_block0, n_blocks, init, prm, *, rowlen):
    cw, cb, dtb_row, a_row, skip_rows = prm

    def fwd(b, s):
        return b * n_blocks + s

    def bwd(b, s):
        return b * n_blocks + (n_blocks - 1 - s)

    state_block = (None, 2, SSD_PAIRS, SSD_STATE, SSD_PAIR_W)
    y_shape = jax.ShapeDtypeStruct((BATCH * n_blocks * MIX_TB, GROUP_W), F32)
    pad_shape = pltpu.VMEM(((MIX_TB // rowlen) * _conv_pitch(rowlen), SSD_XBC), F32)
    return pl.pallas_call(
        functools.partial(_ssd_kernel, rowlen=rowlen),
        grid=(BATCH, n_blocks),
        in_specs=[
            pl.BlockSpec((MIX_TB, SSD_XBC), lambda b, s: (row_block0 + fwd(b, s), xbc_col_block)),
            pl.BlockSpec((MIX_TB, LANE), lambda b, s: (row_block0 + fwd(b, s), 0)),
            pl.BlockSpec((MIX_TB, SSD_XBC), lambda b, s: (row_block0 + bwd(b, s), xbc_col_block)),
            pl.BlockSpec((MIX_TB, LANE), lambda b, s: (row_block0 + bwd(b, s), 0)),
            pl.BlockSpec(state_block, lambda b, s: (b, 0, 0, 0, 0)),
            pl.BlockSpec((3, SSD_XBC), lambda b, s: (0, 0)),
            pl.BlockSpec((1, SSD_XBC), lambda b, s: (0, 0)),
            pl.BlockSpec((1, LANE), lambda b, s: (0, 0)),
            pl.BlockSpec((1, LANE), lambda b, s: (0, 0)),
            pl.BlockSpec((2, GROUP_W), lambda b, s: (0, 0)),
        ],
        out_specs=[
            pl.BlockSpec((MIX_TB, GROUP_W), lambda b, s: (fwd(b, s), 0)),
            pl.BlockSpec((MIX_TB, GROUP_W), lambda b, s: (bwd(b, s), 0)),
            pl.BlockSpec(state_block, lambda b, s: (b, 0, 0, 0, 0)),
        ],
        out_shape=[y_shape, y_shape, jax.ShapeDtypeStruct((BATCH, 2, SSD_PAIRS, SSD_STATE, SSD_PAIR_W), F32)],
        scratch_shapes=[pad_shape, pad_shape, pltpu.VMEM((2, SSD_PAIRS, SSD_STATE, SSD_PAIR_W), F32)],
        compiler_params=pltpu.CompilerParams(
            dimension_semantics=("arbitrary", "arbitrary"), vmem_limit_bytes=VMEM_LIMIT),
        name="ssd_sweep",
    )(xbc_src, dt_src, xbc_src, dt_src, init, cw, cb, dtb_row, a_row, skip_rows)


def _ssd_params(conv_w, conv_b, dt_bias, a_log, d_skip):
    pad = LANE - 2 * SSD_HEADS
    dtb_row = jnp.pad(dt_bias.reshape(-1), (0, pad)).reshape(1, LANE)
    a_row = jnp.pad(-jnp.exp(a_log.reshape(-1)), (0, pad)).reshape(1, LANE)
    skip_rows = jnp.repeat(d_skip, SSD_HEAD_DIM, axis=1)
    return conv_w, conv_b.reshape(1, SSD_XBC), dtb_row, a_row, skip_rows


def _local_kernel(p_ref, z_ref, y0_ref, y1_ref, scw_ref, cfw_ref, cfb_ref, cfg_ref, cfbeta_ref, sgg_ref,
                  sgbeta_ref, sgw_ref, sgb_ref, ng_ref, m_ref, pad_ref, *, rowlen):
    gate_b = p_ref[:, 0:GROUP_W]
    u = p_ref[:, GROUP_W:2 * GROUP_W] * p_ref[:, 2 * GROUP_W:3 * GROUP_W]
    m_ref[:, 0:GROUP_W] = gate_b * _dwconv_block(u, pad_ref, scw_ref, rowlen)
    u = p_ref[:, OFF_CF:OFF_CF + GROUP_W] * jax.nn.sigmoid(p_ref[:, OFF_CF + GROUP_W:OFF_SG])
    u = _dwconv_block(u, pad_ref, cfw_ref, rowlen) + cfb_ref[...]
    m_ref[:, GROUP_W:2 * GROUP_W] = _silu(_ln_rows(u, cfg_ref[...], cfbeta_ref[...]))
    q = p_ref[:, OFF_SG:OFF_SSD]
    q = 0.5 * q * (1.0 + lax.erf(q * (2.0 ** -0.5)))
    v = _ln_rows(q[:, GROUP_W:], sgg_ref[...], sgbeta_ref[...]).astype(BF16)
    n_chunks = MIX_TB // SG_CHUNK
    for h in range(SG_HEADS):
        cols = slice(h * SG_HEAD_DIM, (h + 1) * SG_HEAD_DIM)
        rhs = jnp.concatenate([v[c * SG_CHUNK:(c + 1) * SG_CHUNK, cols] for c in range(n_chunks)], axis=1)
        sres = jnp.dot(sgw_ref[h].astype(BF16), rhs, preferred_element_type=F32) + sgb_ref[:, h:h + 1]
        for c in range(n_chunks):
            rows = slice(c * SG_CHUNK, (c + 1) * SG_CHUNK)
            m_ref[rows, 2 * GROUP_W + h * SG_HEAD_DIM:2 * GROUP_W + (h + 1) * SG_HEAD_DIM] = (
                q[rows, cols] * sres[:, c * SG_HEAD_DIM:(c + 1) * SG_HEAD_DIM])
    yv = (y0_ref[...] + y1_ref[...]) * _silu(z_ref[...])
    gw = GROUP_W // SSD_GROUPS
    for g in range(SSD_GROUPS):
        vg = yv[:, g * gw:(g + 1) * gw]
        vg = vg * lax.rsqrt(jnp.mean(vg * vg, axis=-1, keepdims=True) + EPS)
        m_ref[:, 3 * GROUP_W + g * gw:3 * GROUP_W + (g + 1) * gw] = vg * ng_ref[:, g * gw:(g + 1) * gw]


def _local_mixers_call(p_src, z_col_block, row_block0, n_blocks, y0, y1, lw, *, rowlen):
    vec = lambda a: a.reshape(1, GROUP_W)
    args = [lw['sc_conv_w'], lw['cf_conv_w'], vec(lw['cf_conv_b']), vec(lw['cf_ln_g']), vec(lw['cf_ln_b']),
            vec(lw['sg_ln_g']), vec(lw['sg_ln_b']), lw['sg_w'], lw['sg_b'].T, vec(lw['ssd_norm_g'])]
    full = lambda a: pl.BlockSpec(a.shape, lambda t, nd=a.ndim: (0,) * nd)
    return pl.pallas_call(
        functools.partial(_local_kernel, rowlen=rowlen),
        grid=(n_blocks,),
        in_specs=[
            pl.BlockSpec((MIX_TB, OFF_SSD), lambda t: (t + row_block0, 0)),
            pl.BlockSpec((MIX_TB, GROUP_W), lambda t: (t + row_block0, z_col_block)),
            pl.BlockSpec((MIX_TB, GROUP_W), lambda t: (t, 0)),
            pl.BlockSpec((MIX_TB, GROUP_W), lambda t: (t, 0)),
        ] + [full(a) for a in args],
        out_specs=pl.BlockSpec((MIX_TB, D_MODEL), lambda t: (t, 0)),
        out_shape=jax.ShapeDtypeStruct((n_blocks * MIX_TB, D_MODEL), F32),
        scratch_shapes=[pltpu.VMEM(((MIX_TB // rowlen) * _conv_pitch(rowlen), GROUP_W), F32)],
        compiler_params=pltpu.CompilerParams(dimension_semantics=("arbitrary",), vmem_limit_bytes=VMEM_LIMIT),
        name="local_mixers",
    )(p_src, p_src, y0, y1, *args)


def _token_mixers(p_ctx_src, ctx_cols, p_lat_src, lat_cols, dt_ctx, dt_lat, lat_row_block0, lw, ctx_out):
    prm = _ssd_params(lw['ssd_conv_w'], lw['ssd_conv_b'], lw['ssd_dt_bias'], lw['ssd_a_log'], lw['ssd_d'])
    zero_state = jnp.zeros((BATCH, 2, SSD_PAIRS, SSD_STATE, SSD_PAIR_W), F32)
    ctx_blocks = CTX_LEN // MIX_TB
    lat_blocks = SEQ // MIX_TB
    *y_ctx, state = _ssd_sweep(p_ctx_src, ctx_cols[1], dt_ctx, 0, ctx_blocks, zero_state, prm, rowlen=CTX_LEN)
    *y_lat, _ = _ssd_sweep(p_lat_src, lat_cols[1], dt_lat, lat_row_block0, lat_blocks, state, prm, rowlen=GRID_W)
    m_lat = _local_mixers_call(p_lat_src, lat_cols[0], lat_row_block0, BATCH * lat_blocks, y_lat[0], y_lat[1], lw,
                               rowlen=GRID_W)
    m_ctx = None
    if ctx_out:
        m_ctx = _local_mixers_call(p_ctx_src, ctx_cols[0], 0, BATCH * ctx_blocks, y_ctx[0], y_ctx[1], lw,
                                   rowlen=CTX_LEN)
    return m_lat, m_ctx


def kernel(x, c, ctx, c_ctx, w_mod, b_mod, norm1_g, norm2_g, w_in, b_in, sc_conv_w, cf_conv_w, cf_conv_b,
           cf_ln_g, cf_ln_b, sg_ln_g, sg_ln_b, sg_w, sg_b, ssd_conv_w, ssd_conv_b, ssd_dt_bias, ssd_a_log,
           ssd_d, ssd_norm_g, w_out, b_out, w_router, b_router, w_gate_up, b_gate_up, w_down, b_down,
           final_norm_g):
    cc = jnp.concatenate([c_ctx[None, :], c, jnp.zeros((MOD_ROWS - 1 - BATCH, D_MODEL), F32)], axis=0)
    mod_all = _modulation(cc, w_mod, b_mod).reshape(DEPTH, MOD_ROWS, 6, D_MODEL)

    assert DEPTH == 2
    x_ctx0 = ctx.reshape(N_CTX, D_MODEL)
    x_lat0 = x.reshape(N_LAT, D_MODEL)
    x_all = None

    for i in range(DEPTH):
        last = i == DEPTH - 1
        lw = dict(sc_conv_w=sc_conv_w[i], cf_conv_w=cf_conv_w[i], cf_conv_b=cf_conv_b[i], cf_ln_g=cf_ln_g[i],
                  cf_ln_b=cf_ln_b[i], sg_ln_g=sg_ln_g[i], sg_ln_b=sg_ln_b[i], sg_w=sg_w[i], sg_b=sg_b[i],
                  ssd_conv_w=ssd_conv_w[i], ssd_conv_b=ssd_conv_b[i], ssd_dt_bias=ssd_dt_bias[i],
                  ssd_a_log=ssd_a_log[i], ssd_d=ssd_d[i], ssd_norm_g=ssd_norm_g[i])
        mod = mod_all[i]
        w_in_bf = w_in[i, :, :MAIN_COLS].astype(BF16)
        b_in_main = b_in[i, :MAIN_COLS].reshape(1, MAIN_COLS)
        wdt_bf = jnp.pad(w_in[i, :, MAIN_COLS:], ((0, 0), (0, LANE - DT_COLS))).astype(BF16)
        bdt = jnp.pad(b_in[i, MAIN_COLS:], (0, LANE - DT_COLS)).reshape(1, LANE)
        w_out_bf = w_out[i].astype(BF16)
        wr_bf = jnp.pad(w_router[i], ((0, 0), (0, LANE - N_EXPERTS))).astype(BF16)
        br = jnp.pad(b_router[i], (0, LANE - N_EXPERTS)).reshape(1, LANE)
        in_ctx_tiles = N_CTX // IN_TM
        proj_args = (norm1_g[i], mod, w_in_bf, b_in_main, wdt_bf, bdt)

        z_xbc_cols = (OFF_SSD // GROUP_W, (OFF_SSD + GROUP_W) // SSD_XBC)
        if not last:
            all_cols = dict(col_tile0=0, n_col_tiles=MAIN_COLS // IN_TN, ctx_tiles=in_ctx_tiles)
            p_ctx, dt_ctx = _in_projection(x_ctx0, *proj_args, row_tile0=0, n_row_tiles=in_ctx_tiles, **all_cols)
            p_lat, dt_lat = _in_projection(x_lat0, *proj_args, row_tile0=0, n_row_tiles=N_LAT // IN_TM,
                                           mod_tile0=in_ctx_tiles, **all_cols)
            m_lat, m_ctx = _token_mixers(p_ctx, z_xbc_cols, p_lat, z_xbc_cols, dt_ctx, dt_lat, 0, lw, True)
        else:
            proj = functools.partial(_in_projection, x_all, *proj_args, ctx_tiles=in_ctx_tiles)
            p_lat, dt_lat = proj(row_tile0=in_ctx_tiles, n_row_tiles=N_LAT // IN_TM,
                                 col_tile0=0, n_col_tiles=MAIN_COLS // IN_TN)
            ctx_col0 = MAIN_COLS - 2 * SSD_XBC
            p_ctx, dt_ctx = proj(row_tile0=0, n_row_tiles=in_ctx_tiles,
                                 col_tile0=ctx_col0 // IN_TN, n_col_tiles=(MAIN_COLS - ctx_col0) // IN_TN)
            ctx_cols = ((OFF_SSD - ctx_col0) // GROUP_W, (OFF_SSD + GROUP_W - ctx_col0) // SSD_XBC)
            m_lat, m_ctx = _token_mixers(p_ctx, ctx_cols, p_lat, z_xbc_cols, dt_ctx, dt_lat, 0, lw, False)
        out_ctx_tiles = N_CTX // OUT_TM
        row_tile0 = out_ctx_tiles if last else 0
        x_mid, f_packed, logits = _out_projection(m_ctx, m_lat, None if last else x_ctx0,
                                                  x_all if last else x_lat0, mod, norm2_g[i], w_out_bf,
                                                  b_out[i].reshape(1, D_MODEL), wr_bf, br,
                                                  row_tile0=row_tile0, ctx_tiles=out_ctx_tiles)
        yb, dest, gates = _moe_ffn(i, f_packed, logits[:, :N_EXPERTS], w_gate_up, b_gate_up, w_down, b_down)
        x_all = _combine(yb, dest, gates, x_mid, mod, final_norm_g if last else None,
                         row_tile0=row_tile0 * OUT_TM // CMB_TM, ctx_tiles=N_CTX // CMB_TM)

    return x_all.reshape(BATCH, SEQ, D_MODEL)
```
